```python
import math
import jax, jax.numpy as jnp
from jax import lax
import numpy as np

D_MODEL = 1024
BATCH = 2
SEQ = 8192
DEPTH = 1

CHUNK = 64
EPS = 1e-6
A_HEAD_DIM = 128
A_WIDTH = D_MODEL // 2
A_HEADS = A_WIDTH // A_HEAD_DIM
CONV_K = 4
B_HEAD_DIM = 64
B_WIDTH = D_MODEL - A_WIDTH
B_HEADS = B_WIDTH // B_HEAD_DIM
B_PREV_CHUNKS = 8
B_BAND = (B_PREV_CHUNKS + 1) * CHUNK
REL_CLIP = 128
MIX_WIDTH = A_WIDTH + B_WIDTH
OFF_A_QKV = 0
OFF_A_Z = 3 * A_WIDTH
OFF_A_ALPHA = 4 * A_WIDTH
OFF_A_BETA = OFF_A_ALPHA + A_HEADS
OFF_B_QKV = OFF_A_BETA + A_HEADS
IN_COLS = OFF_B_QKV + 3 * B_WIDTH
N_EXPERTS = 32
TOP_K = 4
D_EXPERT = D_MODEL
SWIGLU_ALPHA = 1.702
SWIGLU_LIMIT = 7.0
MOE_BLOCK = 256

kernel_name = 'hybrid_deltanet_chunkattn_moe_block'


def rms_norm(x, w):
    xf = x.astype(jnp.float32)
    y = xf * lax.rsqrt(jnp.mean(xf * xf, axis=-1, keepdims=True) + EPS)
    return (y * w.astype(jnp.float32)).astype(x.dtype)


def l2_norm(x):
    xf = x.astype(jnp.float32)
    return xf * lax.rsqrt(jnp.sum(xf * xf, axis=-1, keepdims=True) + EPS)


def causal_depthwise_conv(x, w):
    ch = x.shape[-1]
    return lax.conv_general_dilated(x, w[:, None, :].astype(x.dtype), window_strides=(1,),
                                    padding=[(CONV_K - 1, 0)],
                                    dimension_numbers=('NWC', 'WIO', 'NWC'),
                                    feature_group_count=ch)


def gated_delta_rule(q, k, v, g, beta):
    bsz, t_len, n_h, dk = q.shape
    dv = v.shape[-1]
    n_ch = t_len // CHUNK

    def to_chunks(t):
        t = t.astype(jnp.float32).reshape((bsz, n_ch, CHUNK, n_h) + t.shape[3:])
        return jnp.moveaxis(t, 3, 1)

    q = to_chunks(q) * (dk ** -0.5)
    k = to_chunks(k)
    v = to_chunks(v)
    beta = to_chunks(beta)
    g = jnp.cumsum(to_chunks(g), axis=-1)
    causal = jnp.tril(jnp.ones((CHUNK, CHUNK), dtype=bool))
    decay = jnp.exp(jnp.where(causal, g[..., :, None] - g[..., None, :], -jnp.inf))
    k_beta = k * beta[..., None]
    strict = jnp.tril(jnp.ones((CHUNK, CHUNK), jnp.float32), -1)
    lower = jnp.einsum('bhnid,bhnjd->bhnij', k_beta, k) * decay * strict
    eye = jnp.eye(CHUNK, dtype=jnp.float32)
    tmat = lax.linalg.triangular_solve(lower + eye, jnp.broadcast_to(eye, lower.shape),
                                       left_side=True, lower=True, unit_diagonal=True)
    u = tmat @ (v * beta[..., None])
    w = tmat @ (k_beta * jnp.exp(g)[..., None])
    attn = jnp.einsum('bhnid,bhnjd->bhnij', q, k) * decay
    q_dec = q * jnp.exp(g)[..., None]
    g_last = g[..., -1]
    k_dec = k * jnp.exp(g_last[..., None] - g)[..., None]

    def step(state, inp):
        u_n, w_n, attn_n, q_n, k_n, gl_n = inp
        v_new = u_n - w_n @ state
        out = q_n @ state + attn_n @ v_new
        state = state * jnp.exp(gl_n)[..., None, None] + jnp.swapaxes(k_n, -1, -2) @ v_new
        return state, out

    xs = tuple(jnp.moveaxis(t, 2, 0) for t in (u, w, attn, q_dec, k_dec, g_last))
    state0 = jnp.zeros((bsz, n_h, dk, dv), jnp.float32)
    _, out = lax.scan(step, state0, xs)
    return out.transpose(1, 0, 3, 2, 4).reshape(bsz, t_len, n_h, dv)


def chunk_band_attention(q, k, v, rel_bias):
    bsz, t_len, n_h, hd = q.shape
    n_ch = t_len // CHUNK
    qc = q.reshape(bsz, n_ch, CHUNK, n_h, hd)

    def band(t):
        t = t.reshape(bsz, n_ch, CHUNK, n_h, hd)
        t = jnp.pad(t, ((0, 0), (B_PREV_CHUNKS, 0), (0, 0), (0, 0), (0, 0)))
        return jnp.concatenate([t[:, s:s + n_ch] for s in range(B_PREV_CHUNKS + 1)], axis=2)

    kb, vb = band(k), band(v)
    q_off = jnp.arange(CHUNK)[:, None] + B_PREV_CHUNKS * CHUNK
    k_off = jnp.arange(B_BAND)[None, :]
    idx = jnp.clip(q_off - k_off, -REL_CLIP, REL_CLIP) + REL_CLIP
    bias = rel_bias[:, idx].astype(jnp.float32)
    valid = (jnp.arange(n_ch)[:, None] - B_PREV_CHUNKS + k_off // CHUNK) >= 0
    s = jnp.einsum('bnqhd,bnkhd->bnhqk', qc, kb).astype(jnp.float32) * (hd ** -0.5) + bias
    s = jnp.where(valid[None, :, None, None, :], s, -jnp.inf)
    p = jax.nn.softmax(s, axis=-1).astype(v.dtype)
    o = jnp.einsum('bnhqk,bnkhd->bnqhd', p, vb)
    return o.reshape(bsz, t_len, n_h * hd)


def hybrid_mixer(h, w_in, conv_w, a_log, dt_bias, a_norm_w, rel_bias, w_out):
    bsz, t_len, _ = h.shape
    proj = h @ w_in
    qkv_a = jax.nn.silu(causal_depthwise_conv(proj[..., OFF_A_QKV:OFF_A_Z], conv_w))
    q_a, k_a, v_a = [t.reshape(bsz, t_len, A_HEADS, A_HEAD_DIM) for t in jnp.split(qkv_a, 3, axis=-1)]
    z_a = proj[..., OFF_A_Z:OFF_A_ALPHA].reshape(bsz, t_len, A_HEADS, A_HEAD_DIM)
    g = -jnp.exp(a_log.astype(jnp.float32)) * jax.nn.softplus(
        proj[..., OFF_A_ALPHA:OFF_A_BETA].astype(jnp.float32) + dt_bias.astype(jnp.float32))
    beta = jax.nn.sigmoid(proj[..., OFF_A_BETA:OFF_B_QKV].astype(jnp.float32))
    o_a = gated_delta_rule(l2_norm(q_a), l2_norm(k_a), v_a, g, beta)
    o_a = (rms_norm(o_a, a_norm_w) * jax.nn.silu(z_a.astype(jnp.float32))).astype(h.dtype)
    o_a = o_a.reshape(bsz, t_len, A_WIDTH)
    q_b, k_b, v_b = [t.reshape(bsz, t_len, B_HEADS, B_HEAD_DIM)
                     for t in jnp.split(proj[..., OFF_B_QKV:], 3, axis=-1)]
    o_b = chunk_band_attention(q_b, k_b, v_b, rel_bias).astype(h.dtype)
    return jnp.concatenate([o_a, o_b], axis=-1) @ w_out


def moe_ffn(h, w_router, b_router, w_up, b_up, w_down, b_down):
    bsz, t_len, d = h.shape
    x = h.reshape(-1, d)
    n_tok = x.shape[0]
    logits = (x @ w_router + b_router).astype(jnp.float32)
    top_val, top_idx = lax.top_k(logits, TOP_K)
    gates = jax.nn.softmax(top_val, axis=-1)
    n_assign = n_tok * TOP_K
    flat_e = top_idx.reshape(-1)
    flat_tok = jnp.repeat(jnp.arange(n_tok, dtype=jnp.int32), TOP_K)
    flat_g = gates.reshape(-1)
    order = jnp.argsort(flat_e, stable=True)
    sorted_e = flat_e[order]
    counts = jnp.bincount(flat_e, length=N_EXPERTS)
    padded = (counts + MOE_BLOCK - 1) // MOE_BLOCK * MOE_BLOCK
    start = jnp.cumsum(counts) - counts
    pend = jnp.cumsum(padded)
    pstart = pend - padded
    dest = pstart[sorted_e] + (jnp.arange(n_assign) - start[sorted_e])
    n_blocks = -(-n_assign // MOE_BLOCK) + N_EXPERTS
    cap = n_blocks * MOE_BLOCK
    slot_tok = jnp.zeros((cap,), jnp.int32).at[dest].set(flat_tok[order])
    slot_g = jnp.zeros((cap,), jnp.float32).at[dest].set(flat_g[order])
    block_e = jnp.minimum(jnp.searchsorted(pend, jnp.arange(n_blocks) * MOE_BLOCK, side='right'),
                          N_EXPERTS - 1)
    xb = x[slot_tok].reshape(n_blocks, MOE_BLOCK, d)

    def expert_block(args):
        xe, e = args
        hu = xe @ w_up[e] + b_up[e]
        glu = jnp.minimum(hu[:, :D_EXPERT], SWIGLU_LIMIT)
        lin = jnp.clip(hu[:, D_EXPERT:], -SWIGLU_LIMIT, SWIGLU_LIMIT)
        act = glu * jax.nn.sigmoid(SWIGLU_ALPHA * glu) * (lin + 1)
        return act @ w_down[e] + b_down[e]

    yb = lax.map(expert_block, (xb, block_e))
    y = yb.reshape(cap, d) * slot_g[:, None].astype(yb.dtype)
    out = jnp.zeros((n_tok, d), y.dtype).at[slot_tok].add(y)
    return out.reshape(bsz, t_len, d)


def setup_inputs(seed: int = 0) -> dict:
    key = jax.random.key(seed)
    ks = jax.random.split(key, 18)
    f32 = jnp.float32
    L, D, E, F = DEPTH, D_MODEL, N_EXPERTS, D_EXPERT

    def nrm(k, shape, s):
        return jax.random.normal(k, shape, f32) * s

    dt = jnp.exp(jax.random.uniform(ks[8], (L, A_HEADS), f32, math.log(1e-3), math.log(1e-1)))
    return {
        'x': nrm(ks[0], (BATCH, SEQ, D), 1.0),
        'c': nrm(ks[1], (BATCH, D), 1.0),
        'w_ada': nrm(ks[2], (L, D, 6 * D), 0.5 * D ** -0.5),
        'b_ada': nrm(ks[3], (L, 6 * D), 0.02),
        'norm_w': 1.0 + nrm(ks[4], (L, 4, D), 0.02),
        'w_in': nrm(ks[5], (L, D, IN_COLS), D ** -0.5),
        'conv_w': nrm(ks[6], (L, CONV_K, 3 * A_WIDTH), CONV_K ** -0.5),
        'a_log': jnp.log(jax.random.uniform(ks[7], (L, A_HEADS), f32, 1.0, 16.0)),
        'dt_bias': dt + jnp.log(-jnp.expm1(-dt)),
        'a_norm_w': 1.0 + nrm(ks[9], (L, A_HEAD_DIM), 0.02),
        'rel_bias': nrm(ks[10], (L, B_HEADS, 2 * REL_CLIP + 1), 0.1),
        'w_out': nrm(ks[11], (L, MIX_WIDTH, D), MIX_WIDTH ** -0.5),
        'w_router': nrm(ks[12], (L, D, E), D ** -0.5),
        'b_router': nrm(ks[13], (L, E), 0.01),
        'w_up': nrm(ks[14], (L, E, D, 2 * F), D ** -0.5),
        'b_up': nrm(ks[15], (L, E, 2 * F), 0.01),
        'w_down': nrm(ks[16], (L, E, F, D), F ** -0.5),
        'b_down': nrm(ks[17], (L, E, D), 0.01),
    }


def reference(x, c, w_ada, b_ada, norm_w, w_in, conv_w, a_log, dt_bias, a_norm_w, rel_bias, w_out,
              w_router, b_router, w_up, b_up, w_down, b_down):
    for l in range(DEPTH):
        mod = jax.nn.silu(c) @ w_ada[l] + b_ada[l]
        sh1, sc1, ga1, sh2, sc2, ga2 = [m[:, None, :] for m in jnp.split(mod, 6, axis=-1)]
        h = rms_norm(x, norm_w[l, 0]) * (1 + sc1) + sh1
        y = hybrid_mixer(h, w_in[l], conv_w[l], a_log[l], dt_bias[l], a_norm_w[l], rel_bias[l], w_out[l])
        x = x + ga1 * rms_norm(y, norm_w[l, 1])
        h = rms_norm(x, norm_w[l, 2]) * (1 + sc2) + sh2
        y = moe_ffn(h, w_router[l], b_router[l], w_up[l], b_up[l], w_down[l], b_down[l])
        x = x + ga2 * rms_norm(y, norm_w[l, 3])
    return x
```

```python
import functools
import math

import jax
import jax.numpy as jnp
from jax import lax
from jax.experimental import pallas as pl
from jax.experimental.pallas import tpu as pltpu

F32 = jnp.float32
BF16 = jnp.bfloat16
HIGHEST = lax.Precision.HIGHEST

EPS = 1e-6
CHUNK = 64
CONV_K = 4
A_HEAD_DIM = 128
B_HEAD_DIM = 64
B_PREV_CHUNKS = 8
REL_CLIP = 128
TOP_K = 4
SWIGLU_ALPHA = 1.702
SWIGLU_LIMIT = 7.0
MOE_BLOCK = 256
LANES = 128
NEG_BIG = -1e30

MIB = 1024 * 1024


def _dot(a, b):
    return jnp.dot(a, b, preferred_element_type=F32)


def _dot_nt(a, b):
    return lax.dot_general(a, b, (((1,), (1,)), ((), ())), preferred_element_type=F32)


def _dot_exact(a, b):
    return jnp.dot(a, b, precision=HIGHEST, preferred_element_type=F32)


def _rms(x, w):
    return x * lax.rsqrt(jnp.mean(x * x, axis=-1, keepdims=True) + EPS) * w


def _adaln_kernel(c_ref, w_ref, b_ref, o_ref):
    cs = c_ref[...]
    cs = cs * jax.nn.sigmoid(cs)
    o_ref[...] = _dot_exact(cs, w_ref[...]) + b_ref[...]


def _adaln(c, w, b):
    bsz, d = c.shape
    n_out = w.shape[1]
    rows = 8
    cp = jnp.zeros((rows, d), F32).at[:bsz].set(c)
    tn = 512
    out = pl.pallas_call(
        _adaln_kernel,
        grid=(n_out // tn,),
        in_specs=[
            pl.BlockSpec((rows, d), lambda j: (0, 0)),
            pl.BlockSpec((d, tn), lambda j: (0, j)),
            pl.BlockSpec((1, tn), lambda j: (0, j)),
        ],
        out_specs=pl.BlockSpec((rows, tn), lambda j: (0, j)),
        out_shape=jax.ShapeDtypeStruct((rows, n_out), F32),
        name="adaln",
    )(cp, w, b.reshape(1, n_out))
    return out[:bsz]


def _inproj_kernel(x_ref, mod_ref, nw_ref, wm_ref, wg_ref, qkva_ref, z_ref, ab_ref, qkvb_ref,
                   *, a_qkv, a_z):
    x = x_ref[...]
    sh = mod_ref[0, 0:1, :]
    sc = mod_ref[0, 1:2, :]
    h = _rms(x, nw_ref[0:1, :]) * (1.0 + sc) + sh
    hb = h.astype(BF16)
    tn = 512
    for c0 in range(0, a_qkv, tn):
        qkva_ref[:, c0:c0 + tn] = _dot(hb, wm_ref[:, c0:c0 + tn])
    for c0 in range(0, a_z, tn):
        z_ref[:, c0:c0 + tn] = _dot(hb, wm_ref[:, a_qkv + c0:a_qkv + c0 + tn])
    off = a_qkv + a_z
    for c0 in range(0, qkvb_ref.shape[1], tn):
        qkvb_ref[:, c0:c0 + tn] = _dot(hb, wm_ref[:, off + c0:off + c0 + tn]).astype(BF16)
    ab_ref[...] = _dot(hb, wg_ref[...])


def _inproj(x2, mod3, norm_w, w_main, w_gate, seq, a_qkv, a_z, b_qkv):
    n, d = x2.shape
    tm = 512
    tiles_per_seq = seq // tm
    kern = functools.partial(_inproj_kernel, a_qkv=a_qkv, a_z=a_z)
    return pl.pallas_call(
        kern,
        grid=(n // tm,),
        in_specs=[
            pl.BlockSpec((tm, d), lambda i: (i, 0)),
            pl.BlockSpec((1, 6, d), lambda i: (i // tiles_per_seq, 0, 0)),
            pl.BlockSpec((4, d), lambda i: (0, 0)),
            pl.BlockSpec(w_main.shape, lambda i: (0, 0)),
            pl.BlockSpec(w_gate.shape, lambda i: (0, 0)),
        ],
        out_specs=[
            pl.BlockSpec((tm, a_qkv), lambda i: (i, 0)),
            pl.BlockSpec((tm, a_z), lambda i: (i, 0)),
            pl.BlockSpec((tm, LANES), lambda i: (i, 0)),
            pl.BlockSpec((tm, b_qkv), lambda i: (i, 0)),
        ],
        out_shape=[
            jax.ShapeDtypeStruct((n, a_qkv), F32),
            jax.ShapeDtypeStruct((n, a_z), F32),
            jax.ShapeDtypeStruct((n, LANES), F32),
            jax.ShapeDtypeStruct((n, b_qkv), BF16),
        ],
        compiler_params=pltpu.CompilerParams(
            dimension_semantics=("parallel",), vmem_limit_bytes=48 * MIB),
        name="inproj",
    )(x2, mod3, norm_w, w_main, w_gate)


def _gdn_kernel(qkv_ref, z_ref, ab_ref, cw_ref, alog_ref, dtb_ref, anw_ref, o_ref,
                xcat_ref, state_ref, *, n_heads):
    i = pl.program_id(1)
    tt = qkv_ref.shape[0]
    width = qkv_ref.shape[1]
    a_width = width // 3
    dk = A_HEAD_DIM
    n_chunks = tt // CHUNK
    halo = 8

    @pl.when(i == 0)
    def _():
        xcat_ref[0:halo, :] = jnp.zeros((halo, width), F32)
        state_ref[...] = jnp.zeros_like(state_ref)

    xcat_ref[halo:halo + tt, :] = qkv_ref[...]

    def conv_silu(c0):
        acc = cw_ref[CONV_K - 1:CONV_K, c0:c0 + dk] * qkv_ref[:, c0:c0 + dk]
        for j in range(CONV_K - 1):
            start = halo - (CONV_K - 1) + j
            acc = acc + cw_ref[j:j + 1, c0:c0 + dk] * xcat_ref[start:start + tt, c0:c0 + dk]
        return acc * jax.nn.sigmoid(acc)

    def l2n(t):
        return t * lax.rsqrt(jnp.sum(t * t, axis=-1, keepdims=True) + EPS)

    ab = ab_ref[...]
    gfull = -jnp.exp(alog_ref[...]) * jax.nn.softplus(ab + dtb_ref[...])
    bfull = jax.nn.sigmoid(ab)

    row = lax.broadcasted_iota(jnp.int32, (tt, tt), 0)
    col = lax.broadcasted_iota(jnp.int32, (tt, tt), 1)
    shift = int(math.log2(CHUNK))
    same = (row >> shift) == (col >> shift)
    tri_incl = same & (col <= row)
    tri_strict = same & (col < row)
    eye = (row == col).astype(F32)
    col_chunk = lax.broadcasted_iota(jnp.int32, (dk, tt), 1) >> shift

    gcum = _dot_exact(tri_incl.astype(F32), gfull)
    gtot = _dot_exact(same.astype(F32), gfull)
    gcum_t = gcum.T

    for h in range(n_heads):
        gc = gcum[:, h:h + 1]
        gr = gcum_t[h:h + 1, :]
        gl = gtot[:, h:h + 1]
        beta = bfull[:, n_heads + h:n_heads + h + 1]

        q = l2n(conv_silu(h * dk)) * (dk ** -0.5)
        k = l2n(conv_silu(a_width + h * dk))
        v = conv_silu(2 * a_width + h * dk)

        decay = jnp.exp(jnp.where(tri_incl, gc - gr, -jnp.inf))
        kb = k * beta
        k16 = k.astype(BF16)
        lower = jnp.where(tri_strict, _dot_nt(kb.astype(BF16), k16) * decay, 0.0)
        attn = (_dot_nt(q.astype(BF16), k16) * decay).astype(BF16)

        tmat = eye - jnp.where((row >> 1) == (col >> 1), lower, 0.0)
        for lv in range(1, shift):
            couple = ((row >> (lv + 1)) == (col >> (lv + 1))) & ((row >> lv) != (col >> lv))
            t16 = tmat.astype(BF16)
            half = _dot(t16, jnp.where(couple, lower, 0.0).astype(BF16))
            tmat = tmat - _dot(half.astype(BF16), t16)

        eg = jnp.exp(gc)
        rhs = jnp.concatenate([v * beta, kb * eg], axis=1).astype(BF16)
        uw = _dot(tmat.astype(BF16), rhs)
        u = uw[:, :dk]
        w16 = uw[:, dk:].astype(BF16)
        qd16 = (q * eg).astype(BF16)
        kd_t = (k * jnp.exp(gl - gc)).T

        s = state_ref[h]
        v_new = [jnp.zeros((CHUNK, dk), F32)] * n_chunks
        outs = []
        for c in range(n_chunks):
            r0 = c * CHUNK
            s16 = s.astype(BF16)
            ws_qs = _dot(jnp.concatenate([w16[r0:r0 + CHUNK], qd16[r0:r0 + CHUNK]], axis=0), s16)
            v_new[c] = u[r0:r0 + CHUNK] - ws_qs[:CHUNK]
            kd_c = jnp.where(col_chunk == c, kd_t, 0.0).astype(BF16)
            lhs = jnp.concatenate([attn[r0:r0 + CHUNK, :], kd_c], axis=0)
            res = _dot(lhs, jnp.concatenate(v_new, axis=0).astype(BF16))
            outs.append(ws_qs[CHUNK:] + res[:CHUNK])
            s = s * jnp.exp(gtot[r0:r0 + 1, h:h + 1]) + res[CHUNK:]
        state_ref[h] = s

        o = jnp.concatenate(outs, axis=0)
        zh = z_ref[:, h * dk:(h + 1) * dk]
        o_ref[:, h * dk:(h + 1) * dk] = (_rms(o, anw_ref[...]) * (zh * jax.nn.sigmoid(zh))).astype(o_ref.dtype)

    xcat_ref[0:halo, :] = qkv_ref[tt - halo:tt, :]


def _gdn(qkv_a, z, ab, conv_w, a_log, dt_bias, a_norm_w, bsz, seq):
    n, width = qkv_a.shape
    a_width = width // 3
    n_heads = a_width // A_HEAD_DIM
    tt = 256
    nt = seq // tt
    alog = jnp.zeros((1, LANES), F32).at[0, :n_heads].set(a_log)
    dtb = jnp.zeros((1, LANES), F32).at[0, :n_heads].set(dt_bias)
    kern = functools.partial(_gdn_kernel, n_heads=n_heads)
    return pl.pallas_call(
        kern,
        grid=(bsz, nt),
        in_specs=[
            pl.BlockSpec((tt, width), lambda b, i: (b * nt + i, 0)),
            pl.BlockSpec((tt, a_width), lambda b, i: (b * nt + i, 0)),
            pl.BlockSpec((tt, LANES), lambda b, i: (b * nt + i, 0)),
            pl.BlockSpec((CONV_K, width), lambda b, i: (0, 0)),
            pl.BlockSpec((1, LANES), lambda b, i: (0, 0)),
            pl.BlockSpec((1, LANES), lambda b, i: (0, 0)),
            pl.BlockSpec((1, A_HEAD_DIM), lambda b, i: (0, 0)),
        ],
        out_specs=pl.BlockSpec((tt, a_width), lambda b, i: (b * nt + i, 0)),
        out_shape=jax.ShapeDtypeStruct((n, a_width), BF16),
        scratch_shapes=[
            pltpu.VMEM((tt + 8, width), F32),
            pltpu.VMEM((n_heads, A_HEAD_DIM, A_HEAD_DIM), F32),
        ],
        compiler_params=pltpu.CompilerParams(
            dimension_semantics=("arbitrary", "arbitrary"), vmem_limit_bytes=48 * MIB),
        name="gdn",
    )(qkv_a, z, ab, conv_w, alog, dtb, a_norm_w.reshape(1, A_HEAD_DIM))


def _attn_kernel(q_ref, k0_ref, k1_ref, k2_ref, v0_ref, v1_ref, v2_ref, bias_ref, o_ref, *, n_heads):
    i = pl.program_id(1)
    tq = q_ref.shape[0]
    hd = B_HEAD_DIM
    per = LANES // hd
    lane = lax.broadcasted_iota(jnp.int32, (1, LANES), 1)
    k_refs = (k0_ref, k1_ref, k2_ref)
    v_refs = (v0_ref, v1_ref, v2_ref)
    n_kb = len(k_refs)
    for g in range(n_heads // per):
        cs = slice(g * LANES, (g + 1) * LANES)
        qg = q_ref[:, cs]
        ks = [r[:, cs] for r in k_refs]
        vs = [r[:, cs] for r in v_refs]
        out = jnp.zeros((tq, LANES), F32)
        for hh in range(per):
            h = g * per + hh
            in_head = (lane >= hh * hd) & (lane < (hh + 1) * hd)
            qh = jnp.where(in_head, qg, jnp.zeros_like(qg)) * (hd ** -0.5)
            s = []
            for j in range(n_kb):
                sj = _dot_nt(qh, ks[j]) + bias_ref[h, :, j * tq:(j + 1) * tq]
                if j < n_kb - 1:
                    sj = jnp.where(i >= n_kb - 1 - j, sj, NEG_BIG)
                s.append(sj)
            m = s[0].max(axis=-1, keepdims=True)
            for sj in s[1:]:
                m = jnp.maximum(m, sj.max(axis=-1, keepdims=True))
            p = [jnp.exp(sj - m) for sj in s]
            den = p[0].sum(axis=-1, keepdims=True)
            for pj in p[1:]:
                den = den + pj.sum(axis=-1, keepdims=True)
            acc = _dot(p[0].astype(BF16), vs[0])
            for j in range(1, n_kb):
                acc = acc + _dot(p[j].astype(BF16), vs[j])
            out = jnp.where(in_head, acc / den, out)
        o_ref[:, cs] = out.astype(o_ref.dtype)


def _band_bias(rel_bias, tq, n_kb):
    back = (n_kb - 1) * tq
    qi = jnp.arange(tq)[:, None]
    kj = jnp.arange(n_kb * tq)[None, :]
    rel = qi + back - kj
    idx = jnp.clip(rel, -REL_CLIP, REL_CLIP) + REL_CLIP
    qc = qi // CHUNK
    kc = kj // CHUNK - back // CHUNK
    allowed = (kc <= qc) & (kc >= qc - B_PREV_CHUNKS)
    return jnp.where(allowed[None], rel_bias[:, idx].astype(F32), NEG_BIG)


def _band_attn(qkv_b, rel_bias, bsz, seq):
    n, width = qkv_b.shape
    b_width = width // 3
    n_heads = b_width // B_HEAD_DIM
    tq = 256
    n_kb = 3
    assert (n_kb - 1) * tq == B_PREV_CHUNKS * CHUNK
    nt = seq // tq
    bias = _band_bias(rel_bias, tq, n_kb)
    kern = functools.partial(_attn_kernel, n_heads=n_heads)

    def kv_spec(colblk, back):
        return pl.BlockSpec((tq, b_width), lambda b, i: (b * nt + jnp.maximum(i - back, 0), colblk))

    return pl.pallas_call(
        kern,
        grid=(bsz, nt),
        in_specs=[
            pl.BlockSpec((tq, b_width), lambda b, i: (b * nt + i, 0)),
            kv_spec(1, 2), kv_spec(1, 1), kv_spec(1, 0),
            kv_spec(2, 2), kv_spec(2, 1), kv_spec(2, 0),
            pl.BlockSpec(bias.shape, lambda b, i: (0, 0, 0)),
        ],
        out_specs=pl.BlockSpec((tq, b_width), lambda b, i: (b * nt + i, 0)),
        out_shape=jax.ShapeDtypeStruct((n, b_width), BF16),
        compiler_params=pltpu.CompilerParams(
            dimension_semantics=("parallel", "parallel"), vmem_limit_bytes=48 * MIB),
        name="band_attn",
    )(qkv_b, qkv_b, qkv_b, qkv_b, qkv_b, qkv_b, qkv_b, bias)


def _outproj_kernel(oa_ref, ob_ref, x_ref, mod_ref, nw_ref, woa_ref, wob_ref, wr_ref, br_ref,
                    x1_ref, h2_ref, route_ref, gate_ref, cnt_ref, carry_ref, *, n_experts):
    i = pl.program_id(0)
    tm = x_ref.shape[0]

    @pl.when(i == 0)
    def _():
        carry_ref[...] = jnp.zeros_like(carry_ref)

    y = _dot(oa_ref[...], woa_ref[...]) + _dot(ob_ref[...], wob_ref[...])
    ga1 = mod_ref[0, 2:3, :]
    sh2 = mod_ref[0, 3:4, :]
    sc2 = mod_ref[0, 4:5, :]
    x1 = x_ref[...] + ga1 * _rms(y, nw_ref[1:2, :])
    x1_ref[...] = x1
    h2 = _rms(x1, nw_ref[2:3, :]) * (1.0 + sc2) + sh2
    h2_ref[...] = h2

    logits = _dot_exact(h2, wr_ref[...]) + br_ref[...]
    lane_i = lax.broadcasted_iota(jnp.int32, (tm, LANES), 1)
    lane = lane_i.astype(F32)
    lg = jnp.where(lane_i < n_experts, logits, -jnp.inf)
    vals, idxs = [], []
    for _ in range(TOP_K):
        m = lg.max(axis=-1, keepdims=True)
        idx = jnp.where(lg == m, lane, float(LANES)).min(axis=-1, keepdims=True)
        vals.append(m)
        idxs.append(idx)
        lg = jnp.where(lane == idx, -jnp.inf, lg)
    ex = [jnp.exp(v - vals[0]) for v in vals]
    den = ex[0]
    for e in ex[1:]:
        den = den + e

    onehot = jnp.zeros((tm, LANES), F32)
    for idx in idxs:
        onehot = onehot + (lane == idx).astype(F32)
    row = lax.broadcasted_iota(jnp.int32, (tm, tm), 0)
    col = lax.broadcasted_iota(jnp.int32, (tm, tm), 1)
    before = (col < row).astype(BF16)
    cum = _dot(before, onehot.astype(BF16)) + carry_ref[0:1, :]
    carry = carry_ref[0:1, :] + onehot.sum(axis=0, keepdims=True)
    carry_ref[...] = jnp.broadcast_to(carry, carry_ref.shape)
    cnt_ref[...] = jnp.broadcast_to(carry, cnt_ref.shape)

    route = jnp.zeros((tm, LANES), F32)
    gate = jnp.zeros((tm, LANES), F32)
    for k in range(TOP_K):
        rank = jnp.where(lane == idxs[k], cum, 0.0).sum(axis=-1, keepdims=True)
        route = jnp.where(lane_i == k, idxs[k], route)
        route = jnp.where(lane_i == TOP_K + k, rank, route)
        gate = jnp.where(lane_i == k, ex[k] / den, gate)
    route_ref[...] = route.astype(jnp.int32)
    gate_ref[...] = gate


def _outproj(o_a, o_b, x2, mod3, norm_w, w_out_a, w_out_b, w_router, b_router, seq):
    n, d = x2.shape
    n_experts = w_router.shape[1]
    tm = 512
    tiles_per_seq = seq // tm
    wr = jnp.zeros((d, LANES), F32).at[:, :n_experts].set(w_router)
    br = jnp.zeros((1, LANES), F32).at[0, :n_experts].set(b_router)
    kern = functools.partial(_outproj_kernel, n_experts=n_experts)
    aw = o_a.shape[1]
    bw = o_b.shape[1]
    return pl.pallas_call(
        kern,
        grid=(n // tm,),
        in_specs=[
            pl.BlockSpec((tm, aw), lambda i: (i, 0)),
            pl.BlockSpec((tm, bw), lambda i: (i, 0)),
            pl.BlockSpec((tm, d), lambda i: (i, 0)),
            pl.BlockSpec((1, 6, d), lambda i: (i // tiles_per_seq, 0, 0)),
            pl.BlockSpec((4, d), lambda i: (0, 0)),
            pl.BlockSpec((aw, d), lambda i: (0, 0)),
            pl.BlockSpec((bw, d), lambda i: (0, 0)),
            pl.BlockSpec((d, LANES), lambda i: (0, 0)),
            pl.BlockSpec((1, LANES), lambda i: (0, 0)),
        ],
        out_specs=[
            pl.BlockSpec((tm, d), lambda i: (i, 0)),
            pl.BlockSpec((tm, d), lambda i: (i, 0)),
            pl.BlockSpec((tm, LANES), lambda i: (i, 0)),
            pl.BlockSpec((tm, LANES), lambda i: (i, 0)),
            pl.BlockSpec((8, LANES), lambda i: (0, 0)),
        ],
        out_shape=[
            jax.ShapeDtypeStruct((n, d), F32),
            jax.ShapeDtypeStruct((n, d), F32),
            jax.ShapeDtypeStruct((n, LANES), jnp.int32),
            jax.ShapeDtypeStruct((n, LANES), F32),
            jax.ShapeDtypeStruct((8, LANES), F32),
        ],
        scratch_shapes=[pltpu.VMEM((8, LANES), F32)],
        compiler_params=pltpu.CompilerParams(
            dimension_semantics=("arbitrary",), vmem_limit_bytes=48 * MIB),
        name="outproj_router",
    )(o_a, o_b, x2, mod3, norm_w, w_out_a, w_out_b, wr, br)


def _expert_kernel(be_ref, nu_ref, tok_ref, h_hbm, wup_ref, bup_ref, wdn_ref, bdn_ref, y_ref,
                   xbuf, sem, wup16, wdn16):
    j = pl.program_id(0)
    blk, d = xbuf.shape
    f = wdn16.shape[0]

    prev = be_ref[jnp.maximum(j - 1, 0)]
    changed = jnp.logical_or(j == 0, be_ref[j] != prev)

    @pl.when(changed)
    def _():
        step = 128
        for r0 in range(0, d, step):
            wup16[r0:r0 + step, :] = wup_ref[0, r0:r0 + step, :].astype(BF16)
        for r0 in range(0, f, step):
            wdn16[r0:r0 + step, :] = wdn_ref[0, r0:r0 + step, :].astype(BF16)

    @pl.when(j < nu_ref[0])
    def _():
        def issue(r, carry):
            t = tok_ref[0, 0, r]
            pltpu.make_async_copy(h_hbm.at[pl.ds(t, 1), :], xbuf.at[pl.ds(r, 1), :], sem).start()
            return carry

        lax.fori_loop(0, blk, issue, 0)
        pltpu.make_async_copy(h_hbm.at[pl.ds(0, blk), :], xbuf, sem).wait()

        xb = xbuf[...].astype(BF16)
        hu = _dot(xb, wup16[...]) + bup_ref[0]
        glu = jnp.minimum(hu[:, :f], SWIGLU_LIMIT)
        lin = jnp.clip(hu[:, f:], -SWIGLU_LIMIT, SWIGLU_LIMIT)
        act = glu * jax.nn.sigmoid(SWIGLU_ALPHA * glu) * (lin + 1.0)
        y_ref[...] = _dot(act.astype(BF16), wdn16[...]) + bdn_ref[0]

    @pl.when(j >= nu_ref[0])
    def _():
        y_ref[...] = jnp.zeros_like(y_ref)


def _experts(h2, slot_tok, block_e, n_used, w_up, b_up, w_down, b_down):
    n, d = h2.shape
    n_exp, _, f2 = w_up.shape
    f = w_down.shape[1]
    n_blocks = block_e.shape[0]
    blk = MOE_BLOCK
    grid_spec = pltpu.PrefetchScalarGridSpec(
        num_scalar_prefetch=2,
        grid=(n_blocks,),
        in_specs=[
            pl.BlockSpec((1, 1, blk), lambda j, be, nu: (j, 0, 0), memory_space=pltpu.SMEM),
            pl.BlockSpec(memory_space=pl.ANY),
            pl.BlockSpec((1, d, f2), lambda j, be, nu: (be[j], 0, 0)),
            pl.BlockSpec((1, 1, f2), lambda j, be, nu: (be[j], 0, 0)),
            pl.BlockSpec((1, f, d), lambda j, be, nu: (be[j], 0, 0)),
            pl.BlockSpec((1, 1, d), lambda j, be, nu: (be[j], 0, 0)),
        ],
        out_specs=pl.BlockSpec((blk, d), lambda j, be, nu: (j, 0)),
        scratch_shapes=[
            pltpu.VMEM((blk, d), F32),
            pltpu.SemaphoreType.DMA(()),
            pltpu.VMEM((d, f2), BF16),
            pltpu.VMEM((f, d), BF16),
        ],
    )
    return pl.pallas_call(
        _expert_kernel,
        grid_spec=grid_spec,
        out_shape=jax.ShapeDtypeStruct((n_blocks * blk, d), F32),
        compiler_params=pltpu.CompilerParams(
            dimension_semantics=("arbitrary",), vmem_limit_bytes=56 * MIB),
        name="experts",
    )(block_e, n_used, slot_tok.reshape(n_blocks, 1, blk), h2, w_up,
      b_up.reshape(n_exp, 1, f2), w_down, b_down.reshape(n_exp, 1, d))


def _combine_kernel(dest_ref, y_hbm, gate_ref, x1_ref, mod_ref, nw_ref, o_ref, buf, sem):
    tm, d = x1_ref.shape

    def issue(t, carry):
        for k in range(TOP_K):
            s = dest_ref[0, 0, t * TOP_K + k]
            pltpu.make_async_copy(y_hbm.at[pl.ds(s, 1), :], buf.at[k, pl.ds(t, 1), :], sem).start()
        return carry

    lax.fori_loop(0, tm, issue, 0)
    for k in range(TOP_K):
        pltpu.make_async_copy(y_hbm.at[pl.ds(0, tm), :], buf.at[k], sem).wait()

    gate = gate_ref[...]
    ysum = gate[:, 0:1] * buf[0]
    for k in range(1, TOP_K):
        ysum = ysum + gate[:, k:k + 1] * buf[k]
    ga2 = mod_ref[0, 5:6, :]
    o_ref[...] = x1_ref[...] + ga2 * _rms(ysum, nw_ref[3:4, :])


def _combine(y_sorted, dest, gates, x1, mod3, norm_w, seq):
    n, d = x1.shape
    tm = 128
    tiles_per_seq = seq // tm
    return pl.pallas_call(
        _combine_kernel,
        grid=(n // tm,),
        in_specs=[
            pl.BlockSpec((1, 1, tm * TOP_K), lambda i: (i, 0, 0), memory_space=pltpu.SMEM),
            pl.BlockSpec(memory_space=pl.ANY),
            pl.BlockSpec((tm, LANES), lambda i: (i, 0)),
            pl.BlockSpec((tm, d), lambda i: (i, 0)),
            pl.BlockSpec((1, 6, d), lambda i: (i // tiles_per_seq, 0, 0)),
            pl.BlockSpec((4, d), lambda i: (0, 0)),
        ],
        out_specs=pl.BlockSpec((tm, d), lambda i: (i, 0)),
        out_shape=jax.ShapeDtypeStruct((n, d), F32),
        scratch_shapes=[
            pltpu.VMEM((TOP_K, tm, d), F32),
            pltpu.SemaphoreType.DMA(()),
        ],
        compiler_params=pltpu.CompilerParams(
            dimension_semantics=("arbitrary",), vmem_limit_bytes=32 * MIB),
        name="combine",
    )(dest.reshape(n // tm, 1, tm * TOP_K), y_sorted, gates, x1, mod3, norm_w)


def _layer(x, c, w_ada, b_ada, norm_w, w_in, conv_w, a_log, dt_bias, a_norm_w, rel_bias, w_out,
           w_router, b_router, w_up, b_up, w_down, b_down):
    bsz, seq, d = x.shape
    n = bsz * seq
    a_width = conv_w.shape[1] // 3
    a_heads = a_log.shape[0]
    b_width = w_out.shape[0] - a_width
    n_experts = w_router.shape[1]
    off_gate = 4 * a_width
    off_b = off_gate + 2 * a_heads

    mod3 = _adaln(c, w_ada, b_ada).reshape(bsz, 6, d)
    x2 = x.reshape(n, d)

    w_main = jnp.concatenate([w_in[:, :off_gate], w_in[:, off_b:]], axis=1).astype(BF16)
    w_gate = jnp.zeros((d, LANES), F32).at[:, :2 * a_heads].set(w_in[:, off_gate:off_b]).astype(BF16)
    qkv_a, z_a, ab, qkv_b = _inproj(x2, mod3, norm_w, w_main, w_gate, seq, 3 * a_width, a_width, 3 * b_width)

    o_a = _gdn(qkv_a, z_a, ab, conv_w, a_log, dt_bias, a_norm_w, bsz, seq)
    o_b = _band_attn(qkv_b, rel_bias, bsz, seq)

    w_out16 = w_out.astype(BF16)
    x1, h2, route, gates, cnt = _outproj(o_a, o_b, x2, mod3, norm_w, w_out16[:a_width], w_out16[a_width:],
                                         w_router, b_router, seq)

    top_idx = route[:, :TOP_K]
    rank = route[:, TOP_K:2 * TOP_K]
    counts = cnt[0, :n_experts].astype(jnp.int32)
    padded = (counts + MOE_BLOCK - 1) // MOE_BLOCK * MOE_BLOCK
    pend = jnp.cumsum(padded)
    pstart = pend - padded
    dest = pstart[top_idx] + rank
    n_assign = n * TOP_K
    n_blocks = -(-n_assign // MOE_BLOCK) + n_experts
    cap = n_blocks * MOE_BLOCK
    flat_tok = jnp.repeat(jnp.arange(n, dtype=jnp.int32), TOP_K)
    slot_tok = jnp.zeros((cap,), jnp.int32).at[dest.reshape(-1)].set(flat_tok)
    block_e = jnp.minimum(jnp.searchsorted(pend, jnp.arange(n_blocks) * MOE_BLOCK, side='right'),
                          n_experts - 1).astype(jnp.int32)
    n_used = (pend[-1:] // MOE_BLOCK).astype(jnp.int32)

    y_sorted = _experts(h2, slot_tok, block_e, n_used, w_up, b_up, w_down, b_down)
    out = _combine(y_sorted, dest.astype(jnp.int32), gates, x1, mod3, norm_w, seq)
    return out.reshape(bsz, seq, d)


def kernel(x, c, w_ada, b_ada, norm_w, w_in, conv_w, a_log, dt_bias, a_norm_w, rel_bias, w_out,
           w_router, b_router, w_up, b_up, w_down, b_down):
    for l in range(w_ada.shape[0]):
        x = _layer(x, c, w_ada[l], b_ada[l], norm_w[l], w_in[l], conv_w[l], a_log[l], dt_bias[l],
                   a_norm_w[l], rel_bias[l], w_out[l], w_router[l], b_router[l], w_up[l], b_up[l],
                   w_down[l], b_down[l])
    return x
```

```python
import functools
import math

import jax
import jax.numpy as jnp
from jax import lax
from jax.experimental import pallas as pl
from jax.experimental.pallas import tpu as pltpu

F32 = jnp.float32
BF16 = jnp.bfloat16
HIGHEST = lax.Precision.HIGHEST

EPS = 1e-6
CHUNK = 64
CONV_K = 4
A_HEAD_DIM = 128
B_HEAD_DIM = 64
B_PREV_CHUNKS = 8
REL_CLIP = 128
TOP_K = 4
SWIGLU_ALPHA = 1.702
SWIGLU_LIMIT = 7.0
MOE_BLOCK = 256
LANES = 128
NEG_BIG = -1e30

MIB = 1024 * 1024


def _dot(a, b):
    return jnp.dot(a, b, preferred_element_type=F32)


def _dot_nt(a, b):
    return lax.dot_general(a, b, (((1,), (1,)), ((), ())), preferred_element_type=F32)


def _dot_exact(a, b):
    return jnp.dot(a, b, precision=HIGHEST, preferred_element_type=F32)


def _rms(x, w):
    return x * lax.rsqrt(jnp.mean(x * x, axis=-1, keepdims=True) + EPS) * w


def _adaln_kernel(c_ref, w_ref, b_ref, o_ref):
    cs = c_ref[...]
    cs = cs * jax.nn.sigmoid(cs)
    o_ref[...] = _dot_exact(cs, w_ref[...]) + b_ref[...]


def _adaln(c, w, b):
    bsz, d = c.shape
    n_out = w.shape[1]
    rows = 8
    cp = jnp.zeros((rows, d), F32).at[:bsz].set(c)
    tn = 512
    out = pl.pallas_call(
        _adaln_kernel,
        grid=(n_out // tn,),
        in_specs=[
            pl.BlockSpec((rows, d), lambda j: (0, 0)),
            pl.BlockSpec((d, tn), lambda j: (0, j)),
            pl.BlockSpec((1, tn), lambda j: (0, j)),
        ],
        out_specs=pl.BlockSpec((rows, tn), lambda j: (0, j)),
        out_shape=jax.ShapeDtypeStruct((rows, n_out), F32),
        name="adaln",
    )(cp, w, b.reshape(1, n_out))
    return out[:bsz]


def _inproj_kernel(x_ref, mod_ref, nw_ref, wm_ref, wg_ref, qkva_ref, z_ref, ab_ref, qkvb_ref,
                   *, a_qkv, a_z):
    x = x_ref[...]
    sh = mod_ref[0, 0:1, :]
    sc = mod_ref[0, 1:2, :]
    h = _rms(x, nw_ref[0:1, :]) * (1.0 + sc) + sh
    hb = h.astype(BF16)
    tn = 512
    for c0 in range(0, a_qkv, tn):
        qkva_ref[:, c0:c0 + tn] = _dot(hb, wm_ref[:, c0:c0 + tn])
    for c0 in range(0, a_z, tn):
        z_ref[:, c0:c0 + tn] = _dot(hb, wm_ref[:, a_qkv + c0:a_qkv + c0 + tn])
    off = a_qkv + a_z
    for c0 in range(0, qkvb_ref.shape[1], tn):
        qkvb_ref[:, c0:c0 + tn] = _dot(hb, wm_ref[:, off + c0:off + c0 + tn]).astype(BF16)
    ab_ref[...] = _dot(hb, wg_ref[...])


def _inproj(x2, mod3, norm_w, w_main, w_gate, seq, a_qkv, a_z, b_qkv):
    n, d = x2.shape
    tm = 512
    tiles_per_seq = seq // tm
    kern = functools.partial(_inproj_kernel, a_qkv=a_qkv, a_z=a_z)
    return pl.pallas_call(
        kern,
        grid=(n // tm,),
        in_specs=[
            pl.BlockSpec((tm, d), lambda i: (i, 0)),
            pl.BlockSpec((1, 6, d), lambda i: (i // tiles_per_seq, 0, 0)),
            pl.BlockSpec((4, d), lambda i: (0, 0)),
            pl.BlockSpec(w_main.shape, lambda i: (0, 0)),
            pl.BlockSpec(w_gate.shape, lambda i: (0, 0)),
        ],
        out_specs=[
            pl.BlockSpec((tm, a_qkv), lambda i: (i, 0)),
            pl.BlockSpec((tm, a_z), lambda i: (i, 0)),
            pl.BlockSpec((tm, LANES), lambda i: (i, 0)),
            pl.BlockSpec((tm, b_qkv), lambda i: (i, 0)),
        ],
        out_shape=[
            jax.ShapeDtypeStruct((n, a_qkv), F32),
            jax.ShapeDtypeStruct((n, a_z), F32),
            jax.ShapeDtypeStruct((n, LANES), F32),
            jax.ShapeDtypeStruct((n, b_qkv), BF16),
        ],
        compiler_params=pltpu.CompilerParams(
            dimension_semantics=("parallel",), vmem_limit_bytes=48 * MIB),
        name="inproj",
    )(x2, mod3, norm_w, w_main, w_gate)


def _gdn_kernel(qkv_ref, z_ref, ab_ref, cw_ref, alog_ref, dtb_ref, anw_ref, o_ref,
                xcat_ref, state_ref, *, n_heads):
    i = pl.program_id(1)
    tt = qkv_ref.shape[0]
    width = qkv_ref.shape[1]
    a_width = width // 3
    dk = A_HEAD_DIM
    n_chunks = tt // CHUNK
    halo = 8

    @pl.when(i == 0)
    def _():
        xcat_ref[0:halo, :] = jnp.zeros((halo, width), F32)
        state_ref[...] = jnp.zeros_like(state_ref)

    xcat_ref[halo:halo + tt, :] = qkv_ref[...]

    def conv_silu(c0):
        acc = cw_ref[CONV_K - 1:CONV_K, c0:c0 + dk] * qkv_ref[:, c0:c0 + dk]
        for j in range(CONV_K - 1):
            start = halo - (CONV_K - 1) + j
            acc = acc + cw_ref[j:j + 1, c0:c0 + dk] * xcat_ref[start:start + tt, c0:c0 + dk]
        return acc * jax.nn.sigmoid(acc)

    def l2n(t):
        return t * lax.rsqrt(jnp.sum(t * t, axis=-1, keepdims=True) + EPS)

    ab = ab_ref[...]
    gfull = -jnp.exp(alog_ref[...]) * jax.nn.softplus(ab + dtb_ref[...])
    bfull = jax.nn.sigmoid(ab)

    row = lax.broadcasted_iota(jnp.int32, (tt, tt), 0)
    col = lax.broadcasted_iota(jnp.int32, (tt, tt), 1)
    shift = int(math.log2(CHUNK))
    same = (row >> shift) == (col >> shift)
    tri_incl = same & (col <= row)
    tri_strict = same & (col < row)
    eye = (row == col).astype(F32)
    col_chunk = lax.broadcasted_iota(jnp.int32, (dk, tt), 1) >> shift

    gcum = _dot_exact(tri_incl.astype(F32), gfull)
    gtot = _dot_exact(same.astype(F32), gfull)
    gcum_t = gcum.T

    for h in range(n_heads):
        gc = gcum[:, h:h + 1]
        gr = gcum_t[h:h + 1, :]
        gl = gtot[:, h:h + 1]
        beta = bfull[:, n_heads + h:n_heads + h + 1]

        q = l2n(conv_silu(h * dk)) * (dk ** -0.5)
        k = l2n(conv_silu(a_width + h * dk))
        v = conv_silu(2 * a_width + h * dk)

        decay = jnp.exp(jnp.where(tri_incl, gc - gr, -jnp.inf))
        kb = k * beta
        k16 = k.astype(BF16)
        lower = jnp.where(tri_strict, _dot_nt(kb.astype(BF16), k16) * decay, 0.0)
        attn = (_dot_nt(q.astype(BF16), k16) * decay).astype(BF16)

        tmat = eye - jnp.where((row >> 1) == (col >> 1), lower, 0.0)
        for lv in range(1, shift):
            couple = ((row >> (lv + 1)) == (col >> (lv + 1))) & ((row >> lv) != (col >> lv))
            t16 = tmat.astype(BF16)
            half = _dot(t16, jnp.where(couple, lower, 0.0).astype(BF16))
            tmat = tmat - _dot(half.astype(BF16), t16)

        eg = jnp.exp(gc)
        rhs = jnp.concatenate([v * beta, kb * eg], axis=1).astype(BF16)
        uw = _dot(tmat.astype(BF16), rhs)
        u = uw[:, :dk]
        w16 = uw[:, dk:].astype(BF16)
        qd16 = (q * eg).astype(BF16)
        kd_t = (k * jnp.exp(gl - gc)).T

        s = state_ref[h]
        v_new = [jnp.zeros((CHUNK, dk), F32)] * n_chunks
        outs = []
        for c in range(n_chunks):
            r0 = c * CHUNK
            s16 = s.astype(BF16)
            ws_qs = _dot(jnp.concatenate([w16[r0:r0 + CHUNK], qd16[r0:r0 + CHUNK]], axis=0), s16)
            v_new[c] = u[r0:r0 + CHUNK] - ws_qs[:CHUNK]
            kd_c = jnp.where(col_chunk == c, kd_t, 0.0).astype(BF16)
            lhs = jnp.concatenate([attn[r0:r0 + CHUNK, :], kd_c], axis=0)
            res = _dot(lhs, jnp.concatenate(v_new, axis=0).astype(BF16))
            outs.append(ws_qs[CHUNK:] + res[:CHUNK])
            s = s * jnp.exp(gtot[r0:r0 + 1, h:h + 1]) + res[CHUNK:]
        state_ref[h] = s

        o = jnp.concatenate(outs, axis=0)
        zh = z_ref[:, h * dk:(h + 1) * dk]
        o_ref[:, h * dk:(h + 1) * dk] = (_rms(o, anw_ref[...]) * (zh * jax.nn.sigmoid(zh))).astype(o_ref.dtype)

    xcat_ref[0:halo, :] = qkv_ref[tt - halo:tt, :]


def _gdn(qkv_a, z, ab, conv_w, a_log, dt_bias, a_norm_w, bsz, seq):
    n, width = qkv_a.shape
    a_width = width // 3
    n_heads = a_width // A_HEAD_DIM
    tt = 256
    nt = seq // tt
    alog = jnp.zeros((1, LANES), F32).at[0, :n_heads].set(a_log)
    dtb = jnp.zeros((1, LANES), F32).at[0, :n_heads].set(dt_bias)
    kern = functools.partial(_gdn_kernel, n_heads=n_heads)
    return pl.pallas_call(
        kern,
        grid=(bsz, nt),
        in_specs=[
            pl.BlockSpec((tt, width), lambda b, i: (b * nt + i, 0)),
            pl.BlockSpec((tt, a_width), lambda b, i: (b * nt + i, 0)),
            pl.BlockSpec((tt, LANES), lambda b, i: (b * nt + i, 0)),
            pl.BlockSpec((CONV_K, width), lambda b, i: (0, 0)),
            pl.BlockSpec((1, LANES), lambda b, i: (0, 0)),
            pl.BlockSpec((1, LANES), lambda b, i: (0, 0)),
            pl.BlockSpec((1, A_HEAD_DIM), lambda b, i: (0, 0)),
        ],
        out_specs=pl.BlockSpec((tt, a_width), lambda b, i: (b * nt + i, 0)),
        out_shape=jax.ShapeDtypeStruct((n, a_width), BF16),
        scratch_shapes=[
            pltpu.VMEM((tt + 8, width), F32),
            pltpu.VMEM((n_heads, A_HEAD_DIM, A_HEAD_DIM), F32),
        ],
        compiler_params=pltpu.CompilerParams(
            dimension_semantics=("arbitrary", "arbitrary"), vmem_limit_bytes=48 * MIB),
        name="gdn",
    )(qkv_a, z, ab, conv_w, alog, dtb, a_norm_w.reshape(1, A_HEAD_DIM))


def _attn_kernel(q_ref, k0_ref, k1_ref, k2_ref, v0_ref, v1_ref, v2_ref, bias_ref, o_ref, *, n_heads):
    i = pl.program_id(1)
    tq = q_ref.shape[0]
    hd = B_HEAD_DIM
    per = LANES // hd
    lane = lax.broadcasted_iota(jnp.int32, (1, LANES), 1)
    k_refs = (k0_ref, k1_ref, k2_ref)
    v_refs = (v0_ref, v1_ref, v2_ref)
    n_kb = len(k_refs)
    for g in range(n_heads // per):
        cs = slice(g * LANES, (g + 1) * LANES)
        qg = q_ref[:, cs]
        ks = [r[:, cs] for r in k_refs]
        vs = [r[:, cs] for r in v_refs]
        out = jnp.zeros((tq, LANES), F32)
        for hh in range(per):
            h = g * per + hh
            in_head = (lane >= hh * hd) & (lane < (hh + 1) * hd)
            qh = jnp.where(in_head, qg, jnp.zeros_like(qg)) * (hd ** -0.5)
            s = []
            for j in range(n_kb):
                sj = _dot_nt(qh, ks[j]) + bias_ref[h, :, j * tq:(j + 1) * tq]
                if j < n_kb - 1:
                    sj = jnp.where(i >= n_kb - 1 - j, sj, NEG_BIG)
                s.append(sj)
            m = s[0].max(axis=-1, keepdims=True)
            for sj in s[1:]:
                m = jnp.maximum(m, sj.max(axis=-1, keepdims=True))
            p = [jnp.exp(sj - m) for sj in s]
            den = p[0].sum(axis=-1, keepdims=True)
            for pj in p[1:]:
                den = den + pj.sum(axis=-1, keepdims=True)
            acc = _dot(p[0].astype(BF16), vs[0])
            for j in range(1, n_kb):
                acc = acc + _dot(p[j].astype(BF16), vs[j])
            out = jnp.where(in_head, acc / den, out)
        o_ref[:, cs] = out.astype(o_ref.dtype)


def _band_bias(rel_bias, tq, n_kb):
    n_h = rel_bias.shape[0]
    back = (n_kb - 1) * tq
    nk = n_kb * tq
    span = tq + nk - 1
    lo = (nk - 1 - back) - REL_CLIP
    hi = span - lo - (2 * REL_CLIP + 1)
    by_offset = jnp.concatenate([jnp.broadcast_to(rel_bias[:, :1], (n_h, lo)), rel_bias,
                                 jnp.broadcast_to(rel_bias[:, -1:], (n_h, hi))], axis=1).astype(F32)
    rev = jnp.pad(by_offset[:, ::-1], ((0, 0), (0, 1)))
    skew = jnp.broadcast_to(rev[:, None, :], (n_h, tq, span + 1)).reshape(n_h, tq * (span + 1))
    skew = skew[:, :tq * span].reshape(n_h, tq, span)
    table = skew[:, :, tq - 1:tq - 1 + nk]
    qc = jnp.arange(tq)[:, None] // CHUNK
    kc = jnp.arange(nk)[None, :] // CHUNK - back // CHUNK
    allowed = (kc <= qc) & (kc >= qc - B_PREV_CHUNKS)
    return jnp.where(allowed[None], table, NEG_BIG)


def _band_attn(qkv_b, rel_bias, bsz, seq):
    n, width = qkv_b.shape
    b_width = width // 3
    n_heads = b_width // B_HEAD_DIM
    tq = 256
    n_kb = 3
    assert (n_kb - 1) * tq == B_PREV_CHUNKS * CHUNK
    nt = seq // tq
    bias = _band_bias(rel_bias, tq, n_kb)
    kern = functools.partial(_attn_kernel, n_heads=n_heads)

    def kv_spec(colblk, back):
        return pl.BlockSpec((tq, b_width), lambda b, i: (b * nt + jnp.maximum(i - back, 0), colblk))

    return pl.pallas_call(
        kern,
        grid=(bsz, nt),
        in_specs=[
            pl.BlockSpec((tq, b_width), lambda b, i: (b * nt + i, 0)),
            kv_spec(1, 2), kv_spec(1, 1), kv_spec(1, 0),
            kv_spec(2, 2), kv_spec(2, 1), kv_spec(2, 0),
            pl.BlockSpec(bias.shape, lambda b, i: (0, 0, 0)),
        ],
        out_specs=pl.BlockSpec((tq, b_width), lambda b, i: (b * nt + i, 0)),
        out_shape=jax.ShapeDtypeStruct((n, b_width), BF16),
        compiler_params=pltpu.CompilerParams(
            dimension_semantics=("parallel", "parallel"), vmem_limit_bytes=48 * MIB),
        name="band_attn",
    )(qkv_b, qkv_b, qkv_b, qkv_b, qkv_b, qkv_b, qkv_b, bias)


def _outproj_kernel(oa_ref, ob_ref, x_ref, mod_ref, nw_ref, woa_ref, wob_ref, wr_ref, br_ref,
                    x1_ref, h2_ref, route_ref, gate_ref, cnt_ref, carry_ref, *, n_experts):
    i = pl.program_id(0)
    tm = x_ref.shape[0]

    @pl.when(i == 0)
    def _():
        carry_ref[...] = jnp.zeros_like(carry_ref)

    y = _dot(oa_ref[...], woa_ref[...]) + _dot(ob_ref[...], wob_ref[...])
    ga1 = mod_ref[0, 2:3, :]
    sh2 = mod_ref[0, 3:4, :]
    sc2 = mod_ref[0, 4:5, :]
    x1 = x_ref[...] + ga1 * _rms(y, nw_ref[1:2, :])
    x1_ref[...] = x1
    h2 = _rms(x1, nw_ref[2:3, :]) * (1.0 + sc2) + sh2
    h2_ref[...] = h2

    logits = _dot_exact(h2, wr_ref[...]) + br_ref[...]
    lane_i = lax.broadcasted_iota(jnp.int32, (tm, LANES), 1)
    lane = lane_i.astype(F32)
    lg = jnp.where(lane_i < n_experts, logits, -jnp.inf)
    vals, idxs = [], []
    for _ in range(TOP_K):
        m = lg.max(axis=-1, keepdims=True)
        idx = jnp.where(lg == m, lane, float(LANES)).min(axis=-1, keepdims=True)
        vals.append(m)
        idxs.append(idx)
        lg = jnp.where(lane == idx, -jnp.inf, lg)
    ex = [jnp.exp(v - vals[0]) for v in vals]
    den = ex[0]
    for e in ex[1:]:
        den = den + e

    onehot = jnp.zeros((tm, LANES), F32)
    for idx in idxs:
        onehot = onehot + (lane == idx).astype(F32)
    row = lax.broadcasted_iota(jnp.int32, (tm, tm), 0)
    col = lax.broadcasted_iota(jnp.int32, (tm, tm), 1)
    before = (col < row).astype(BF16)
    cum = _dot(before, onehot.astype(BF16)) + carry_ref[0:1, :]
    carry = carry_ref[0:1, :] + onehot.sum(axis=0, keepdims=True)
    carry_ref[...] = jnp.broadcast_to(carry, carry_ref.shape)
    cnt_ref[...] = jnp.broadcast_to(carry, cnt_ref.shape)

    route = jnp.zeros((tm, LANES), F32)
    gate = jnp.zeros((tm, LANES), F32)
    for k in range(TOP_K):
        rank = jnp.where(lane == idxs[k], cum, 0.0).sum(axis=-1, keepdims=True)
        route = jnp.where(lane_i == k, idxs[k], route)
        route = jnp.where(lane_i == TOP_K + k, rank, route)
        gate = jnp.where(lane_i == k, ex[k] / den, gate)
    route_ref[...] = route.astype(jnp.int32)
    gate_ref[...] = gate


def _outproj(o_a, o_b, x2, mod3, norm_w, w_out_a, w_out_b, w_router, b_router, seq):
    n, d = x2.shape
    n_experts = w_router.shape[1]
    tm = 512
    tiles_per_seq = seq // tm
    wr = jnp.zeros((d, LANES), F32).at[:, :n_experts].set(w_router)
    br = jnp.zeros((1, LANES), F32).at[0, :n_experts].set(b_router)
    kern = functools.partial(_outproj_kernel, n_experts=n_experts)
    aw = o_a.shape[1]
    bw = o_b.shape[1]
    return pl.pallas_call(
        kern,
        grid=(n // tm,),
        in_specs=[
            pl.BlockSpec((tm, aw), lambda i: (i, 0)),
            pl.BlockSpec((tm, bw), lambda i: (i, 0)),
            pl.BlockSpec((tm, d), lambda i: (i, 0)),
            pl.BlockSpec((1, 6, d), lambda i: (i // tiles_per_seq, 0, 0)),
            pl.BlockSpec((4, d), lambda i: (0, 0)),
            pl.BlockSpec((aw, d), lambda i: (0, 0)),
            pl.BlockSpec((bw, d), lambda i: (0, 0)),
            pl.BlockSpec((d, LANES), lambda i: (0, 0)),
            pl.BlockSpec((1, LANES), lambda i: (0, 0)),
        ],
        out_specs=[
            pl.BlockSpec((tm, d), lambda i: (i, 0)),
            pl.BlockSpec((tm, d), lambda i: (i, 0)),
            pl.BlockSpec((tm, LANES), lambda i: (i, 0)),
            pl.BlockSpec((tm, LANES), lambda i: (i, 0)),
            pl.BlockSpec((8, LANES), lambda i: (0, 0)),
        ],
        out_shape=[
            jax.ShapeDtypeStruct((n, d), F32),
            jax.ShapeDtypeStruct((n, d), F32),
            jax.ShapeDtypeStruct((n, LANES), jnp.int32),
            jax.ShapeDtypeStruct((n, LANES), F32),
            jax.ShapeDtypeStruct((8, LANES), F32),
        ],
        scratch_shapes=[pltpu.VMEM((8, LANES), F32)],
        compiler_params=pltpu.CompilerParams(
            dimension_semantics=("arbitrary",), vmem_limit_bytes=48 * MIB),
        name="outproj_router",
    )(o_a, o_b, x2, mod3, norm_w, w_out_a, w_out_b, wr, br)


def _expert_kernel(be_ref, nu_ref, tok_ref, tok_next_ref, h_hbm, wup_ref, bup_ref, wdn_ref, bdn_ref, y_ref,
                   xbuf, sem, wup16, wdn16):
    j = pl.program_id(0)
    _, blk, d = xbuf.shape
    f = wdn16.shape[0]
    n_used = nu_ref[0]
    slot = lax.rem(j, 2)

    def start_gather(idx_ref, dst_slot):
        def issue(r, carry):
            t = idx_ref[0, 0, r]
            pltpu.make_async_copy(h_hbm.at[pl.ds(t, 1), :], xbuf.at[dst_slot, pl.ds(r, 1), :],
                                  sem.at[dst_slot]).start()
            return carry

        lax.fori_loop(0, blk, issue, 0, unroll=8)

    @pl.when(jnp.logical_and(j == 0, n_used > 0))
    def _():
        start_gather(tok_ref, 0)

    @pl.when(j + 1 < n_used)
    def _():
        start_gather(tok_next_ref, 1 - slot)

    prev = be_ref[jnp.maximum(j - 1, 0)]
    changed = jnp.logical_or(j == 0, be_ref[j] != prev)

    @pl.when(changed)
    def _():
        step = 128
        for r0 in range(0, d, step):
            wup16[r0:r0 + step, :] = wup_ref[0, r0:r0 + step, :].astype(BF16)
        for r0 in range(0, f, step):
            wdn16[r0:r0 + step, :] = wdn_ref[0, r0:r0 + step, :].astype(BF16)

    @pl.when(j < n_used)
    def _():
        pltpu.make_async_copy(h_hbm.at[pl.ds(0, blk), :], xbuf.at[slot], sem.at[slot]).wait()
        xb = xbuf[slot].astype(BF16)
        hu = _dot(xb, wup16[...]) + bup_ref[0]
        glu = jnp.minimum(hu[:, :f], SWIGLU_LIMIT)
        lin = jnp.clip(hu[:, f:], -SWIGLU_LIMIT, SWIGLU_LIMIT)
        act = glu * jax.nn.sigmoid(SWIGLU_ALPHA * glu) * (lin + 1.0)
        y_ref[...] = _dot(act.astype(BF16), wdn16[...]) + bdn_ref[0]

    @pl.when(j >= n_used)
    def _():
        y_ref[...] = jnp.zeros_like(y_ref)


def _experts(h2, slot_tok, block_e, n_used, w_up, b_up, w_down, b_down):
    n, d = h2.shape
    n_exp, _, f2 = w_up.shape
    f = w_down.shape[1]
    n_blocks = block_e.shape[0]
    blk = MOE_BLOCK
    tok3 = slot_tok.reshape(n_blocks, 1, blk)
    grid_spec = pltpu.PrefetchScalarGridSpec(
        num_scalar_prefetch=2,
        grid=(n_blocks,),
        in_specs=[
            pl.BlockSpec((1, 1, blk), lambda j, be, nu: (j, 0, 0), memory_space=pltpu.SMEM),
            pl.BlockSpec((1, 1, blk), lambda j, be, nu: (jnp.minimum(j + 1, n_blocks - 1), 0, 0),
                         memory_space=pltpu.SMEM),
            pl.BlockSpec(memory_space=pl.ANY),
            pl.BlockSpec((1, d, f2), lambda j, be, nu: (be[j], 0, 0)),
            pl.BlockSpec((1, 1, f2), lambda j, be, nu: (be[j], 0, 0)),
            pl.BlockSpec((1, f, d), lambda j, be, nu: (be[j], 0, 0)),
            pl.BlockSpec((1, 1, d), lambda j, be, nu: (be[j], 0, 0)),
        ],
        out_specs=pl.BlockSpec((blk, d), lambda j, be, nu: (j, 0)),
        scratch_shapes=[
            pltpu.VMEM((2, blk, d), F32),
            pltpu.SemaphoreType.DMA((2,)),
            pltpu.VMEM((d, f2), BF16),
            pltpu.VMEM((f, d), BF16),
        ],
    )
    return pl.pallas_call(
        _expert_kernel,
        grid_spec=grid_spec,
        out_shape=jax.ShapeDtypeStruct((n_blocks * blk, d), F32),
        compiler_params=pltpu.CompilerParams(
            dimension_semantics=("arbitrary",), vmem_limit_bytes=56 * MIB),
        name="experts",
    )(block_e, n_used, tok3, tok3, h2, w_up,
      b_up.reshape(n_exp, 1, f2), w_down, b_down.reshape(n_exp, 1, d))


def _combine_kernel(dest_ref, dest_next_ref, y_hbm, gate_ref, x1_ref, mod_ref, nw_ref, o_ref, buf, sem):
    i = pl.program_id(0)
    n_tiles = pl.num_programs(0)
    tm, d = x1_ref.shape
    slot = lax.rem(i, 2)

    def start_gather(idx_ref, dst_slot):
        def issue(t, carry):
            for k in range(TOP_K):
                s = idx_ref[0, 0, t * TOP_K + k]
                pltpu.make_async_copy(y_hbm.at[pl.ds(s, 1), :], buf.at[dst_slot, k, pl.ds(t, 1), :],
                                      sem.at[dst_slot]).start()
            return carry

        lax.fori_loop(0, tm, issue, 0, unroll=2)

    @pl.when(i == 0)
    def _():
        start_gather(dest_ref, 0)

    @pl.when(i + 1 < n_tiles)
    def _():
        start_gather(dest_next_ref, 1 - slot)

    for k in range(TOP_K):
        pltpu.make_async_copy(y_hbm.at[pl.ds(0, tm), :], buf.at[slot, k], sem.at[slot]).wait()

    gate = gate_ref[...]
    ysum = gate[:, 0:1] * buf[slot, 0]
    for k in range(1, TOP_K):
        ysum = ysum + gate[:, k:k + 1] * buf[slot, k]
    ga2 = mod_ref[0, 5:6, :]
    o_ref[...] = x1_ref[...] + ga2 * _rms(ysum, nw_ref[3:4, :])


def _combine(y_sorted, dest, gates, x1, mod3, norm_w, seq):
    n, d = x1.shape
    tm = 128
    n_tiles = n // tm
    tiles_per_seq = seq // tm
    dest3 = dest.reshape(n_tiles, 1, tm * TOP_K)
    return pl.pallas_call(
        _combine_kernel,
        grid=(n_tiles,),
        in_specs=[
            pl.BlockSpec((1, 1, tm * TOP_K), lambda i: (i, 0, 0), memory_space=pltpu.SMEM),
            pl.BlockSpec((1, 1, tm * TOP_K), lambda i: (jnp.minimum(i + 1, n_tiles - 1), 0, 0),
                         memory_space=pltpu.SMEM),
            pl.BlockSpec(memory_space=pl.ANY),
            pl.BlockSpec((tm, LANES), lambda i: (i, 0)),
            pl.BlockSpec((tm, d), lambda i: (i, 0)),
            pl.BlockSpec((1, 6, d), lambda i: (i // tiles_per_seq, 0, 0)),
            pl.BlockSpec((4, d), lambda i: (0, 0)),
        ],
        out_specs=pl.BlockSpec((tm, d), lambda i: (i, 0)),
        out_shape=jax.ShapeDtypeStruct((n, d), F32),
        scratch_shapes=[
            pltpu.VMEM((2, TOP_K, tm, d), F32),
            pltpu.SemaphoreType.DMA((2,)),
        ],
        compiler_params=pltpu.CompilerParams(
            dimension_semantics=("arbitrary",), vmem_limit_bytes=32 * MIB),
        name="combine",
    )(dest3, dest3, y_sorted, gates, x1, mod3, norm_w)


def _layer(x, c, w_ada, b_ada, norm_w, w_in, conv_w, a_log, dt_bias, a_norm_w, rel_bias, w_out,
           w_router, b_router, w_up, b_up, w_down, b_down):
    bsz, seq, d = x.shape
    n = bsz * seq
    a_width = conv_w.shape[1] // 3
    a_heads = a_log.shape[0]
    b_width = w_out.shape[0] - a_width
    n_experts = w_router.shape[1]
    off_gate = 4 * a_width
    off_b = off_gate + 2 * a_heads

    mod3 = _adaln(c, w_ada, b_ada).reshape(bsz, 6, d)
    x2 = x.reshape(n, d)

    w_main = jnp.concatenate([w_in[:, :off_gate], w_in[:, off_b:]], axis=1).astype(BF16)
    w_gate = jnp.zeros((d, LANES), F32).at[:, :2 * a_heads].set(w_in[:, off_gate:off_b]).astype(BF16)
    qkv_a, z_a, ab, qkv_b = _inproj(x2, mod3, norm_w, w_main, w_gate, seq, 3 * a_width, a_width, 3 * b_width)

    o_a = _gdn(qkv_a, z_a, ab, conv_w, a_log, dt_bias, a_norm_w, bsz, seq)
    o_b = _band_attn(qkv_b, rel_bias, bsz, seq)

    w_out16 = w_out.astype(BF16)
    x1, h2, route, gates, cnt = _outproj(o_a, o_b, x2, mod3, norm_w, w_out16[:a_width], w_out16[a_width:],
                                         w_router, b_router, seq)

    top_idx = route[:, :TOP_K]
    rank = route[:, TOP_K:2 * TOP_K]
    counts = cnt[0, :n_experts].astype(jnp.int32)
    padded = (counts + MOE_BLOCK - 1) // MOE_BLOCK * MOE_BLOCK
    pend = jnp.cumsum(padded)
    pstart = pend - padded
    expert_ids = jnp.arange(n_experts, dtype=jnp.int32)
    dest = jnp.sum(jnp.where(top_idx[..., None] == expert_ids, pstart, 0), axis=-1) + rank
    n_assign = n * TOP_K
    n_blocks = -(-n_assign // MOE_BLOCK) + n_experts
    cap = n_blocks * MOE_BLOCK
    flat_tok = jnp.repeat(jnp.arange(n, dtype=jnp.int32), TOP_K)
    slot_tok = jnp.zeros((cap,), jnp.int32).at[dest.reshape(-1)].set(flat_tok)
    block_start = jnp.arange(n_blocks, dtype=jnp.int32) * MOE_BLOCK
    block_e = jnp.minimum(jnp.sum(pend[None, :] <= block_start[:, None], axis=1), n_experts - 1).astype(jnp.int32)
    n_used = (pend[-1:] // MOE_BLOCK).astype(jnp.int32)

    y_sorted = _experts(h2, slot_tok, block_e, n_used, w_up, b_up, w_down, b_down)
    out = _combine(y_sorted, dest.astype(jnp.int32), gates, x1, mod3, norm_w, seq)
    return out.reshape(bsz, seq, d)


def kernel(x, c, w_ada, b_ada, norm_w, w_in, conv_w, a_log, dt_bias, a_norm_w, rel_bias, w_out,
           w_router, b_router, w_up, b_up, w_down, b_down):
    for l in range(w_ada.shape[0]):
        x = _layer(x, c, w_ada[l], b_ada[l], norm_w[l], w_in[l], conv_w[l], a_log[l], dt_bias[l],
                   a_norm_w[l], rel_bias[l], w_out[l], w_router[l], b_router[l], w_up[l], b_up[l],
                   w_down[l], b_down[l])
    return x
```

```python
import functools
import math

import jax
import jax.numpy as jnp
from jax import lax
from jax.experimental import pallas as pl
from jax.experimental.pallas import tpu as pltpu

F32 = jnp.float32
BF16 = jnp.bfloat16
HIGHEST = lax.Precision.HIGHEST

EPS = 1e-6
CHUNK = 64
CONV_K = 4
A_HEAD_DIM = 128
B_HEAD_DIM = 64
B_PREV_CHUNKS = 8
REL_CLIP = 128
TOP_K = 4
SWIGLU_ALPHA = 1.702
SWIGLU_LIMIT = 7.0
MOE_BLOCK = 256
LANES = 128
NEG_BIG = -1e30

MIB = 1024 * 1024


def _dot(a, b):
    return jnp.dot(a, b, preferred_element_type=F32)


def _dot_nt(a, b):
    return lax.dot_general(a, b, (((1,), (1,)), ((), ())), preferred_element_type=F32)


def _dot_exact(a, b):
    return jnp.dot(a, b, precision=HIGHEST, preferred_element_type=F32)


def _rms(x, w):
    return x * lax.rsqrt(jnp.mean(x * x, axis=-1, keepdims=True) + EPS) * w


def _adaln_kernel(c_ref, w_ref, b_ref, o_ref):
    cs = c_ref[...]
    cs = cs * jax.nn.sigmoid(cs)
    o_ref[...] = _dot_exact(cs, w_ref[...]) + b_ref[...]


def _adaln(c, w, b):
    bsz, d = c.shape
    n_out = w.shape[1]
    rows = 8
    cp = jnp.zeros((rows, d), F32).at[:bsz].set(c)
    tn = 512
    out = pl.pallas_call(
        _adaln_kernel,
        grid=(n_out // tn,),
        in_specs=[
            pl.BlockSpec((rows, d), lambda j: (0, 0)),
            pl.BlockSpec((d, tn), lambda j: (0, j)),
            pl.BlockSpec((1, tn), lambda j: (0, j)),
        ],
        out_specs=pl.BlockSpec((rows, tn), lambda j: (0, j)),
        out_shape=jax.ShapeDtypeStruct((rows, n_out), F32),
        name="adaln",
    )(cp, w, b.reshape(1, n_out))
    return out[:bsz]


def _inproj_kernel(x_ref, mod_ref, nw_ref, wm_ref, wg_ref, qkva_ref, z_ref, ab_ref, qkvb_ref,
                   *, a_qkv, a_z):
    x = x_ref[...]
    sh = mod_ref[0, 0:1, :]
    sc = mod_ref[0, 1:2, :]
    h = _rms(x, nw_ref[0:1, :]) * (1.0 + sc) + sh
    hb = h.astype(BF16)
    tn = 512
    for c0 in range(0, a_qkv, tn):
        qkva_ref[:, c0:c0 + tn] = _dot(hb, wm_ref[:, c0:c0 + tn])
    for c0 in range(0, a_z, tn):
        z_ref[:, c0:c0 + tn] = _dot(hb, wm_ref[:, a_qkv + c0:a_qkv + c0 + tn])
    off = a_qkv + a_z
    for c0 in range(0, qkvb_ref.shape[1], tn):
        qkvb_ref[:, c0:c0 + tn] = _dot(hb, wm_ref[:, off + c0:off + c0 + tn]).astype(BF16)
    ab_ref[...] = _dot(hb, wg_ref[...])


def _inproj(x2, mod3, norm_w, w_main, w_gate, seq, a_qkv, a_z, b_qkv):
    n, d = x2.shape
    tm = 512
    tiles_per_seq = seq // tm
    kern = functools.partial(_inproj_kernel, a_qkv=a_qkv, a_z=a_z)
    return pl.pallas_call(
        kern,
        grid=(n // tm,),
        in_specs=[
            pl.BlockSpec((tm, d), lambda i: (i, 0)),
            pl.BlockSpec((1, 6, d), lambda i: (i // tiles_per_seq, 0, 0)),
            pl.BlockSpec((4, d), lambda i: (0, 0)),
            pl.BlockSpec(w_main.shape, lambda i: (0, 0)),
            pl.BlockSpec(w_gate.shape, lambda i: (0, 0)),
        ],
        out_specs=[
            pl.BlockSpec((tm, a_qkv), lambda i: (i, 0)),
            pl.BlockSpec((tm, a_z), lambda i: (i, 0)),
            pl.BlockSpec((tm, LANES), lambda i: (i, 0)),
            pl.BlockSpec((tm, b_qkv), lambda i: (i, 0)),
        ],
        out_shape=[
            jax.ShapeDtypeStruct((n, a_qkv), F32),
            jax.ShapeDtypeStruct((n, a_z), F32),
            jax.ShapeDtypeStruct((n, LANES), F32),
            jax.ShapeDtypeStruct((n, b_qkv), BF16),
        ],
        compiler_params=pltpu.CompilerParams(
            dimension_semantics=("parallel",), vmem_limit_bytes=48 * MIB),
        name="inproj",
    )(x2, mod3, norm_w, w_main, w_gate)


def _gdn_kernel(qkv_ref, z_ref, ab_ref, cw_ref, alog_ref, dtb_ref, anw_ref, o_ref,
                xcat_ref, state_ref, *, n_heads):
    i = pl.program_id(1)
    tt = qkv_ref.shape[0]
    width = qkv_ref.shape[1]
    a_width = width // 3
    dk = A_HEAD_DIM
    sub = 2 * CHUNK
    per_sub = sub // CHUNK
    halo = 8

    @pl.when(i == 0)
    def _():
        xcat_ref[0:halo, :] = jnp.zeros((halo, width), F32)
        state_ref[...] = jnp.zeros_like(state_ref)

    xcat_ref[halo:halo + tt, :] = qkv_ref[...]

    def conv_silu(c0, r0):
        acc = cw_ref[CONV_K - 1:CONV_K, c0:c0 + dk] * qkv_ref[r0:r0 + sub, c0:c0 + dk]
        for j in range(CONV_K - 1):
            start = halo - (CONV_K - 1) + j + r0
            acc = acc + cw_ref[j:j + 1, c0:c0 + dk] * xcat_ref[start:start + sub, c0:c0 + dk]
        return acc * jax.nn.sigmoid(acc)

    def l2n(t):
        return t * lax.rsqrt(jnp.sum(t * t, axis=-1, keepdims=True) + EPS)

    row = lax.broadcasted_iota(jnp.int32, (sub, sub), 0)
    col = lax.broadcasted_iota(jnp.int32, (sub, sub), 1)
    shift = int(math.log2(CHUNK))
    same = (row >> shift) == (col >> shift)
    tri_incl = same & (col <= row)
    tri_strict = same & (col < row)
    eye = (row == col).astype(F32)
    pair = (row >> 1) == (col >> 1)
    couples = [((row >> (lv + 1)) == (col >> (lv + 1))) & ((row >> lv) != (col >> lv))
               for lv in range(1, shift)]
    col_chunk = lax.broadcasted_iota(jnp.int32, (dk, sub), 1) >> shift
    tri_incl_f = tri_incl.astype(F32)
    same_f = same.astype(F32)

    n_sub = tt // sub
    gates = []
    for p in range(n_sub):
        r0 = p * sub
        ab = ab_ref[r0:r0 + sub, :]
        gfull = -jnp.exp(alog_ref[...]) * jax.nn.softplus(ab + dtb_ref[...])
        bfull = jax.nn.sigmoid(ab)
        gcum = _dot_exact(tri_incl_f, gfull)
        gtot = _dot_exact(same_f, gfull)
        gates.append((gcum, gcum.T, gtot, bfull))

    items = [(p, h) for p in range(n_sub) for h in range(n_heads)]
    pre = {}
    for p, h in items:
        r0 = p * sub
        gcum, gcum_t, gtot, bfull = gates[p]
        gc = gcum[:, h:h + 1]
        beta = bfull[:, n_heads + h:n_heads + h + 1]
        q = l2n(conv_silu(h * dk, r0)) * (dk ** -0.5)
        k = l2n(conv_silu(a_width + h * dk, r0))
        v = conv_silu(2 * a_width + h * dk, r0)
        decay = jnp.exp(jnp.where(tri_incl, gc - gcum_t[h:h + 1, :], -jnp.inf))
        kb = k * beta
        k16 = k.astype(BF16)
        lower = jnp.where(tri_strict, _dot_nt(kb.astype(BF16), k16) * decay, 0.0)
        attn16 = (_dot_nt(q.astype(BF16), k16) * decay).astype(BF16)
        eg = jnp.exp(gc)
        rhs = jnp.concatenate([kb * eg, v * beta], axis=1).astype(BF16)
        kd_t = (k * jnp.exp(gtot[:, h:h + 1] - gc)).T
        kd_st = jnp.concatenate([jnp.where(col_chunk == c, kd_t, 0.0) for c in range(per_sub)],
                                axis=0).astype(BF16)
        pre[p, h] = dict(lower=lower, attn16=attn16, rhs=rhs, kd_st=kd_st, qeg=q * eg)

    tmat = {it: eye - jnp.where(pair, pre[it]["lower"], 0.0) for it in items}
    for couple in couples:
        t16 = {it: tmat[it].astype(BF16) for it in items}
        half = {it: _dot(t16[it], jnp.where(couple, pre[it]["lower"], 0.0).astype(BF16)) for it in items}
        tmat = {it: tmat[it] - _dot(half[it].astype(BF16), t16[it]) for it in items}

    wu16 = {it: _dot(tmat[it].astype(BF16), pre[it]["rhs"]).astype(BF16) for it in items}
    a_wu = {it: _dot(pre[it]["attn16"], wu16[it]) for it in items}
    k_wu = {it: _dot(pre[it]["kd_st"], wu16[it]) for it in items}
    qp16 = {it: (pre[it]["qeg"] - a_wu[it][:, :dk]).astype(BF16) for it in items}

    state = [state_ref[h] for h in range(n_heads)]
    for p in range(n_sub):
        r0 = p * sub
        gtot = gates[p][2]
        outs = [[] for _ in range(n_heads)]
        for c in range(per_sub):
            c0 = c * CHUNK
            for h in range(n_heads):
                kw = k_wu[p, h]
                m16 = kw[c * dk:(c + 1) * dk, :dk].astype(BF16)
                res = _dot(jnp.concatenate([m16, qp16[p, h][c0:c0 + CHUNK]], axis=0), state[h].astype(BF16))
                outs[h].append(res[dk:] + a_wu[p, h][c0:c0 + CHUNK, dk:])
                state[h] = (state[h] * jnp.exp(gtot[c0:c0 + 1, h:h + 1]) - res[:dk]
                            + kw[c * dk:(c + 1) * dk, dk:])
        for h in range(n_heads):
            o = jnp.concatenate(outs[h], axis=0)
            zh = z_ref[r0:r0 + sub, h * dk:(h + 1) * dk]
            o_ref[r0:r0 + sub, h * dk:(h + 1) * dk] = (
                _rms(o, anw_ref[...]) * (zh * jax.nn.sigmoid(zh))).astype(o_ref.dtype)
    for h in range(n_heads):
        state_ref[h] = state[h]

    xcat_ref[0:halo, :] = qkv_ref[tt - halo:tt, :]


def _gdn(qkv_a, z, ab, conv_w, a_log, dt_bias, a_norm_w, bsz, seq):
    n, width = qkv_a.shape
    a_width = width // 3
    n_heads = a_width // A_HEAD_DIM
    tt = 256
    nt = seq // tt
    alog = jnp.zeros((1, LANES), F32).at[0, :n_heads].set(a_log)
    dtb = jnp.zeros((1, LANES), F32).at[0, :n_heads].set(dt_bias)
    kern = functools.partial(_gdn_kernel, n_heads=n_heads)
    return pl.pallas_call(
        kern,
        grid=(bsz, nt),
        in_specs=[
            pl.BlockSpec((tt, width), lambda b, i: (b * nt + i, 0)),
            pl.BlockSpec((tt, a_width), lambda b, i: (b * nt + i, 0)),
            pl.BlockSpec((tt, LANES), lambda b, i: (b * nt + i, 0)),
            pl.BlockSpec((CONV_K, width), lambda b, i: (0, 0)),
            pl.BlockSpec((1, LANES), lambda b, i: (0, 0)),
            pl.BlockSpec((1, LANES), lambda b, i: (0, 0)),
            pl.BlockSpec((1, A_HEAD_DIM), lambda b, i: (0, 0)),
        ],
        out_specs=pl.BlockSpec((tt, a_width), lambda b, i: (b * nt + i, 0)),
        out_shape=jax.ShapeDtypeStruct((n, a_width), BF16),
        scratch_shapes=[
            pltpu.VMEM((tt + 8, width), F32),
            pltpu.VMEM((n_heads, A_HEAD_DIM, A_HEAD_DIM), F32),
        ],
        compiler_params=pltpu.CompilerParams(
            dimension_semantics=("arbitrary", "arbitrary"), vmem_limit_bytes=48 * MIB),
        name="gdn",
    )(qkv_a, z, ab, conv_w, alog, dtb, a_norm_w.reshape(1, A_HEAD_DIM))


def _attn_kernel(q_ref, k0_ref, k1_ref, k2_ref, v0_ref, v1_ref, v2_ref, bias_ref, o_ref, *, n_heads):
    i = pl.program_id(1)
    tq = q_ref.shape[0]
    hd = B_HEAD_DIM
    per = LANES // hd
    lane = lax.broadcasted_iota(jnp.int32, (1, LANES), 1)
    k_refs = (k0_ref, k1_ref, k2_ref)
    v_refs = (v0_ref, v1_ref, v2_ref)
    n_kb = len(k_refs)
    for g in range(n_heads // per):
        cs = slice(g * LANES, (g + 1) * LANES)
        qg = q_ref[:, cs]
        ks = [r[:, cs] for r in k_refs]
        vs = [r[:, cs] for r in v_refs]
        out = jnp.zeros((tq, LANES), F32)
        for hh in range(per):
            h = g * per + hh
            in_head = (lane >= hh * hd) & (lane < (hh + 1) * hd)
            qh = jnp.where(in_head, qg, jnp.zeros_like(qg)) * (hd ** -0.5)
            s = []
            for j in range(n_kb):
                sj = _dot_nt(qh, ks[j]) + bias_ref[h, :, j * tq:(j + 1) * tq]
                if j < n_kb - 1:
                    sj = jnp.where(i >= n_kb - 1 - j, sj, NEG_BIG)
                s.append(sj)
            m = s[0].max(axis=-1, keepdims=True)
            for sj in s[1:]:
                m = jnp.maximum(m, sj.max(axis=-1, keepdims=True))
            p = [jnp.exp(sj - m) for sj in s]
            den = p[0].sum(axis=-1, keepdims=True)
            for pj in p[1:]:
                den = den + pj.sum(axis=-1, keepdims=True)
            acc = _dot(p[0].astype(BF16), vs[0])
            for j in range(1, n_kb):
                acc = acc + _dot(p[j].astype(BF16), vs[j])
            out = jnp.where(in_head, acc / den, out)
        o_ref[:, cs] = out.astype(o_ref.dtype)


def _band_bias(rel_bias, tq, n_kb):
    n_h = rel_bias.shape[0]
    back = (n_kb - 1) * tq
    nk = n_kb * tq
    span = tq + nk - 1
    lo = (nk - 1 - back) - REL_CLIP
    hi = span - lo - (2 * REL_CLIP + 1)
    by_offset = jnp.concatenate([jnp.broadcast_to(rel_bias[:, :1], (n_h, lo)), rel_bias,
                                 jnp.broadcast_to(rel_bias[:, -1:], (n_h, hi))], axis=1).astype(F32)
    rev = jnp.pad(by_offset[:, ::-1], ((0, 0), (0, 1)))
    skew = jnp.broadcast_to(rev[:, None, :], (n_h, tq, span + 1)).reshape(n_h, tq * (span + 1))
    skew = skew[:, :tq * span].reshape(n_h, tq, span)
    table = skew[:, :, tq - 1:tq - 1 + nk]
    qc = jnp.arange(tq)[:, None] // CHUNK
    kc = jnp.arange(nk)[None, :] // CHUNK - back // CHUNK
    allowed = (kc <= qc) & (kc >= qc - B_PREV_CHUNKS)
    return jnp.where(allowed[None], table, NEG_BIG)


def _band_attn(qkv_b, rel_bias, bsz, seq):
    n, width = qkv_b.shape
    b_width = width // 3
    n_heads = b_width // B_HEAD_DIM
    tq = 256
    n_kb = 3
    assert (n_kb - 1) * tq == B_PREV_CHUNKS * CHUNK
    nt = seq // tq
    bias = _band_bias(rel_bias, tq, n_kb)
    kern = functools.partial(_attn_kernel, n_heads=n_heads)

    def kv_spec(colblk, back):
        return pl.BlockSpec((tq, b_width), lambda b, i: (b * nt + jnp.maximum(i - back, 0), colblk))

    return pl.pallas_call(
        kern,
        grid=(bsz, nt),
        in_specs=[
            pl.BlockSpec((tq, b_width), lambda b, i: (b * nt + i, 0)),
            kv_spec(1, 2), kv_spec(1, 1), kv_spec(1, 0),
            kv_spec(2, 2), kv_spec(2, 1), kv_spec(2, 0),
            pl.BlockSpec(bias.shape, lambda b, i: (0, 0, 0)),
        ],
        out_specs=pl.BlockSpec((tq, b_width), lambda b, i: (b * nt + i, 0)),
        out_shape=jax.ShapeDtypeStruct((n, b_width), BF16),
        compiler_params=pltpu.CompilerParams(
            dimension_semantics=("parallel", "parallel"), vmem_limit_bytes=48 * MIB),
        name="band_attn",
    )(qkv_b, qkv_b, qkv_b, qkv_b, qkv_b, qkv_b, qkv_b, bias)


def _outproj_kernel(oa_ref, ob_ref, x_ref, mod_ref, nw_ref, woa_ref, wob_ref, wr_ref, br_ref,
                    x1_ref, h2_ref, route_ref, gate_ref, cnt_ref, carry_ref, *, n_experts):
    i = pl.program_id(0)
    tm = x_ref.shape[0]

    @pl.when(i == 0)
    def _():
        carry_ref[...] = jnp.zeros_like(carry_ref)

    y = _dot(oa_ref[...], woa_ref[...]) + _dot(ob_ref[...], wob_ref[...])
    ga1 = mod_ref[0, 2:3, :]
    sh2 = mod_ref[0, 3:4, :]
    sc2 = mod_ref[0, 4:5, :]
    x1 = x_ref[...] + ga1 * _rms(y, nw_ref[1:2, :])
    x1_ref[...] = x1
    h2 = _rms(x1, nw_ref[2:3, :]) * (1.0 + sc2) + sh2
    h2_ref[...] = h2

    logits = _dot_exact(h2, wr_ref[...]) + br_ref[...]
    lane_i = lax.broadcasted_iota(jnp.int32, (tm, LANES), 1)
    lane = lane_i.astype(F32)
    lg = jnp.where(lane_i < n_experts, logits, -jnp.inf)
    vals, idxs = [], []
    for _ in range(TOP_K):
        m = lg.max(axis=-1, keepdims=True)
        idx = jnp.where(lg == m, lane, float(LANES)).min(axis=-1, keepdims=True)
        vals.append(m)
        idxs.append(idx)
        lg = jnp.where(lane == idx, -jnp.inf, lg)
    ex = [jnp.exp(v - vals[0]) for v in vals]
    den = ex[0]
    for e in ex[1:]:
        den = den + e

    onehot = jnp.zeros((tm, LANES), F32)
    for idx in idxs:
        onehot = onehot + (lane == idx).astype(F32)
    row = lax.broadcasted_iota(jnp.int32, (tm, tm), 0)
    col = lax.broadcasted_iota(jnp.int32, (tm, tm), 1)
    before = (col < row).astype(BF16)
    cum = _dot(before, onehot.astype(BF16)) + carry_ref[0:1, :]
    carry = carry_ref[0:1, :] + onehot.sum(axis=0, keepdims=True)
    carry_ref[...] = jnp.broadcast_to(carry, carry_ref.shape)
    cnt_ref[...] = jnp.broadcast_to(carry, cnt_ref.shape)

    route = jnp.zeros((tm, LANES), F32)
    gate = jnp.zeros((tm, LANES), F32)
    for k in range(TOP_K):
        rank = jnp.where(lane == idxs[k], cum, 0.0).sum(axis=-1, keepdims=True)
        route = jnp.where(lane_i == k, idxs[k], route)
        route = jnp.where(lane_i == TOP_K + k, rank, route)
        gate = jnp.where(lane_i == k, ex[k] / den, gate)
    route_ref[...] = route.astype(jnp.int32)
    gate_ref[...] = gate


def _outproj(o_a, o_b, x2, mod3, norm_w, w_out_a, w_out_b, w_router, b_router, seq):
    n, d = x2.shape
    n_experts = w_router.shape[1]
    tm = 512
    tiles_per_seq = seq // tm
    wr = jnp.zeros((d, LANES), F32).at[:, :n_experts].set(w_router)
    br = jnp.zeros((1, LANES), F32).at[0, :n_experts].set(b_router)
    kern = functools.partial(_outproj_kernel, n_experts=n_experts)
    aw = o_a.shape[1]
    bw = o_b.shape[1]
    return pl.pallas_call(
        kern,
        grid=(n // tm,),
        in_specs=[
            pl.BlockSpec((tm, aw), lambda i: (i, 0)),
            pl.BlockSpec((tm, bw), lambda i: (i, 0)),
            pl.BlockSpec((tm, d), lambda i: (i, 0)),
            pl.BlockSpec((1, 6, d), lambda i: (i // tiles_per_seq, 0, 0)),
            pl.BlockSpec((4, d), lambda i: (0, 0)),
            pl.BlockSpec((aw, d), lambda i: (0, 0)),
            pl.BlockSpec((bw, d), lambda i: (0, 0)),
            pl.BlockSpec((d, LANES), lambda i: (0, 0)),
            pl.BlockSpec((1, LANES), lambda i: (0, 0)),
        ],
        out_specs=[
            pl.BlockSpec((tm, d), lambda i: (i, 0)),
            pl.BlockSpec((tm, d), lambda i: (i, 0)),
            pl.BlockSpec((tm, LANES), lambda i: (i, 0)),
            pl.BlockSpec((tm, LANES), lambda i: (i, 0)),
            pl.BlockSpec((8, LANES), lambda i: (0, 0)),
        ],
        out_shape=[
            jax.ShapeDtypeStruct((n, d), F32),
            jax.ShapeDtypeStruct((n, d), F32),
            jax.ShapeDtypeStruct((n, LANES), jnp.int32),
            jax.ShapeDtypeStruct((n, LANES), F32),
            jax.ShapeDtypeStruct((8, LANES), F32),
        ],
        scratch_shapes=[pltpu.VMEM((8, LANES), F32)],
        compiler_params=pltpu.CompilerParams(
            dimension_semantics=("arbitrary",), vmem_limit_bytes=48 * MIB),
        name="outproj_router",
    )(o_a, o_b, x2, mod3, norm_w, w_out_a, w_out_b, wr, br)


def _dispatch_kernel(pend_ref, padded_ref, nu_ref, dest_ref, h_ref, xs_hbm, stage, zeros, sem, zsem, tsem,
                     *, n_experts, blk):
    i = pl.program_id(0)
    n_tiles = pl.num_programs(0)
    tm, d = h_ref.shape
    slot = lax.rem(i, 2)
    n_blocks = xs_hbm.shape[0] // blk

    def wait_rows(s):
        for _ in range(TOP_K):
            pltpu.make_async_copy(stage.at[s], xs_hbm.at[pl.ds(0, tm), :], sem.at[s]).wait()

    def clear_tail(b, carry):
        start = pl.multiple_of(b * blk, blk)
        pltpu.make_async_copy(zeros, xs_hbm.at[pl.ds(start, blk), :], tsem).start()
        return carry

    def wait_tail(b, carry):
        pltpu.make_async_copy(zeros, xs_hbm.at[pl.ds(0, blk), :], tsem).wait()
        return carry

    @pl.when(i == 0)
    def _():
        zeros[...] = jnp.zeros_like(zeros)
        for e in range(n_experts):
            @pl.when(padded_ref[e] > 0)
            def _():
                start = pl.multiple_of(pend_ref[e] - blk, blk)
                pltpu.make_async_copy(zeros, xs_hbm.at[pl.ds(start, blk), :], zsem).start()
        lax.fori_loop(nu_ref[0], n_blocks, clear_tail, 0)
        for e in range(n_experts):
            @pl.when(padded_ref[e] > 0)
            def _():
                pltpu.make_async_copy(zeros, xs_hbm.at[pl.ds(0, blk), :], zsem).wait()

    @pl.when(i >= 2)
    def _():
        wait_rows(slot)

    stage[slot] = h_ref[...]

    def issue(t, carry):
        for k in range(TOP_K):
            s = dest_ref[0, 0, t * TOP_K + k]
            pltpu.make_async_copy(stage.at[slot, pl.ds(t, 1), :], xs_hbm.at[pl.ds(s, 1), :], sem.at[slot]).start()
        return carry

    lax.fori_loop(0, tm, issue, 0, unroll=2)

    @pl.when(i == n_tiles - 1)
    def _():
        wait_rows(slot)

        @pl.when(n_tiles > 1)
        def _():
            wait_rows(1 - slot)

        lax.fori_loop(nu_ref[0], n_blocks, wait_tail, 0)


def _dispatch(h2, dest, pend, padded, n_used, cap):
    n, d = h2.shape
    n_experts = pend.shape[0]
    tm = 256
    n_tiles = n // tm
    kern = functools.partial(_dispatch_kernel, n_experts=n_experts, blk=MOE_BLOCK)
    grid_spec = pltpu.PrefetchScalarGridSpec(
        num_scalar_prefetch=3,
        grid=(n_tiles,),
        in_specs=[
            pl.BlockSpec((1, 1, tm * TOP_K), lambda i, pe, pa, nu: (i, 0, 0), memory_space=pltpu.SMEM),
            pl.BlockSpec((tm, d), lambda i, pe, pa, nu: (i, 0)),
        ],
        out_specs=pl.BlockSpec(memory_space=pl.ANY),
        scratch_shapes=[
            pltpu.VMEM((2, tm, d), F32),
            pltpu.VMEM((MOE_BLOCK, d), F32),
            pltpu.SemaphoreType.DMA((2,)),
            pltpu.SemaphoreType.DMA(()),
            pltpu.SemaphoreType.DMA(()),
        ],
    )
    return pl.pallas_call(
        kern,
        grid_spec=grid_spec,
        out_shape=jax.ShapeDtypeStruct((cap, d), F32),
        compiler_params=pltpu.CompilerParams(
            dimension_semantics=("arbitrary",), vmem_limit_bytes=32 * MIB),
        name="dispatch",
    )(pend, padded, n_used, dest.reshape(n_tiles, 1, tm * TOP_K), h2)


def _expert_kernel(be_ref, nu_ref, xs_ref, wup_ref, bup_ref, wdn_ref, bdn_ref, y_ref, wup16, wdn16):
    j = pl.program_id(0)
    d = xs_ref.shape[1]
    f = wdn16.shape[0]
    n_used = nu_ref[0]

    prev = be_ref[jnp.maximum(j - 1, 0)]
    changed = jnp.logical_or(j == 0, be_ref[j] != prev)

    @pl.when(changed)
    def _():
        step = 128
        for r0 in range(0, d, step):
            wup16[r0:r0 + step, :] = wup_ref[0, r0:r0 + step, :].astype(BF16)
        for r0 in range(0, f, step):
            wdn16[r0:r0 + step, :] = wdn_ref[0, r0:r0 + step, :].astype(BF16)

    @pl.when(j < n_used)
    def _():
        hu = _dot(xs_ref[...].astype(BF16), wup16[...]) + bup_ref[0]
        glu = jnp.minimum(hu[:, :f], SWIGLU_LIMIT)
        lin = jnp.clip(hu[:, f:], -SWIGLU_LIMIT, SWIGLU_LIMIT)
        act = glu * jax.nn.sigmoid(SWIGLU_ALPHA * glu) * (lin + 1.0)
        y_ref[...] = _dot(act.astype(BF16), wdn16[...]) + bdn_ref[0]

    @pl.when(j >= n_used)
    def _():
        y_ref[...] = jnp.zeros_like(y_ref)


def _experts(xs, block_e, n_used, w_up, b_up, w_down, b_down):
    cap, d = xs.shape
    n_exp, _, f2 = w_up.shape
    f = w_down.shape[1]
    blk = MOE_BLOCK
    n_blocks = cap // blk
    grid_spec = pltpu.PrefetchScalarGridSpec(
        num_scalar_prefetch=2,
        grid=(n_blocks,),
        in_specs=[
            pl.BlockSpec((blk, d), lambda j, be, nu: (jnp.minimum(j, jnp.maximum(nu[0] - 1, 0)), 0)),
            pl.BlockSpec((1, d, f2), lambda j, be, nu: (be[j], 0, 0)),
            pl.BlockSpec((1, 1, f2), lambda j, be, nu: (be[j], 0, 0)),
            pl.BlockSpec((1, f, d), lambda j, be, nu: (be[j], 0, 0)),
            pl.BlockSpec((1, 1, d), lambda j, be, nu: (be[j], 0, 0)),
        ],
        out_specs=pl.BlockSpec((blk, d), lambda j, be, nu: (j, 0)),
        scratch_shapes=[
            pltpu.VMEM((d, f2), BF16),
            pltpu.VMEM((f, d), BF16),
        ],
    )
    return pl.pallas_call(
        _expert_kernel,
        grid_spec=grid_spec,
        out_shape=jax.ShapeDtypeStruct((cap, d), F32),
        compiler_params=pltpu.CompilerParams(
            dimension_semantics=("arbitrary",), vmem_limit_bytes=56 * MIB),
        name="experts",
    )(block_e, n_used, xs, w_up, b_up.reshape(n_exp, 1, f2), w_down, b_down.reshape(n_exp, 1, d))


def _combine_kernel(dest_ref, dest_next_ref, y_hbm, gate_ref, x1_ref, mod_ref, nw_ref, o_ref, buf, sem):
    i = pl.program_id(0)
    n_tiles = pl.num_programs(0)
    tm, d = x1_ref.shape
    slot = lax.rem(i, 2)

    def start_gather(idx_ref, dst_slot):
        def issue(t, carry):
            for k in range(TOP_K):
                s = idx_ref[0, 0, t * TOP_K + k]
                pltpu.make_async_copy(y_hbm.at[pl.ds(s, 1), :], buf.at[dst_slot, k, pl.ds(t, 1), :],
                                      sem.at[dst_slot]).start()
            return carry

        lax.fori_loop(0, tm, issue, 0, unroll=2)

    @pl.when(i == 0)
    def _():
        start_gather(dest_ref, 0)

    @pl.when(i + 1 < n_tiles)
    def _():
        start_gather(dest_next_ref, 1 - slot)

    for k in range(TOP_K):
        pltpu.make_async_copy(y_hbm.at[pl.ds(0, tm), :], buf.at[slot, k], sem.at[slot]).wait()

    gate = gate_ref[...]
    ysum = gate[:, 0:1] * buf[slot, 0]
    for k in range(1, TOP_K):
        ysum = ysum + gate[:, k:k + 1] * buf[slot, k]
    ga2 = mod_ref[0, 5:6, :]
    o_ref[...] = x1_ref[...] + ga2 * _rms(ysum, nw_ref[3:4, :])


def _combine(y_sorted, dest, gates, x1, mod3, norm_w, seq):
    n, d = x1.shape
    tm = 128
    n_tiles = n // tm
    tiles_per_seq = seq // tm
    dest3 = dest.reshape(n_tiles, 1, tm * TOP_K)
    return pl.pallas_call(
        _combine_kernel,
        grid=(n_tiles,),
        in_specs=[
            pl.BlockSpec((1, 1, tm * TOP_K), lambda i: (i, 0, 0), memory_space=pltpu.SMEM),
            pl.BlockSpec((1, 1, tm * TOP_K), lambda i: (jnp.minimum(i + 1, n_tiles - 1), 0, 0),
                         memory_space=pltpu.SMEM),
            pl.BlockSpec(memory_space=pl.ANY),
            pl.BlockSpec((tm, LANES), lambda i: (i, 0)),
            pl.BlockSpec((tm, d), lambda i: (i, 0)),
            pl.BlockSpec((1, 6, d), lambda i: (i // tiles_per_seq, 0, 0)),
            pl.BlockSpec((4, d), lambda i: (0, 0)),
        ],
        out_specs=pl.BlockSpec((tm, d), lambda i: (i, 0)),
        out_shape=jax.ShapeDtypeStruct((n, d), F32),
        scratch_shapes=[
            pltpu.VMEM((2, TOP_K, tm, d), F32),
            pltpu.SemaphoreType.DMA((2,)),
        ],
        compiler_params=pltpu.CompilerParams(
            dimension_semantics=("arbitrary",), vmem_limit_bytes=32 * MIB),
        name="combine",
    )(dest3, dest3, y_sorted, gates, x1, mod3, norm_w)


def _layer(x, c, w_ada, b_ada, norm_w, w_in, conv_w, a_log, dt_bias, a_norm_w, rel_bias, w_out,
           w_router, b_router, w_up, b_up, w_down, b_down):
    bsz, seq, d = x.shape
    n = bsz * seq
    a_width = conv_w.shape[1] // 3
    a_heads = a_log.shape[0]
    b_width = w_out.shape[0] - a_width
    n_experts = w_router.shape[1]
    off_gate = 4 * a_width
    off_b = off_gate + 2 * a_heads

    mod3 = _adaln(c, w_ada, b_ada).reshape(bsz, 6, d)
    x2 = x.reshape(n, d)

    w_main = jnp.concatenate([w_in[:, :off_gate], w_in[:, off_b:]], axis=1).astype(BF16)
    w_gate = jnp.zeros((d, LANES), F32).at[:, :2 * a_heads].set(w_in[:, off_gate:off_b]).astype(BF16)
    qkv_a, z_a, ab, qkv_b = _inproj(x2, mod3, norm_w, w_main, w_gate, seq, 3 * a_width, a_width, 3 * b_width)

    o_a = _gdn(qkv_a, z_a, ab, conv_w, a_log, dt_bias, a_norm_w, bsz, seq)
    o_b = _band_attn(qkv_b, rel_bias, bsz, seq)

    w_out16 = w_out.astype(BF16)
    x1, h2, route, gates, cnt = _outproj(o_a, o_b, x2, mod3, norm_w, w_out16[:a_width], w_out16[a_width:],
                                         w_router, b_router, seq)

    top_idx = route[:, :TOP_K]
    rank = route[:, TOP_K:2 * TOP_K]
    counts = cnt[0, :n_experts].astype(jnp.int32)
    padded = (counts + MOE_BLOCK - 1) // MOE_BLOCK * MOE_BLOCK
    pend = jnp.cumsum(padded)
    pstart = pend - padded
    expert_ids = jnp.arange(n_experts, dtype=jnp.int32)
    dest = jnp.sum(jnp.where(top_idx[..., None] == expert_ids, pstart, 0), axis=-1) + rank
    n_assign = n * TOP_K
    n_blocks = -(-n_assign // MOE_BLOCK) + n_experts
    cap = n_blocks * MOE_BLOCK
    block_start = jnp.arange(n_blocks, dtype=jnp.int32) * MOE_BLOCK
    block_e = jnp.minimum(jnp.sum(pend[None, :] <= block_start[:, None], axis=1), n_experts - 1).astype(jnp.int32)
    n_used = (pend[-1:] // MOE_BLOCK).astype(jnp.int32)
    dest = dest.astype(jnp.int32)

    xs = _dispatch(h2, dest, pend.astype(jnp.int32), padded.astype(jnp.int32), n_used, cap)
    y_sorted = _experts(xs, block_e, n_used, w_up, b_up, w_down, b_down)
    out = _combine(y_sorted, dest, gates, x1, mod3, norm_w, seq)
    return out.reshape(bsz, seq, d)


def kernel(x, c, w_ada, b_ada, norm_w, w_in, conv_w, a_log, dt_bias, a_norm_w, rel_bias, w_out,
           w_router, b_router, w_up, b_up, w_down, b_down):
    for l in range(w_ada.shape[0]):
        x = _layer(x, c, w_ada[l], b_ada[l], norm_w[l], w_in[l], conv_w[l], a_log[l], dt_bias[l],
                   a_norm_w[l], rel_bias[l], w_out[l], w_router[l], b_router[l], w_up[l], b_up[l],
                   w_down[l], b_down[l])
    return x
```

```python
import functools
import math

import jax
import jax.numpy as jnp
from jax import lax
from jax.experimental import pallas as pl
from jax.experimental.pallas import tpu as pltpu

F32 = jnp.float32
BF16 = jnp.bfloat16
HIGHEST = lax.Precision.HIGHEST

EPS = 1e-6
CHUNK = 64
CONV_K = 4
A_HEAD_DIM = 128
B_HEAD_DIM = 64
B_PREV_CHUNKS = 8
REL_CLIP = 128
TOP_K = 4
SWIGLU_ALPHA = 1.702
SWIGLU_LIMIT = 7.0
MOE_BLOCK = 512
CLEAR_ROWS = 64
LANES = 128
NEG_BIG = -1e30

MIB = 1024 * 1024


def _dot(a, b):
    return jnp.dot(a, b, preferred_element_type=F32)


def _dot_nt(a, b):
    return lax.dot_general(a, b, (((1,), (1,)), ((), ())), preferred_element_type=F32)


def _dot_exact(a, b):
    return jnp.dot(a, b, precision=HIGHEST, preferred_element_type=F32)


def _rms(x, w):
    return x * lax.rsqrt(jnp.mean(x * x, axis=-1, keepdims=True) + EPS) * w


def _adaln_kernel(c_ref, w_ref, b_ref, o_ref):
    cs = c_ref[...]
    cs = cs * jax.nn.sigmoid(cs)
    o_ref[...] = _dot_exact(cs, w_ref[...]) + b_ref[...]


def _adaln(c, w, b):
    bsz, d = c.shape
    n_out = w.shape[1]
    rows = 8
    cp = jnp.zeros((rows, d), F32).at[:bsz].set(c)
    tn = 512
    out = pl.pallas_call(
        _adaln_kernel,
        grid=(n_out // tn,),
        in_specs=[
            pl.BlockSpec((rows, d), lambda j: (0, 0)),
            pl.BlockSpec((d, tn), lambda j: (0, j)),
            pl.BlockSpec((1, tn), lambda j: (0, j)),
        ],
        out_specs=pl.BlockSpec((rows, tn), lambda j: (0, j)),
        out_shape=jax.ShapeDtypeStruct((rows, n_out), F32),
        name="adaln",
    )(cp, w, b.reshape(1, n_out))
    return out[:bsz]


def _inproj_kernel(x_ref, mod_ref, nw_ref, wm_ref, wg_ref, qkva_ref, z_ref, ab_ref, qkvb_ref,
                   *, a_qkv, a_z):
    x = x_ref[...]
    sh = mod_ref[0, 0:1, :]
    sc = mod_ref[0, 1:2, :]
    h = _rms(x, nw_ref[0:1, :]) * (1.0 + sc) + sh
    hb = h.astype(BF16)
    tn = 512
    for c0 in range(0, a_qkv, tn):
        qkva_ref[:, c0:c0 + tn] = _dot(hb, wm_ref[:, c0:c0 + tn])
    for c0 in range(0, a_z, tn):
        z_ref[:, c0:c0 + tn] = _dot(hb, wm_ref[:, a_qkv + c0:a_qkv + c0 + tn])
    off = a_qkv + a_z
    for c0 in range(0, qkvb_ref.shape[1], tn):
        qkvb_ref[:, c0:c0 + tn] = _dot(hb, wm_ref[:, off + c0:off + c0 + tn]).astype(BF16)
    ab_ref[...] = _dot(hb, wg_ref[...])


def _inproj(x2, mod3, norm_w, w_main, w_gate, seq, a_qkv, a_z, b_qkv):
    n, d = x2.shape
    tm = 512
    tiles_per_seq = seq // tm
    kern = functools.partial(_inproj_kernel, a_qkv=a_qkv, a_z=a_z)
    return pl.pallas_call(
        kern,
        grid=(n // tm,),
        in_specs=[
            pl.BlockSpec((tm, d), lambda i: (i, 0)),
            pl.BlockSpec((1, 6, d), lambda i: (i // tiles_per_seq, 0, 0)),
            pl.BlockSpec((4, d), lambda i: (0, 0)),
            pl.BlockSpec(w_main.shape, lambda i: (0, 0)),
            pl.BlockSpec(w_gate.shape, lambda i: (0, 0)),
        ],
        out_specs=[
            pl.BlockSpec((tm, a_qkv), lambda i: (i, 0)),
            pl.BlockSpec((tm, a_z), lambda i: (i, 0)),
            pl.BlockSpec((tm, LANES), lambda i: (i, 0)),
            pl.BlockSpec((tm, b_qkv), lambda i: (i, 0)),
        ],
        out_shape=[
            jax.ShapeDtypeStruct((n, a_qkv), F32),
            jax.ShapeDtypeStruct((n, a_z), F32),
            jax.ShapeDtypeStruct((n, LANES), F32),
            jax.ShapeDtypeStruct((n, b_qkv), BF16),
        ],
        compiler_params=pltpu.CompilerParams(
            dimension_semantics=("parallel",), vmem_limit_bytes=48 * MIB),
        name="inproj",
    )(x2, mod3, norm_w, w_main, w_gate)


def _gdn_kernel(qkv_ref, z_ref, ab_ref, cw_ref, alog_ref, dtb_ref, anw_ref, o_ref,
                xcat_ref, state_ref, *, n_heads):
    i = pl.program_id(1)
    tt = qkv_ref.shape[0]
    width = qkv_ref.shape[1]
    a_width = width // 3
    dk = A_HEAD_DIM
    sub = 2 * CHUNK
    per_sub = sub // CHUNK
    halo = 8

    @pl.when(i == 0)
    def _():
        xcat_ref[0:halo, :] = jnp.zeros((halo, width), F32)
        state_ref[...] = jnp.zeros_like(state_ref)

    xcat_ref[halo:halo + tt, :] = qkv_ref[...]

    def conv_silu(c0, r0):
        acc = cw_ref[CONV_K - 1:CONV_K, c0:c0 + dk] * qkv_ref[r0:r0 + sub, c0:c0 + dk]
        for j in range(CONV_K - 1):
            start = halo - (CONV_K - 1) + j + r0
            acc = acc + cw_ref[j:j + 1, c0:c0 + dk] * xcat_ref[start:start + sub, c0:c0 + dk]
        return acc * jax.nn.sigmoid(acc)

    def l2n(t):
        return t * lax.rsqrt(jnp.sum(t * t, axis=-1, keepdims=True) + EPS)

    row = lax.broadcasted_iota(jnp.int32, (sub, sub), 0)
    col = lax.broadcasted_iota(jnp.int32, (sub, sub), 1)
    shift = int(math.log2(CHUNK))
    same = (row >> shift) == (col >> shift)
    tri_incl = same & (col <= row)
    tri_strict = same & (col < row)
    eye = (row == col).astype(F32)
    pair = (row >> 1) == (col >> 1)
    couples = [((row >> (lv + 1)) == (col >> (lv + 1))) & ((row >> lv) != (col >> lv))
               for lv in range(1, shift)]
    col_chunk = lax.broadcasted_iota(jnp.int32, (dk, sub), 1) >> shift
    tri_incl_f = tri_incl.astype(F32)
    same_f = same.astype(F32)

    n_sub = tt // sub
    gates = []
    for p in range(n_sub):
        r0 = p * sub
        ab = ab_ref[r0:r0 + sub, :]
        gfull = -jnp.exp(alog_ref[...]) * jax.nn.softplus(ab + dtb_ref[...])
        bfull = jax.nn.sigmoid(ab)
        gcum = _dot_exact(tri_incl_f, gfull)
        gtot = _dot_exact(same_f, gfull)
        gates.append((gcum, gcum.T, gtot, bfull))

    items = [(p, h) for p in range(n_sub) for h in range(n_heads)]
    pre = {}
    for p, h in items:
        r0 = p * sub
        gcum, gcum_t, gtot, bfull = gates[p]
        gc = gcum[:, h:h + 1]
        beta = bfull[:, n_heads + h:n_heads + h + 1]
        q = l2n(conv_silu(h * dk, r0)) * (dk ** -0.5)
        k = l2n(conv_silu(a_width + h * dk, r0))
        v = conv_silu(2 * a_width + h * dk, r0)
        decay = jnp.exp(jnp.where(tri_incl, gc - gcum_t[h:h + 1, :], -jnp.inf))
        kb = k * beta
        k16 = k.astype(BF16)
        lower = jnp.where(tri_strict, _dot_nt(kb.astype(BF16), k16) * decay, 0.0)
        attn16 = (_dot_nt(q.astype(BF16), k16) * decay).astype(BF16)
        eg = jnp.exp(gc)
        rhs = jnp.concatenate([kb * eg, v * beta], axis=1).astype(BF16)
        kd_t = (k * jnp.exp(gtot[:, h:h + 1] - gc)).T
        kd_st = jnp.concatenate([jnp.where(col_chunk == c, kd_t, 0.0) for c in range(per_sub)],
                                axis=0).astype(BF16)
        pre[p, h] = dict(lower=lower, attn16=attn16, rhs=rhs, kd_st=kd_st, qeg=q * eg)

    tmat = {it: eye - jnp.where(pair, pre[it]["lower"], 0.0) for it in items}
    for couple in couples:
        t16 = {it: tmat[it].astype(BF16) for it in items}
        half = {it: _dot(t16[it], jnp.where(couple, pre[it]["lower"], 0.0).astype(BF16)) for it in items}
        tmat = {it: tmat[it] - _dot(half[it].astype(BF16), t16[it]) for it in items}

    wu16 = {it: _dot(tmat[it].astype(BF16), pre[it]["rhs"]).astype(BF16) for it in items}
    a_wu = {it: _dot(pre[it]["attn16"], wu16[it]) for it in items}
    k_wu = {it: _dot(pre[it]["kd_st"], wu16[it]) for it in items}
    qp16 = {it: (pre[it]["qeg"] - a_wu[it][:, :dk]).astype(BF16) for it in items}

    state = [state_ref[h] for h in range(n_heads)]
    for p in range(n_sub):
        r0 = p * sub
        gtot = gates[p][2]
        outs = [[] for _ in range(n_heads)]
        for c in range(per_sub):
            c0 = c * CHUNK
            for h in range(n_heads):
                kw = k_wu[p, h]
                m16 = kw[c * dk:(c + 1) * dk, :dk].astype(BF16)
                res = _dot(jnp.concatenate([m16, qp16[p, h][c0:c0 + CHUNK]], axis=0), state[h].astype(BF16))
                outs[h].append(res[dk:] + a_wu[p, h][c0:c0 + CHUNK, dk:])
                state[h] = (state[h] * jnp.exp(gtot[c0:c0 + 1, h:h + 1]) - res[:dk]
                            + kw[c * dk:(c + 1) * dk, dk:])
        for h in range(n_heads):
            o = jnp.concatenate(outs[h], axis=0)
            zh = z_ref[r0:r0 + sub, h * dk:(h + 1) * dk]
            o_ref[r0:r0 + sub, h * dk:(h + 1) * dk] = (
                _rms(o, anw_ref[...]) * (zh * jax.nn.sigmoid(zh))).astype(o_ref.dtype)
    for h in range(n_heads):
        state_ref[h] = state[h]

    xcat_ref[0:halo, :] = qkv_ref[tt - halo:tt, :]


def _gdn(qkv_a, z, ab, conv_w, a_log, dt_bias, a_norm_w, bsz, seq):
    n, width = qkv_a.shape
    a_width = width // 3
    n_heads = a_width // A_HEAD_DIM
    tt = 256
    nt = seq // tt
    alog = jnp.zeros((1, LANES), F32).at[0, :n_heads].set(a_log)
    dtb = jnp.zeros((1, LANES), F32).at[0, :n_heads].set(dt_bias)
    kern = functools.partial(_gdn_kernel, n_heads=n_heads)
    return pl.pallas_call(
        kern,
        grid=(bsz, nt),
        in_specs=[
            pl.BlockSpec((tt, width), lambda b, i: (b * nt + i, 0)),
            pl.BlockSpec((tt, a_width), lambda b, i: (b * nt + i, 0)),
            pl.BlockSpec((tt, LANES), lambda b, i: (b * nt + i, 0)),
            pl.BlockSpec((CONV_K, width), lambda b, i: (0, 0)),
            pl.BlockSpec((1, LANES), lambda b, i: (0, 0)),
            pl.BlockSpec((1, LANES), lambda b, i: (0, 0)),
            pl.BlockSpec((1, A_HEAD_DIM), lambda b, i: (0, 0)),
        ],
        out_specs=pl.BlockSpec((tt, a_width), lambda b, i: (b * nt + i, 0)),
        out_shape=jax.ShapeDtypeStruct((n, a_width), BF16),
        scratch_shapes=[
            pltpu.VMEM((tt + 8, width), F32),
            pltpu.VMEM((n_heads, A_HEAD_DIM, A_HEAD_DIM), F32),
        ],
        compiler_params=pltpu.CompilerParams(
            dimension_semantics=("arbitrary", "arbitrary"), vmem_limit_bytes=48 * MIB),
        name="gdn",
    )(qkv_a, z, ab, conv_w, alog, dtb, a_norm_w.reshape(1, A_HEAD_DIM))


def _attn_kernel(q_ref, k0_ref, k1_ref, k2_ref, v0_ref, v1_ref, v2_ref, bias_ref, o_ref, *, n_heads):
    i = pl.program_id(1)
    tq = q_ref.shape[0]
    hd = B_HEAD_DIM
    per = LANES // hd
    lane = lax.broadcasted_iota(jnp.int32, (1, LANES), 1)
    k_refs = (k0_ref, k1_ref, k2_ref)
    v_refs = (v0_ref, v1_ref, v2_ref)
    n_kb = len(k_refs)
    for g in range(n_heads // per):
        cs = slice(g * LANES, (g + 1) * LANES)
        qg = q_ref[:, cs]
        ks = [r[:, cs] for r in k_refs]
        vs = [r[:, cs] for r in v_refs]
        out = jnp.zeros((tq, LANES), F32)
        for hh in range(per):
            h = g * per + hh
            in_head = (lane >= hh * hd) & (lane < (hh + 1) * hd)
            qh = jnp.where(in_head, qg, jnp.zeros_like(qg)) * (hd ** -0.5)
            s = []
            for j in range(n_kb):
                sj = _dot_nt(qh, ks[j]) + bias_ref[h, :, j * tq:(j + 1) * tq]
                if j < n_kb - 1:
                    sj = jnp.where(i >= n_kb - 1 - j, sj, NEG_BIG)
                s.append(sj)
            m = s[0].max(axis=-1, keepdims=True)
            for sj in s[1:]:
                m = jnp.maximum(m, sj.max(axis=-1, keepdims=True))
            p = [jnp.exp(sj - m) for sj in s]
            den = p[0].sum(axis=-1, keepdims=True)
            for pj in p[1:]:
                den = den + pj.sum(axis=-1, keepdims=True)
            acc = _dot(p[0].astype(BF16), vs[0])
            for j in range(1, n_kb):
                acc = acc + _dot(p[j].astype(BF16), vs[j])
            out = jnp.where(in_head, acc / den, out)
        o_ref[:, cs] = out.astype(o_ref.dtype)


def _band_bias(rel_bias, tq, n_kb):
    n_h = rel_bias.shape[0]
    back = (n_kb - 1) * tq
    nk = n_kb * tq
    span = tq + nk - 1
    lo = (nk - 1 - back) - REL_CLIP
    hi = span - lo - (2 * REL_CLIP + 1)
    by_offset = jnp.concatenate([jnp.broadcast_to(rel_bias[:, :1], (n_h, lo)), rel_bias,
                                 jnp.broadcast_to(rel_bias[:, -1:], (n_h, hi))], axis=1).astype(F32)
    rev = jnp.pad(by_offset[:, ::-1], ((0, 0), (0, 1)))
    skew = jnp.broadcast_to(rev[:, None, :], (n_h, tq, span + 1)).reshape(n_h, tq * (span + 1))
    skew = skew[:, :tq * span].reshape(n_h, tq, span)
    table = skew[:, :, tq - 1:tq - 1 + nk]
    qc = jnp.arange(tq)[:, None] // CHUNK
    kc = jnp.arange(nk)[None, :] // CHUNK - back // CHUNK
    allowed = (kc <= qc) & (kc >= qc - B_PREV_CHUNKS)
    return jnp.where(allowed[None], table, NEG_BIG)


def _band_attn(qkv_b, rel_bias, bsz, seq):
    n, width = qkv_b.shape
    b_width = width // 3
    n_heads = b_width // B_HEAD_DIM
    tq = 256
    n_kb = 3
    assert (n_kb - 1) * tq == B_PREV_CHUNKS * CHUNK
    nt = seq // tq
    bias = _band_bias(rel_bias, tq, n_kb)
    kern = functools.partial(_attn_kernel, n_heads=n_heads)

    def kv_spec(colblk, back):
        return pl.BlockSpec((tq, b_width), lambda b, i: (b * nt + jnp.maximum(i - back, 0), colblk))

    return pl.pallas_call(
        kern,
        grid=(bsz, nt),
        in_specs=[
            pl.BlockSpec((tq, b_width), lambda b, i: (b * nt + i, 0)),
            kv_spec(1, 2), kv_spec(1, 1), kv_spec(1, 0),
            kv_spec(2, 2), kv_spec(2, 1), kv_spec(2, 0),
            pl.BlockSpec(bias.shape, lambda b, i: (0, 0, 0)),
        ],
        out_specs=pl.BlockSpec((tq, b_width), lambda b, i: (b * nt + i, 0)),
        out_shape=jax.ShapeDtypeStruct((n, b_width), BF16),
        compiler_params=pltpu.CompilerParams(
            dimension_semantics=("parallel", "parallel"), vmem_limit_bytes=48 * MIB),
        name="band_attn",
    )(qkv_b, qkv_b, qkv_b, qkv_b, qkv_b, qkv_b, qkv_b, bias)


def _outproj_kernel(oa_ref, ob_ref, x_ref, mod_ref, nw_ref, woa_ref, wob_ref, wr_ref, br_ref,
                    x1_ref, h2_ref, route_ref, gate_ref, cnt_ref, carry_ref, *, n_experts):
    i = pl.program_id(0)
    tm = x_ref.shape[0]

    @pl.when(i == 0)
    def _():
        carry_ref[...] = jnp.zeros_like(carry_ref)

    y = _dot(oa_ref[...], woa_ref[...]) + _dot(ob_ref[...], wob_ref[...])
    ga1 = mod_ref[0, 2:3, :]
    sh2 = mod_ref[0, 3:4, :]
    sc2 = mod_ref[0, 4:5, :]
    x1 = x_ref[...] + ga1 * _rms(y, nw_ref[1:2, :])
    x1_ref[...] = x1
    h2 = _rms(x1, nw_ref[2:3, :]) * (1.0 + sc2) + sh2
    h2_ref[...] = h2

    h_hi = h2.astype(BF16)
    h_lo = (h2 - h_hi.astype(F32)).astype(BF16)
    logits = _dot(h_hi, wr_ref[0]) + (_dot(h_hi, wr_ref[1]) + _dot(h_lo, wr_ref[0])) + br_ref[...]
    lane_i = lax.broadcasted_iota(jnp.int32, (tm, LANES), 1)
    lane = lane_i.astype(F32)
    lg = jnp.where(lane_i < n_experts, logits, -jnp.inf)
    vals, idxs = [], []
    for _ in range(TOP_K):
        m = lg.max(axis=-1, keepdims=True)
        idx = jnp.where(lg == m, lane, float(LANES)).min(axis=-1, keepdims=True)
        vals.append(m)
        idxs.append(idx)
        lg = jnp.where(lane == idx, -jnp.inf, lg)
    ex = [jnp.exp(v - vals[0]) for v in vals]
    den = ex[0]
    for e in ex[1:]:
        den = den + e

    onehot = jnp.zeros((tm, LANES), F32)
    for idx in idxs:
        onehot = onehot + (lane == idx).astype(F32)
    row = lax.broadcasted_iota(jnp.int32, (tm, tm), 0)
    col = lax.broadcasted_iota(jnp.int32, (tm, tm), 1)
    before = (col < row).astype(BF16)
    cum = _dot(before, onehot.astype(BF16)) + carry_ref[0:1, :]
    carry = carry_ref[0:1, :] + onehot.sum(axis=0, keepdims=True)
    carry_ref[...] = jnp.broadcast_to(carry, carry_ref.shape)
    cnt_ref[...] = jnp.broadcast_to(carry, cnt_ref.shape)

    route = jnp.zeros((tm, LANES), F32)
    gate = jnp.zeros((tm, LANES), F32)
    for k in range(TOP_K):
        rank = jnp.where(lane == idxs[k], cum, 0.0).sum(axis=-1, keepdims=True)
        route = jnp.where(lane_i == k, idxs[k], route)
        route = jnp.where(lane_i == TOP_K + k, rank, route)
        gate = jnp.where(lane_i == k, ex[k] / den, gate)
    route_ref[...] = route.astype(jnp.int32)
    gate_ref[...] = gate


def _outproj(o_a, o_b, x2, mod3, norm_w, w_out_a, w_out_b, w_router, b_router, seq):
    n, d = x2.shape
    n_experts = w_router.shape[1]
    tm = 512
    tiles_per_seq = seq // tm
    wr32 = jnp.zeros((d, LANES), F32).at[:, :n_experts].set(w_router)
    wr_hi = wr32.astype(BF16)
    wr = jnp.stack([wr_hi, (wr32 - wr_hi.astype(F32)).astype(BF16)])
    br = jnp.zeros((1, LANES), F32).at[0, :n_experts].set(b_router)
    kern = functools.partial(_outproj_kernel, n_experts=n_experts)
    aw = o_a.shape[1]
    bw = o_b.shape[1]
    return pl.pallas_call(
        kern,
        grid=(n // tm,),
        in_specs=[
            pl.BlockSpec((tm, aw), lambda i: (i, 0)),
            pl.BlockSpec((tm, bw), lambda i: (i, 0)),
            pl.BlockSpec((tm, d), lambda i: (i, 0)),
            pl.BlockSpec((1, 6, d), lambda i: (i // tiles_per_seq, 0, 0)),
            pl.BlockSpec((4, d), lambda i: (0, 0)),
            pl.BlockSpec((aw, d), lambda i: (0, 0)),
            pl.BlockSpec((bw, d), lambda i: (0, 0)),
            pl.BlockSpec((2, d, LANES), lambda i: (0, 0, 0)),
            pl.BlockSpec((1, LANES), lambda i: (0, 0)),
        ],
        out_specs=[
            pl.BlockSpec((tm, d), lambda i: (i, 0)),
            pl.BlockSpec((tm, d), lambda i: (i, 0)),
            pl.BlockSpec((tm, LANES), lambda i: (i, 0)),
            pl.BlockSpec((tm, LANES), lambda i: (i, 0)),
            pl.BlockSpec((8, LANES), lambda i: (0, 0)),
        ],
        out_shape=[
            jax.ShapeDtypeStruct((n, d), F32),
            jax.ShapeDtypeStruct((n, d), F32),
            jax.ShapeDtypeStruct((n, LANES), jnp.int32),
            jax.ShapeDtypeStruct((n, LANES), F32),
            jax.ShapeDtypeStruct((8, LANES), F32),
        ],
        scratch_shapes=[pltpu.VMEM((8, LANES), F32)],
        compiler_params=pltpu.CompilerParams(
            dimension_semantics=("arbitrary",), vmem_limit_bytes=48 * MIB),
        name="outproj_router",
    )(o_a, o_b, x2, mod3, norm_w, w_out_a, w_out_b, wr, br)


def _dispatch_kernel(pend_ref, vend_ref, nu_ref, dest_ref, h_ref, xs_hbm, stage, zeros, sem, zsem, tsem,
                     *, n_experts, blk):
    i = pl.program_id(0)
    n_tiles = pl.num_programs(0)
    tm, d = h_ref.shape
    slot = lax.rem(i, 2)
    piece = zeros.shape[0]
    shift = int(math.log2(piece))
    n_pieces = xs_hbm.shape[0] // piece

    def wait_rows(s):
        for _ in range(TOP_K):
            pltpu.make_async_copy(stage.at[s], xs_hbm.at[pl.ds(0, tm), :], sem.at[s]).wait()

    def clear(sem_ref):
        def body(p, carry):
            start = pl.multiple_of(p * piece, piece)
            pltpu.make_async_copy(zeros, xs_hbm.at[pl.ds(start, piece), :], sem_ref).start()
            return carry
        return body

    def wait_clear(sem_ref):
        def body(p, carry):
            pltpu.make_async_copy(zeros, xs_hbm.at[pl.ds(0, piece), :], sem_ref).wait()
            return carry
        return body

    @pl.when(i == 0)
    def _():
        zeros[...] = jnp.zeros_like(zeros)
        for e in range(n_experts):
            lax.fori_loop(vend_ref[e] >> shift, pend_ref[e] >> shift, clear(zsem), 0)
        lax.fori_loop((nu_ref[0] * blk) >> shift, n_pieces, clear(tsem), 0)
        for e in range(n_experts):
            lax.fori_loop(vend_ref[e] >> shift, pend_ref[e] >> shift, wait_clear(zsem), 0)

    @pl.when(i >= 2)
    def _():
        wait_rows(slot)

    stage[slot] = h_ref[...]

    for t in range(tm):
        for k in range(TOP_K):
            s = dest_ref[0, 0, t * TOP_K + k]
            pltpu.make_async_copy(stage.at[slot, pl.ds(t, 1), :], xs_hbm.at[pl.ds(s, 1), :], sem.at[slot]).start()

    @pl.when(i == n_tiles - 1)
    def _():
        wait_rows(slot)

        @pl.when(n_tiles > 1)
        def _():
            wait_rows(1 - slot)

        lax.fori_loop((nu_ref[0] * blk) >> shift, n_pieces, wait_clear(tsem), 0)


def _dispatch(h2, dest, pend, vend, n_used, cap):
    n, d = h2.shape
    n_experts = pend.shape[0]
    tm = 256
    n_tiles = n // tm
    kern = functools.partial(_dispatch_kernel, n_experts=n_experts, blk=MOE_BLOCK)
    grid_spec = pltpu.PrefetchScalarGridSpec(
        num_scalar_prefetch=3,
        grid=(n_tiles,),
        in_specs=[
            pl.BlockSpec((1, 1, tm * TOP_K), lambda i, pe, pa, nu: (i, 0, 0), memory_space=pltpu.SMEM),
            pl.BlockSpec((tm, d), lambda i, pe, pa, nu: (i, 0)),
        ],
        out_specs=pl.BlockSpec(memory_space=pl.ANY),
        scratch_shapes=[
            pltpu.VMEM((2, tm, d), F32),
            pltpu.VMEM((CLEAR_ROWS, d), F32),
            pltpu.SemaphoreType.DMA((2,)),
            pltpu.SemaphoreType.DMA(()),
            pltpu.SemaphoreType.DMA(()),
        ],
    )
    return pl.pallas_call(
        kern,
        grid_spec=grid_spec,
        out_shape=jax.ShapeDtypeStruct((cap, d), F32),
        compiler_params=pltpu.CompilerParams(
            dimension_semantics=("arbitrary",), vmem_limit_bytes=32 * MIB),
        name="dispatch",
    )(pend, vend, n_used, dest.reshape(n_tiles, 1, tm * TOP_K), h2)


def _expert_kernel(be_ref, nu_ref, xs_ref, wup_ref, bup_ref, wdn_ref, bdn_ref, y_ref, wup16, wdn16):
    j = pl.program_id(0)
    d = xs_ref.shape[1]
    f = wdn16.shape[0]
    n_used = nu_ref[0]

    prev = be_ref[jnp.maximum(j - 1, 0)]
    changed = jnp.logical_or(j == 0, be_ref[j] != prev)

    @pl.when(changed)
    def _():
        step = 128
        for r0 in range(0, d, step):
            wup16[r0:r0 + step, :] = wup_ref[0, r0:r0 + step, :].astype(BF16)
        for r0 in range(0, f, step):
            wdn16[r0:r0 + step, :] = wdn_ref[0, r0:r0 + step, :].astype(BF16)

    @pl.when(j < n_used)
    def _():
        xb = xs_ref[...].astype(BF16)
        fc = min(f, 512)
        acc = None
        for c0 in range(0, f, fc):
            glu = _dot(xb, wup16[:, c0:c0 + fc]) + bup_ref[0, :, c0:c0 + fc]
            lin = _dot(xb, wup16[:, f + c0:f + c0 + fc]) + bup_ref[0, :, f + c0:f + c0 + fc]
            glu = jnp.minimum(glu, SWIGLU_LIMIT)
            lin = jnp.clip(lin, -SWIGLU_LIMIT, SWIGLU_LIMIT)
            act = glu * jax.nn.sigmoid(SWIGLU_ALPHA * glu) * (lin + 1.0)
            part = _dot(act.astype(BF16), wdn16[c0:c0 + fc, :])
            acc = part if acc is None else acc + part
        y_ref[...] = acc + bdn_ref[0]

    @pl.when(j >= n_used)
    def _():
        y_ref[...] = jnp.zeros_like(y_ref)


def _experts(xs, block_e, n_used, w_up, b_up, w_down, b_down):
    cap, d = xs.shape
    n_exp, _, f2 = w_up.shape
    f = w_down.shape[1]
    blk = MOE_BLOCK
    n_blocks = cap // blk
    grid_spec = pltpu.PrefetchScalarGridSpec(
        num_scalar_prefetch=2,
        grid=(n_blocks,),
        in_specs=[
            pl.BlockSpec((blk, d), lambda j, be, nu: (jnp.minimum(j, jnp.maximum(nu[0] - 1, 0)), 0)),
            pl.BlockSpec((1, d, f2), lambda j, be, nu: (be[j], 0, 0)),
            pl.BlockSpec((1, 1, f2), lambda j, be, nu: (be[j], 0, 0)),
            pl.BlockSpec((1, f, d), lambda j, be, nu: (be[j], 0, 0)),
            pl.BlockSpec((1, 1, d), lambda j, be, nu: (be[j], 0, 0)),
        ],
        out_specs=pl.BlockSpec((blk, d), lambda j, be, nu: (j, 0)),
        scratch_shapes=[
            pltpu.VMEM((d, f2), BF16),
            pltpu.VMEM((f, d), BF16),
        ],
    )
    return pl.pallas_call(
        _expert_kernel,
        grid_spec=grid_spec,
        out_shape=jax.ShapeDtypeStruct((cap, d), F32),
        compiler_params=pltpu.CompilerParams(
            dimension_semantics=("arbitrary",), vmem_limit_bytes=56 * MIB),
        name="experts",
    )(block_e, n_used, xs, w_up, b_up.reshape(n_exp, 1, f2), w_down, b_down.reshape(n_exp, 1, d))


def _combine_kernel(dest_ref, dest_next_ref, y_hbm, gate_ref, x1_ref, mod_ref, nw_ref, o_ref, buf, sem):
    i = pl.program_id(0)
    n_tiles = pl.num_programs(0)
    tm, d = x1_ref.shape
    slot = lax.rem(i, 2)

    def start_gather(idx_ref, dst_slot):
        for t in range(tm):
            for k in range(TOP_K):
                s = idx_ref[0, 0, t * TOP_K + k]
                pltpu.make_async_copy(y_hbm.at[pl.ds(s, 1), :], buf.at[dst_slot, k, pl.ds(t, 1), :],
                                      sem.at[dst_slot]).start()

    @pl.when(i == 0)
    def _():
        start_gather(dest_ref, 0)

    @pl.when(i + 1 < n_tiles)
    def _():
        start_gather(dest_next_ref, 1 - slot)

    for k in range(TOP_K):
        pltpu.make_async_copy(y_hbm.at[pl.ds(0, tm), :], buf.at[slot, k], sem.at[slot]).wait()

    gate = gate_ref[...]
    ysum = gate[:, 0:1] * buf[slot, 0]
    for k in range(1, TOP_K):
        ysum = ysum + gate[:, k:k + 1] * buf[slot, k]
    ga2 = mod_ref[0, 5:6, :]
    o_ref[...] = x1_ref[...] + ga2 * _rms(ysum, nw_ref[3:4, :])


def _combine(y_sorted, dest, gates, x1, mod3, norm_w, seq):
    n, d = x1.shape
    tm = 128
    n_tiles = n // tm
    tiles_per_seq = seq // tm
    dest3 = dest.reshape(n_tiles, 1, tm * TOP_K)
    return pl.pallas_call(
        _combine_kernel,
        grid=(n_tiles,),
        in_specs=[
            pl.BlockSpec((1, 1, tm * TOP_K), lambda i: (i, 0, 0), memory_space=pltpu.SMEM),
            pl.BlockSpec((1, 1, tm * TOP_K), lambda i: (jnp.minimum(i + 1, n_tiles - 1), 0, 0),
                         memory_space=pltpu.SMEM),
            pl.BlockSpec(memory_space=pl.ANY),
            pl.BlockSpec((tm, LANES), lambda i: (i, 0)),
            pl.BlockSpec((tm, d), lambda i: (i, 0)),
            pl.BlockSpec((1, 6, d), lambda i: (i // tiles_per_seq, 0, 0)),
            pl.BlockSpec((4, d), lambda i: (0, 0)),
        ],
        out_specs=pl.BlockSpec((tm, d), lambda i: (i, 0)),
        out_shape=jax.ShapeDtypeStruct((n, d), F32),
        scratch_shapes=[
            pltpu.VMEM((2, TOP_K, tm, d), F32),
            pltpu.SemaphoreType.DMA((2,)),
        ],
        compiler_params=pltpu.CompilerParams(
            dimension_semantics=("arbitrary",), vmem_limit_bytes=32 * MIB),
        name="combine",
    )(dest3, dest3, y_sorted, gates, x1, mod3, norm_w)


def _layer(x, c, w_ada, b_ada, norm_w, w_in, conv_w, a_log, dt_bias, a_norm_w, rel_bias, w_out,
           w_router, b_router, w_up, b_up, w_down, b_down):
    bsz, seq, d = x.shape
    n = bsz * seq
    a_width = conv_w.shape[1] // 3
    a_heads = a_log.shape[0]
    b_width = w_out.shape[0] - a_width
    n_experts = w_router.shape[1]
    off_gate = 4 * a_width
    off_b = off_gate + 2 * a_heads

    mod3 = _adaln(c, w_ada, b_ada).reshape(bsz, 6, d)
    x2 = x.reshape(n, d)

    w_main = jnp.concatenate([w_in[:, :off_gate], w_in[:, off_b:]], axis=1).astype(BF16)
    w_gate = jnp.zeros((d, LANES), F32).at[:, :2 * a_heads].set(w_in[:, off_gate:off_b]).astype(BF16)
    qkv_a, z_a, ab, qkv_b = _inproj(x2, mod3, norm_w, w_main, w_gate, seq, 3 * a_width, a_width, 3 * b_width)

    o_a = _gdn(qkv_a, z_a, ab, conv_w, a_log, dt_bias, a_norm_w, bsz, seq)
    o_b = _band_attn(qkv_b, rel_bias, bsz, seq)

    w_out16 = w_out.astype(BF16)
    x1, h2, route, gates, cnt = _outproj(o_a, o_b, x2, mod3, norm_w, w_out16[:a_width], w_out16[a_width:],
                                         w_router, b_router, seq)

    top_idx = route[:, :TOP_K]
    rank = route[:, TOP_K:2 * TOP_K]
    counts = cnt[0, :n_experts].astype(jnp.int32)
    padded = (counts + MOE_BLOCK - 1) // MOE_BLOCK * MOE_BLOCK
    pend = jnp.cumsum(padded)
    pstart = pend - padded
    expert_ids = jnp.arange(n_experts, dtype=jnp.int32)
    dest = jnp.sum(jnp.where(top_idx[..., None] == expert_ids, pstart, 0), axis=-1) + rank
    n_assign = n * TOP_K
    n_blocks = -(-n_assign // MOE_BLOCK) + n_experts
    cap = n_blocks * MOE_BLOCK
    block_start = jnp.arange(n_blocks, dtype=jnp.int32) * MOE_BLOCK
    block_e = jnp.minimum(jnp.sum(pend[None, :] <= block_start[:, None], axis=1), n_experts - 1).astype(jnp.int32)
    n_used = (pend[-1:] // MOE_BLOCK).astype(jnp.int32)
    dest = dest.astype(jnp.int32)

    xs = _dispatch(h2, dest, pend.astype(jnp.int32), (pstart + counts).astype(jnp.int32), n_used, cap)
    y_sorted = _experts(xs, block_e, n_used, w_up, b_up, w_down, b_down)
    out = _combine(y_sorted, dest, gates, x1, mod3, norm_w, seq)
    return out.reshape(bsz, seq, d)


def kernel(x, c, w_ada, b_ada, norm_w, w_in, conv_w, a_log, dt_bias, a_norm_w, rel_bias, w_out,
           w_router, b_router, w_up, b_up, w_down, b_down):
    for l in range(w_ada.shape[0]):
        x = _layer(x, c, w_ada[l], b_ada[l], norm_w[l], w_in[l], conv_w[l], a_log[l], dt_bias[l],
                   a_norm_w[l], rel_bias[l], w_out[l], w_router[l], b_router[l], w_up[l], b_up[l],
                   w_down[l], b_down[l])
    return x
```

```python
import functools
import math

import jax
import jax.numpy as jnp
from jax import lax
from jax.experimental import pallas as pl
from jax.experimental.pallas import tpu as pltpu

F32 = jnp.float32
BF16 = jnp.bfloat16
HIGHEST = lax.Precision.HIGHEST

EPS = 1e-6
CHUNK = 64
CONV_K = 4
A_HEAD_DIM = 128
B_HEAD_DIM = 64
B_PREV_CHUNKS = 8
REL_CLIP = 128
TOP_K = 4
SWIGLU_ALPHA = 1.702
SWIGLU_LIMIT = 7.0
MOE_BLOCK = 512
CLEAR_ROWS = 64
LANES = 128
NEG_BIG = -1e30

MIB = 1024 * 1024


def _dot(a, b):
    return jnp.dot(a, b, preferred_element_type=F32)


def _dot_nt(a, b):
    return lax.dot_general(a, b, (((1,), (1,)), ((), ())), preferred_element_type=F32)


def _dot_exact(a, b):
    return jnp.dot(a, b, precision=HIGHEST, preferred_element_type=F32)


def _rms(x, w):
    return x * lax.rsqrt(jnp.mean(x * x, axis=-1, keepdims=True) + EPS) * w


def _adaln_kernel(c_ref, w_ref, b_ref, o_ref):
    cs = c_ref[...]
    cs = cs * jax.nn.sigmoid(cs)
    o_ref[...] = _dot_exact(cs, w_ref[...]) + b_ref[...]


def _adaln(c, w, b):
    bsz, d = c.shape
    n_out = w.shape[1]
    rows = 8
    cp = jnp.zeros((rows, d), F32).at[:bsz].set(c)
    tn = 512
    out = pl.pallas_call(
        _adaln_kernel,
        grid=(n_out // tn,),
        in_specs=[
            pl.BlockSpec((rows, d), lambda j: (0, 0)),
            pl.BlockSpec((d, tn), lambda j: (0, j)),
            pl.BlockSpec((1, tn), lambda j: (0, j)),
        ],
        out_specs=pl.BlockSpec((rows, tn), lambda j: (0, j)),
        out_shape=jax.ShapeDtypeStruct((rows, n_out), F32),
        name="adaln",
    )(cp, w, b.reshape(1, n_out))
    return out[:bsz]


def _inproj_kernel(x_ref, mod_ref, nw_ref, wm_ref, wg_ref, qkva_ref, z_ref, ab_ref, qkvb_ref,
                   *, a_qkv, a_z):
    x = x_ref[...]
    sh = mod_ref[0, 0:1, :]
    sc = mod_ref[0, 1:2, :]
    h = _rms(x, nw_ref[0:1, :]) * (1.0 + sc) + sh
    hb = h.astype(BF16)
    tn = 512
    for c0 in range(0, a_qkv, tn):
        qkva_ref[:, c0:c0 + tn] = _dot(hb, wm_ref[:, c0:c0 + tn])
    for c0 in range(0, a_z, tn):
        z_ref[:, c0:c0 + tn] = _dot(hb, wm_ref[:, a_qkv + c0:a_qkv + c0 + tn])
    off = a_qkv + a_z
    for c0 in range(0, qkvb_ref.shape[1], tn):
        qkvb_ref[:, c0:c0 + tn] = _dot(hb, wm_ref[:, off + c0:off + c0 + tn]).astype(BF16)
    ab_ref[...] = _dot(hb, wg_ref[...])


def _inproj(x2, mod3, norm_w, w_main, w_gate, seq, a_qkv, a_z, b_qkv):
    n, d = x2.shape
    tm = 512
    tiles_per_seq = seq // tm
    kern = functools.partial(_inproj_kernel, a_qkv=a_qkv, a_z=a_z)
    return pl.pallas_call(
        kern,
        grid=(n // tm,),
        in_specs=[
            pl.BlockSpec((tm, d), lambda i: (i, 0)),
            pl.BlockSpec((1, 6, d), lambda i: (i // tiles_per_seq, 0, 0)),
            pl.BlockSpec((4, d), lambda i: (0, 0)),
            pl.BlockSpec(w_main.shape, lambda i: (0, 0)),
            pl.BlockSpec(w_gate.shape, lambda i: (0, 0)),
        ],
        out_specs=[
            pl.BlockSpec((tm, a_qkv), lambda i: (i, 0)),
            pl.BlockSpec((tm, a_z), lambda i: (i, 0)),
            pl.BlockSpec((tm, LANES), lambda i: (i, 0)),
            pl.BlockSpec((tm, b_qkv), lambda i: (i, 0)),
        ],
        out_shape=[
            jax.ShapeDtypeStruct((n, a_qkv), F32),
            jax.ShapeDtypeStruct((n, a_z), F32),
            jax.ShapeDtypeStruct((n, LANES), F32),
            jax.ShapeDtypeStruct((n, b_qkv), BF16),
        ],
        compiler_params=pltpu.CompilerParams(
            dimension_semantics=("parallel",), vmem_limit_bytes=48 * MIB),
        name="inproj",
    )(x2, mod3, norm_w, w_main, w_gate)


def _gdn_kernel(qkv_ref, z_ref, ab_ref, cw_ref, alog_ref, dtb_ref, anw_ref, o_ref,
                xcat_ref, state_ref, *, n_heads):
    i = pl.program_id(1)
    tt = qkv_ref.shape[0]
    width = qkv_ref.shape[1]
    a_width = width // 3
    dk = A_HEAD_DIM
    sub = 2 * CHUNK
    per_sub = sub // CHUNK
    halo = 8

    @pl.when(i == 0)
    def _():
        xcat_ref[0:halo, :] = jnp.zeros((halo, width), F32)
        state_ref[...] = jnp.zeros_like(state_ref)

    xcat_ref[halo:halo + tt, :] = qkv_ref[...]

    def conv_silu(c0, r0):
        acc = cw_ref[CONV_K - 1:CONV_K, c0:c0 + dk] * qkv_ref[r0:r0 + sub, c0:c0 + dk]
        for j in range(CONV_K - 1):
            start = halo - (CONV_K - 1) + j + r0
            acc = acc + cw_ref[j:j + 1, c0:c0 + dk] * xcat_ref[start:start + sub, c0:c0 + dk]
        return acc * jax.nn.sigmoid(acc)

    def l2n(t):
        return t * lax.rsqrt(jnp.sum(t * t, axis=-1, keepdims=True) + EPS)

    row = lax.broadcasted_iota(jnp.int32, (sub, sub), 0)
    col = lax.broadcasted_iota(jnp.int32, (sub, sub), 1)
    shift = int(math.log2(CHUNK))
    same = (row >> shift) == (col >> shift)
    tri_incl = same & (col <= row)
    tri_strict = same & (col < row)
    eye = (row == col).astype(F32)
    pair = (row >> 1) == (col >> 1)
    couples = [((row >> (lv + 1)) == (col >> (lv + 1))) & ((row >> lv) != (col >> lv))
               for lv in range(1, shift)]
    col_chunk = lax.broadcasted_iota(jnp.int32, (dk, sub), 1) >> shift
    tri_incl_f = tri_incl.astype(F32)
    same_f = same.astype(F32)

    n_sub = tt // sub
    gates = []
    for p in range(n_sub):
        r0 = p * sub
        ab = ab_ref[r0:r0 + sub, :]
        gfull = -jnp.exp(alog_ref[...]) * jax.nn.softplus(ab + dtb_ref[...])
        bfull = jax.nn.sigmoid(ab)
        gcum = _dot_exact(tri_incl_f, gfull)
        gtot = _dot_exact(same_f, gfull)
        gates.append((gcum, gcum.T, gtot, bfull))

    items = [(p, h) for p in range(n_sub) for h in range(n_heads)]
    pre = {}
    for p, h in items:
        r0 = p * sub
        gcum, gcum_t, gtot, bfull = gates[p]
        gc = gcum[:, h:h + 1]
        beta = bfull[:, n_heads + h:n_heads + h + 1]
        q = l2n(conv_silu(h * dk, r0)) * (dk ** -0.5)
        k = l2n(conv_silu(a_width + h * dk, r0))
        v = conv_silu(2 * a_width + h * dk, r0)
        decay = jnp.exp(jnp.where(tri_incl, gc - gcum_t[h:h + 1, :], -jnp.inf))
        kb = k * beta
        k16 = k.astype(BF16)
        lower = jnp.where(tri_strict, _dot_nt(kb.astype(BF16), k16) * decay, 0.0)
        attn16 = (_dot_nt(q.astype(BF16), k16) * decay).astype(BF16)
        eg = jnp.exp(gc)
        rhs = jnp.concatenate([kb * eg, v * beta], axis=1).astype(BF16)
        kd_t = (k * jnp.exp(gtot[:, h:h + 1] - gc)).T
        kd_st = jnp.concatenate([jnp.where(col_chunk == c, kd_t, 0.0) for c in range(per_sub)],
                                axis=0).astype(BF16)
        pre[p, h] = dict(lower=lower, attn16=attn16, rhs=rhs, kd_st=kd_st, qeg=q * eg)

    tmat = {it: eye - jnp.where(pair, pre[it]["lower"], 0.0) for it in items}
    for couple in couples:
        t16 = {it: tmat[it].astype(BF16) for it in items}
        half = {it: _dot(t16[it], jnp.where(couple, pre[it]["lower"], 0.0).astype(BF16)) for it in items}
        tmat = {it: tmat[it] - _dot(half[it].astype(BF16), t16[it]) for it in items}

    wu16 = {it: _dot(tmat[it].astype(BF16), pre[it]["rhs"]).astype(BF16) for it in items}
    a_wu = {it: _dot(pre[it]["attn16"], wu16[it]) for it in items}
    k_wu = {it: _dot(pre[it]["kd_st"], wu16[it]) for it in items}
    qp16 = {it: (pre[it]["qeg"] - a_wu[it][:, :dk]).astype(BF16) for it in items}

    state = [state_ref[h] for h in range(n_heads)]
    for p in range(n_sub):
        r0 = p * sub
        gtot = gates[p][2]
        outs = [[] for _ in range(n_heads)]
        for c in range(per_sub):
            c0 = c * CHUNK
            for h in range(n_heads):
                kw = k_wu[p, h]
                m16 = kw[c * dk:(c + 1) * dk, :dk].astype(BF16)
                res = _dot(jnp.concatenate([m16, qp16[p, h][c0:c0 + CHUNK]], axis=0), state[h].astype(BF16))
                outs[h].append(res[dk:] + a_wu[p, h][c0:c0 + CHUNK, dk:])
                state[h] = (state[h] * jnp.exp(gtot[c0:c0 + 1, h:h + 1]) - res[:dk]
                            + kw[c * dk:(c + 1) * dk, dk:])
        for h in range(n_heads):
            o = jnp.concatenate(outs[h], axis=0)
            zh = z_ref[r0:r0 + sub, h * dk:(h + 1) * dk]
            o_ref[r0:r0 + sub, h * dk:(h + 1) * dk] = (
                _rms(o, anw_ref[...]) * (zh * jax.nn.sigmoid(zh))).astype(o_ref.dtype)
    for h in range(n_heads):
        state_ref[h] = state[h]

    xcat_ref[0:halo, :] = qkv_ref[tt - halo:tt, :]


def _gdn(qkv_a, z, ab, conv_w, a_log, dt_bias, a_norm_w, bsz, seq):
    n, width = qkv_a.shape
    a_width = width // 3
    n_heads = a_width // A_HEAD_DIM
    tt = 256
    nt = seq // tt
    alog = jnp.zeros((1, LANES), F32).at[0, :n_heads].set(a_log)
    dtb = jnp.zeros((1, LANES), F32).at[0, :n_heads].set(dt_bias)
    kern = functools.partial(_gdn_kernel, n_heads=n_heads)
    return pl.pallas_call(
        kern,
        grid=(bsz, nt),
        in_specs=[
            pl.BlockSpec((tt, width), lambda b, i: (b * nt + i, 0)),
            pl.BlockSpec((tt, a_width), lambda b, i: (b * nt + i, 0)),
            pl.BlockSpec((tt, LANES), lambda b, i: (b * nt + i, 0)),
            pl.BlockSpec((CONV_K, width), lambda b, i: (0, 0)),
            pl.BlockSpec((1, LANES), lambda b, i: (0, 0)),
            pl.BlockSpec((1, LANES), lambda b, i: (0, 0)),
            pl.BlockSpec((1, A_HEAD_DIM), lambda b, i: (0, 0)),
        ],
        out_specs=pl.BlockSpec((tt, a_width), lambda b, i: (b * nt + i, 0)),
        out_shape=jax.ShapeDtypeStruct((n, a_width), BF16),
        scratch_shapes=[
            pltpu.VMEM((tt + 8, width), F32),
            pltpu.VMEM((n_heads, A_HEAD_DIM, A_HEAD_DIM), F32),
        ],
        compiler_params=pltpu.CompilerParams(
            dimension_semantics=("arbitrary", "arbitrary"), vmem_limit_bytes=48 * MIB),
        name="gdn",
    )(qkv_a, z, ab, conv_w, alog, dtb, a_norm_w.reshape(1, A_HEAD_DIM))


def _attn_kernel(q_ref, k0_ref, k1_ref, k2_ref, v0_ref, v1_ref, v2_ref, bias_ref, o_ref, *, n_heads):
    i = pl.program_id(1)
    tq = q_ref.shape[0]
    hd = B_HEAD_DIM
    per = LANES // hd
    lane = lax.broadcasted_iota(jnp.int32, (1, LANES), 1)
    k_refs = (k0_ref, k1_ref, k2_ref)
    v_refs = (v0_ref, v1_ref, v2_ref)
    n_kb = len(k_refs)
    for g in range(n_heads // per):
        cs = slice(g * LANES, (g + 1) * LANES)
        qg = q_ref[:, cs]
        ks = [r[:, cs] for r in k_refs]
        vs = [r[:, cs] for r in v_refs]
        out = jnp.zeros((tq, LANES), F32)
        for hh in range(per):
            h = g * per + hh
            in_head = (lane >= hh * hd) & (lane < (hh + 1) * hd)
            qh = jnp.where(in_head, qg, jnp.zeros_like(qg)) * (hd ** -0.5)
            s = []
            for j in range(n_kb):
                sj = _dot_nt(qh, ks[j]) + bias_ref[h, :, j * tq:(j + 1) * tq]
                if j < n_kb - 1:
                    sj = jnp.where(i >= n_kb - 1 - j, sj, NEG_BIG)
                s.append(sj)
            m = s[0].max(axis=-1, keepdims=True)
            for sj in s[1:]:
                m = jnp.maximum(m, sj.max(axis=-1, keepdims=True))
            p = [jnp.exp(sj - m) for sj in s]
            den = p[0].sum(axis=-1, keepdims=True)
            for pj in p[1:]:
                den = den + pj.sum(axis=-1, keepdims=True)
            acc = _dot(p[0].astype(BF16), vs[0])
            for j in range(1, n_kb):
                acc = acc + _dot(p[j].astype(BF16), vs[j])
            out = jnp.where(in_head, acc / den, out)
        o_ref[:, cs] = out.astype(o_ref.dtype)


def _band_bias(rel_bias, tq, n_kb):
    n_h = rel_bias.shape[0]
    back = (n_kb - 1) * tq
    nk = n_kb * tq
    span = tq + nk - 1
    lo = (nk - 1 - back) - REL_CLIP
    hi = span - lo - (2 * REL_CLIP + 1)
    by_offset = jnp.concatenate([jnp.broadcast_to(rel_bias[:, :1], (n_h, lo)), rel_bias,
                                 jnp.broadcast_to(rel_bias[:, -1:], (n_h, hi))], axis=1).astype(F32)
    rev = jnp.pad(by_offset[:, ::-1], ((0, 0), (0, 1)))
    skew = jnp.broadcast_to(rev[:, None, :], (n_h, tq, span + 1)).reshape(n_h, tq * (span + 1))
    skew = skew[:, :tq * span].reshape(n_h, tq, span)
    table = skew[:, :, tq - 1:tq - 1 + nk]
    qc = jnp.arange(tq)[:, None] // CHUNK
    kc = jnp.arange(nk)[None, :] // CHUNK - back // CHUNK
    allowed = (kc <= qc) & (kc >= qc - B_PREV_CHUNKS)
    return jnp.where(allowed[None], table, NEG_BIG)


def _band_attn(qkv_b, rel_bias, bsz, seq):
    n, width = qkv_b.shape
    b_width = width // 3
    n_heads = b_width // B_HEAD_DIM
    tq = 256
    n_kb = 3
    assert (n_kb - 1) * tq == B_PREV_CHUNKS * CHUNK
    nt = seq // tq
    bias = _band_bias(rel_bias, tq, n_kb)
    kern = functools.partial(_attn_kernel, n_heads=n_heads)

    def kv_spec(colblk, back):
        return pl.BlockSpec((tq, b_width), lambda b, i: (b * nt + jnp.maximum(i - back, 0), colblk))

    return pl.pallas_call(
        kern,
        grid=(bsz, nt),
        in_specs=[
            pl.BlockSpec((tq, b_width), lambda b, i: (b * nt + i, 0)),
            kv_spec(1, 2), kv_spec(1, 1), kv_spec(1, 0),
            kv_spec(2, 2), kv_spec(2, 1), kv_spec(2, 0),
            pl.BlockSpec(bias.shape, lambda b, i: (0, 0, 0)),
        ],
        out_specs=pl.BlockSpec((tq, b_width), lambda b, i: (b * nt + i, 0)),
        out_shape=jax.ShapeDtypeStruct((n, b_width), BF16),
        compiler_params=pltpu.CompilerParams(
            dimension_semantics=("parallel", "parallel"), vmem_limit_bytes=48 * MIB),
        name="band_attn",
    )(qkv_b, qkv_b, qkv_b, qkv_b, qkv_b, qkv_b, qkv_b, bias)


def _outproj_kernel(oa_ref, ob_ref, x_ref, mod_ref, nw_ref, woa_ref, wob_ref, wr_ref, br_ref,
                    x1_ref, h2_ref, route_ref, gate_ref, cnt_ref, carry_ref, *, n_experts):
    i = pl.program_id(0)
    tm = x_ref.shape[0]

    @pl.when(i == 0)
    def _():
        carry_ref[...] = jnp.zeros_like(carry_ref)

    y = _dot(oa_ref[...], woa_ref[...]) + _dot(ob_ref[...], wob_ref[...])
    ga1 = mod_ref[0, 2:3, :]
    sh2 = mod_ref[0, 3:4, :]
    sc2 = mod_ref[0, 4:5, :]
    x1 = x_ref[...] + ga1 * _rms(y, nw_ref[1:2, :])
    x1_ref[...] = x1
    h2 = _rms(x1, nw_ref[2:3, :]) * (1.0 + sc2) + sh2
    h2_ref[...] = h2

    h_hi = h2.astype(BF16)
    h_lo = (h2 - h_hi.astype(F32)).astype(BF16)
    logits = _dot(h_hi, wr_ref[0]) + (_dot(h_hi, wr_ref[1]) + _dot(h_lo, wr_ref[0])) + br_ref[...]
    lane_i = lax.broadcasted_iota(jnp.int32, (tm, LANES), 1)
    lane = lane_i.astype(F32)
    lg = jnp.where(lane_i < n_experts, logits, -jnp.inf)
    vals, idxs = [], []
    for _ in range(TOP_K):
        m = lg.max(axis=-1, keepdims=True)
        idx = jnp.where(lg == m, lane, float(LANES)).min(axis=-1, keepdims=True)
        vals.append(m)
        idxs.append(idx)
        lg = jnp.where(lane == idx, -jnp.inf, lg)
    ex = [jnp.exp(v - vals[0]) for v in vals]
    den = ex[0]
    for e in ex[1:]:
        den = den + e

    onehot = jnp.zeros((tm, LANES), F32)
    for idx in idxs:
        onehot = onehot + (lane == idx).astype(F32)
    row = lax.broadcasted_iota(jnp.int32, (tm, tm), 0)
    col = lax.broadcasted_iota(jnp.int32, (tm, tm), 1)
    before = (col < row).astype(BF16)
    cum = _dot(before, onehot.astype(BF16)) + carry_ref[0:1, :]
    carry = carry_ref[0:1, :] + onehot.sum(axis=0, keepdims=True)
    carry_ref[...] = jnp.broadcast_to(carry, carry_ref.shape)
    cnt_ref[...] = jnp.broadcast_to(carry, cnt_ref.shape)

    route = jnp.zeros((tm, LANES), F32)
    gate = jnp.zeros((tm, LANES), F32)
    for k in range(TOP_K):
        rank = jnp.where(lane == idxs[k], cum, 0.0).sum(axis=-1, keepdims=True)
        route = jnp.where(lane_i == k, idxs[k], route)
        route = jnp.where(lane_i == TOP_K + k, rank, route)
        gate = jnp.where(lane_i == k, ex[k] / den, gate)
    route_ref[...] = route.astype(jnp.int32)
    gate_ref[...] = gate


def _outproj(o_a, o_b, x2, mod3, norm_w, w_out_a, w_out_b, w_router, b_router, seq):
    n, d = x2.shape
    n_experts = w_router.shape[1]
    tm = 512
    tiles_per_seq = seq // tm
    wr32 = jnp.zeros((d, LANES), F32).at[:, :n_experts].set(w_router)
    wr_hi = wr32.astype(BF16)
    wr = jnp.stack([wr_hi, (wr32 - wr_hi.astype(F32)).astype(BF16)])
    br = jnp.zeros((1, LANES), F32).at[0, :n_experts].set(b_router)
    kern = functools.partial(_outproj_kernel, n_experts=n_experts)
    aw = o_a.shape[1]
    bw = o_b.shape[1]
    return pl.pallas_call(
        kern,
        grid=(n // tm,),
        in_specs=[
            pl.BlockSpec((tm, aw), lambda i: (i, 0)),
            pl.BlockSpec((tm, bw), lambda i: (i, 0)),
            pl.BlockSpec((tm, d), lambda i: (i, 0)),
            pl.BlockSpec((1, 6, d), lambda i: (i // tiles_per_seq, 0, 0)),
            pl.BlockSpec((4, d), lambda i: (0, 0)),
            pl.BlockSpec((aw, d), lambda i: (0, 0)),
            pl.BlockSpec((bw, d), lambda i: (0, 0)),
            pl.BlockSpec((2, d, LANES), lambda i: (0, 0, 0)),
            pl.BlockSpec((1, LANES), lambda i: (0, 0)),
        ],
        out_specs=[
            pl.BlockSpec((tm, d), lambda i: (i, 0)),
            pl.BlockSpec((tm, d), lambda i: (i, 0)),
            pl.BlockSpec((tm, LANES), lambda i: (i, 0)),
            pl.BlockSpec((tm, LANES), lambda i: (i, 0)),
            pl.BlockSpec((8, LANES), lambda i: (0, 0)),
        ],
        out_shape=[
            jax.ShapeDtypeStruct((n, d), F32),
            jax.ShapeDtypeStruct((n, d), F32),
            jax.ShapeDtypeStruct((n, LANES), jnp.int32),
            jax.ShapeDtypeStruct((n, LANES), F32),
            jax.ShapeDtypeStruct((8, LANES), F32),
        ],
        scratch_shapes=[pltpu.VMEM((8, LANES), F32)],
        compiler_params=pltpu.CompilerParams(
            dimension_semantics=("arbitrary",), vmem_limit_bytes=48 * MIB),
        name="outproj_router",
    )(o_a, o_b, x2, mod3, norm_w, w_out_a, w_out_b, wr, br)


def _dispatch_kernel(pend_ref, vend_ref, nu_ref, dest_ref, h_ref, xs_hbm, stage, zeros, sem, zsem, tsem,
                     *, n_experts, blk):
    i = pl.program_id(0)
    n_tiles = pl.num_programs(0)
    tm, d = h_ref.shape
    slot = lax.rem(i, 2)
    piece = zeros.shape[0]
    shift = int(math.log2(piece))
    n_pieces = xs_hbm.shape[0] // piece

    def wait_rows(s):
        for _ in range(TOP_K):
            pltpu.make_async_copy(stage.at[s], xs_hbm.at[pl.ds(0, tm), :], sem.at[s]).wait()

    def clear(sem_ref):
        def body(p, carry):
            start = pl.multiple_of(p * piece, piece)
            pltpu.make_async_copy(zeros, xs_hbm.at[pl.ds(start, piece), :], sem_ref).start()
            return carry
        return body

    def wait_clear(sem_ref):
        def body(p, carry):
            pltpu.make_async_copy(zeros, xs_hbm.at[pl.ds(0, piece), :], sem_ref).wait()
            return carry
        return body

    @pl.when(i == 0)
    def _():
        zeros[...] = jnp.zeros_like(zeros)
        for e in range(n_experts):
            lax.fori_loop(vend_ref[e] >> shift, pend_ref[e] >> shift, clear(zsem), 0)
        lax.fori_loop((nu_ref[0] * blk) >> shift, n_pieces, clear(tsem), 0)
        for e in range(n_experts):
            lax.fori_loop(vend_ref[e] >> shift, pend_ref[e] >> shift, wait_clear(zsem), 0)

    @pl.when(i >= 2)
    def _():
        wait_rows(slot)

    stage[slot] = h_ref[...]

    for t in range(tm):
        for k in range(TOP_K):
            s = dest_ref[0, 0, t * TOP_K + k]
            pltpu.make_async_copy(stage.at[slot, pl.ds(t, 1), :], xs_hbm.at[pl.ds(s, 1), :],
                                  sem.at[slot]).start(priority=k % 2)

    @pl.when(i == n_tiles - 1)
    def _():
        wait_rows(slot)

        @pl.when(n_tiles > 1)
        def _():
            wait_rows(1 - slot)

        lax.fori_loop((nu_ref[0] * blk) >> shift, n_pieces, wait_clear(tsem), 0)


def _dispatch(h2, dest, pend, vend, n_used, cap):
    n, d = h2.shape
    n_experts = pend.shape[0]
    tm = 256
    n_tiles = n // tm
    kern = functools.partial(_dispatch_kernel, n_experts=n_experts, blk=MOE_BLOCK)
    grid_spec = pltpu.PrefetchScalarGridSpec(
        num_scalar_prefetch=3,
        grid=(n_tiles,),
        in_specs=[
            pl.BlockSpec((1, 1, tm * TOP_K), lambda i, pe, pa, nu: (i, 0, 0), memory_space=pltpu.SMEM),
            pl.BlockSpec((tm, d), lambda i, pe, pa, nu: (i, 0)),
        ],
        out_specs=pl.BlockSpec(memory_space=pl.ANY),
        scratch_shapes=[
            pltpu.VMEM((2, tm, d), F32),
            pltpu.VMEM((CLEAR_ROWS, d), F32),
            pltpu.SemaphoreType.DMA((2,)),
            pltpu.SemaphoreType.DMA(()),
            pltpu.SemaphoreType.DMA(()),
        ],
    )
    return pl.pallas_call(
        kern,
        grid_spec=grid_spec,
        out_shape=jax.ShapeDtypeStruct((cap, d), F32),
        compiler_params=pltpu.CompilerParams(
            dimension_semantics=("arbitrary",), vmem_limit_bytes=32 * MIB),
        name="dispatch",
    )(pend, vend, n_used, dest.reshape(n_tiles, 1, tm * TOP_K), h2)


def _expert_kernel(be_ref, nu_ref, nv_ref, nx_ref, xs_ref, wup_hbm, bup_ref, wdn_hbm, bdn_ref, y_ref,
                   up_stage, dn_stage, wsem, wup16, wdn16):
    j = pl.program_id(0)
    blk, d = xs_ref.shape
    f = wdn16.shape[0]
    n_valid = nv_ref[j]
    half = blk // 2

    def weight_copies(e):
        return (pltpu.make_async_copy(wup_hbm.at[e], up_stage, wsem.at[0]),
                pltpu.make_async_copy(wdn_hbm.at[e], dn_stage, wsem.at[1]))

    prev = be_ref[jnp.maximum(j - 1, 0)]
    first_of_expert = jnp.logical_and(n_valid > 0, jnp.logical_or(j == 0, be_ref[j] != prev))

    @pl.when(jnp.logical_and(j == 0, n_valid > 0))
    def _():
        for c in weight_copies(be_ref[0]):
            c.start()

    @pl.when(first_of_expert)
    def _():
        for c in weight_copies(be_ref[j]):
            c.wait()
        step = 128
        for r0 in range(0, d, step):
            wup16[r0:r0 + step, :] = up_stage[r0:r0 + step, :].astype(BF16)
        for r0 in range(0, f, step):
            wdn16[r0:r0 + step, :] = dn_stage[r0:r0 + step, :].astype(BF16)

        @pl.when(nx_ref[j] >= 0)
        def _():
            for c in weight_copies(nx_ref[j]):
                c.start()

    def mlp(rows):
        xb = xs_ref[0:rows, :].astype(BF16)
        fc = min(f, 512)
        acc = None
        for c0 in range(0, f, fc):
            glu = _dot(xb, wup16[:, c0:c0 + fc]) + bup_ref[0, :, c0:c0 + fc]
            lin = _dot(xb, wup16[:, f + c0:f + c0 + fc]) + bup_ref[0, :, f + c0:f + c0 + fc]
            glu = jnp.minimum(glu, SWIGLU_LIMIT)
            lin = jnp.clip(lin, -SWIGLU_LIMIT, SWIGLU_LIMIT)
            act = glu * jax.nn.sigmoid(SWIGLU_ALPHA * glu) * (lin + 1.0)
            part = _dot(act.astype(BF16), wdn16[c0:c0 + fc, :])
            acc = part if acc is None else acc + part
        y_ref[0:rows, :] = acc + bdn_ref[0]

    @pl.when(n_valid > half)
    def _():
        mlp(blk)

    @pl.when(jnp.logical_and(n_valid > 0, n_valid <= half))
    def _():
        mlp(half)
        y_ref[half:, :] = jnp.zeros((blk - half, d), y_ref.dtype)

    @pl.when(n_valid == 0)
    def _():
        y_ref[...] = jnp.zeros_like(y_ref)


def _experts(xs, block_e, n_used, n_valid, next_e, w_up, b_up, w_down, b_down):
    cap, d = xs.shape
    n_exp, _, f2 = w_up.shape
    f = w_down.shape[1]
    blk = MOE_BLOCK
    n_blocks = cap // blk
    grid_spec = pltpu.PrefetchScalarGridSpec(
        num_scalar_prefetch=4,
        grid=(n_blocks,),
        in_specs=[
            pl.BlockSpec((blk, d), lambda j, be, nu, nv, nx: (jnp.minimum(j, jnp.maximum(nu[0] - 1, 0)), 0)),
            pl.BlockSpec(memory_space=pl.ANY),
            pl.BlockSpec((1, 1, f2), lambda j, be, nu, nv, nx: (be[j], 0, 0)),
            pl.BlockSpec(memory_space=pl.ANY),
            pl.BlockSpec((1, 1, d), lambda j, be, nu, nv, nx: (be[j], 0, 0)),
        ],
        out_specs=pl.BlockSpec((blk, d), lambda j, be, nu, nv, nx: (j, 0)),
        scratch_shapes=[
            pltpu.VMEM((d, f2), F32),
            pltpu.VMEM((f, d), F32),
            pltpu.SemaphoreType.DMA((2,)),
            pltpu.VMEM((d, f2), BF16),
            pltpu.VMEM((f, d), BF16),
        ],
    )
    return pl.pallas_call(
        _expert_kernel,
        grid_spec=grid_spec,
        out_shape=jax.ShapeDtypeStruct((cap, d), F32),
        compiler_params=pltpu.CompilerParams(
            dimension_semantics=("arbitrary",), vmem_limit_bytes=56 * MIB),
        name="experts",
    )(block_e, n_used, n_valid, next_e, xs, w_up, b_up.reshape(n_exp, 1, f2), w_down, b_down.reshape(n_exp, 1, d))


def _combine_kernel(dest_ref, dest_next_ref, y_hbm, gate_ref, x1_ref, mod_ref, nw_ref, o_ref, buf, sem):
    i = pl.program_id(0)
    n_tiles = pl.num_programs(0)
    tm, d = x1_ref.shape
    slot = lax.rem(i, 2)

    def start_gather(idx_ref, dst_slot):
        for t in range(tm):
            for k in range(TOP_K):
                s = idx_ref[0, 0, t * TOP_K + k]
                pltpu.make_async_copy(y_hbm.at[pl.ds(s, 1), :], buf.at[dst_slot, k, pl.ds(t, 1), :],
                                      sem.at[dst_slot]).start(priority=k % 2)

    @pl.when(i == 0)
    def _():
        start_gather(dest_ref, 0)

    @pl.when(i + 1 < n_tiles)
    def _():
        start_gather(dest_next_ref, 1 - slot)

    for k in range(TOP_K):
        pltpu.make_async_copy(y_hbm.at[pl.ds(0, tm), :], buf.at[slot, k], sem.at[slot]).wait()

    gate = gate_ref[...]
    ysum = gate[:, 0:1] * buf[slot, 0]
    for k in range(1, TOP_K):
        ysum = ysum + gate[:, k:k + 1] * buf[slot, k]
    ga2 = mod_ref[0, 5:6, :]
    o_ref[...] = x1_ref[...] + ga2 * _rms(ysum, nw_ref[3:4, :])


def _combine(y_sorted, dest, gates, x1, mod3, norm_w, seq):
    n, d = x1.shape
    tm = 128
    n_tiles = n // tm
    tiles_per_seq = seq // tm
    dest3 = dest.reshape(n_tiles, 1, tm * TOP_K)
    return pl.pallas_call(
        _combine_kernel,
        grid=(n_tiles,),
        in_specs=[
            pl.BlockSpec((1, 1, tm * TOP_K), lambda i: (i, 0, 0), memory_space=pltpu.SMEM),
            pl.BlockSpec((1, 1, tm * TOP_K), lambda i: (jnp.minimum(i + 1, n_tiles - 1), 0, 0),
                         memory_space=pltpu.SMEM),
            pl.BlockSpec(memory_space=pl.ANY),
            pl.BlockSpec((tm, LANES), lambda i: (i, 0)),
            pl.BlockSpec((tm, d), lambda i: (i, 0)),
            pl.BlockSpec((1, 6, d), lambda i: (i // tiles_per_seq, 0, 0)),
            pl.BlockSpec((4, d), lambda i: (0, 0)),
        ],
        out_specs=pl.BlockSpec((tm, d), lambda i: (i, 0)),
        out_shape=jax.ShapeDtypeStruct((n, d), F32),
        scratch_shapes=[
            pltpu.VMEM((2, TOP_K, tm, d), F32),
            pltpu.SemaphoreType.DMA((2,)),
        ],
        compiler_params=pltpu.CompilerParams(
            dimension_semantics=("arbitrary",), vmem_limit_bytes=32 * MIB),
        name="combine",
    )(dest3, dest3, y_sorted, gates, x1, mod3, norm_w)


def _layer(x, c, w_ada, b_ada, norm_w, w_in, conv_w, a_log, dt_bias, a_norm_w, rel_bias, w_out,
           w_router, b_router, w_up, b_up, w_down, b_down):
    bsz, seq, d = x.shape
    n = bsz * seq
    a_width = conv_w.shape[1] // 3
    a_heads = a_log.shape[0]
    b_width = w_out.shape[0] - a_width
    n_experts = w_router.shape[1]
    off_gate = 4 * a_width
    off_b = off_gate + 2 * a_heads

    mod3 = _adaln(c, w_ada, b_ada).reshape(bsz, 6, d)
    x2 = x.reshape(n, d)

    w_main = jnp.concatenate([w_in[:, :off_gate], w_in[:, off_b:]], axis=1).astype(BF16)
    w_gate = jnp.zeros((d, LANES), F32).at[:, :2 * a_heads].set(w_in[:, off_gate:off_b]).astype(BF16)
    qkv_a, z_a, ab, qkv_b = _inproj(x2, mod3, norm_w, w_main, w_gate, seq, 3 * a_width, a_width, 3 * b_width)

    o_a = _gdn(qkv_a, z_a, ab, conv_w, a_log, dt_bias, a_norm_w, bsz, seq)
    o_b = _band_attn(qkv_b, rel_bias, bsz, seq)

    w_out16 = w_out.astype(BF16)
    x1, h2, route, gates, cnt = _outproj(o_a, o_b, x2, mod3, norm_w, w_out16[:a_width], w_out16[a_width:],
                                         w_router, b_router, seq)

    top_idx = route[:, :TOP_K]
    rank = route[:, TOP_K:2 * TOP_K]
    counts = cnt[0, :n_experts].astype(jnp.int32)
    padded = (counts + MOE_BLOCK - 1) // MOE_BLOCK * MOE_BLOCK
    pend = jnp.cumsum(padded)
    pstart = pend - padded
    expert_ids = jnp.arange(n_experts, dtype=jnp.int32)
    dest = jnp.sum(jnp.where(top_idx[..., None] == expert_ids, pstart, 0), axis=-1) + rank
    n_assign = n * TOP_K
    n_blocks = -(-n_assign // MOE_BLOCK) + n_experts
    cap = n_blocks * MOE_BLOCK
    block_start = jnp.arange(n_blocks, dtype=jnp.int32) * MOE_BLOCK
    block_e = jnp.minimum(jnp.sum(pend[None, :] <= block_start[:, None], axis=1), n_experts - 1).astype(jnp.int32)
    n_used = (pend[-1:] // MOE_BLOCK).astype(jnp.int32)
    dest = dest.astype(jnp.int32)

    vend = (pstart + counts).astype(jnp.int32)
    block_vend = jnp.sum(jnp.where(block_e[:, None] == expert_ids, vend, 0), axis=1)
    n_valid = jnp.clip(block_vend - block_start, 0, MOE_BLOCK).astype(jnp.int32)

    xs = _dispatch(h2, dest, pend.astype(jnp.int32), vend, n_used, cap)
    blocks = jnp.arange(n_blocks, dtype=jnp.int32)
    later = (blocks[None, :] > blocks[:, None]) & (block_e[None, :] != block_e[:, None]) & (n_valid[None, :] > 0)
    first_later = jnp.min(jnp.where(later, blocks[None, :], n_blocks), axis=1)
    next_e = jnp.sum(jnp.where(blocks[None, :] == first_later[:, None], block_e[None, :] + 1, 0), axis=1) - 1
    y_sorted = _experts(xs, block_e, n_used, n_valid, next_e.astype(jnp.int32), w_up, b_up, w_down, b_down)
    out = _combine(y_sorted, dest, gates, x1, mod3, norm_w, seq)
    return out.reshape(bsz, seq, d)


def kernel(x, c, w_ada, b_ada, norm_w, w_in, conv_w, a_log, dt_bias, a_norm_w, rel_bias, w_out,
           w_router, b_router, w_up, b_up, w_down, b_down):
    for l in range(w_ada.shape[0]):
        x = _layer(x, c, w_ada[l], b_ada[l], norm_w[l], w_in[l], conv_w[l], a_log[l], dt_bias[l],
                   a_norm_w[l], rel_bias[l], w_out[l], w_router[l], b_router[l], w_up[l], b_up[l],
                   w_down[l], b_down[l])
    return x
```

```python
import functools
import math

import jax
import jax.numpy as jnp
from jax import lax
from jax.experimental import pallas as pl
from jax.experimental.pallas import tpu as pltpu

F32 = jnp.float32
BF16 = jnp.bfloat16
HIGHEST = lax.Precision.HIGHEST

EPS = 1e-6
CHUNK = 64
CONV_K = 4
A_HEAD_DIM = 128
B_HEAD_DIM = 64
B_PREV_CHUNKS = 8
REL_CLIP = 128
TOP_K = 4
SWIGLU_ALPHA = 1.702
SWIGLU_LIMIT = 7.0
MOE_BLOCK = 512
CLEAR_ROWS = 64
LANES = 128
NEG_BIG = -1e30

MIB = 1024 * 1024


def _dot(a, b):
    return jnp.dot(a, b, preferred_element_type=F32)


def _dot_nt(a, b):
    return lax.dot_general(a, b, (((1,), (1,)), ((), ())), preferred_element_type=F32)


def _dot_exact(a, b):
    return jnp.dot(a, b, precision=HIGHEST, preferred_element_type=F32)


def _rms(x, w):
    return x * lax.rsqrt(jnp.mean(x * x, axis=-1, keepdims=True) + EPS) * w


def _adaln_kernel(c_ref, w_ref, b_ref, o_ref):
    cs = c_ref[...]
    cs = cs * jax.nn.sigmoid(cs)
    o_ref[...] = _dot_exact(cs, w_ref[...]) + b_ref[...]


def _adaln(c, w, b):
    bsz, d = c.shape
    n_out = w.shape[1]
    rows = 8
    cp = jnp.zeros((rows, d), F32).at[:bsz].set(c)
    tn = 512
    out = pl.pallas_call(
        _adaln_kernel,
        grid=(n_out // tn,),
        in_specs=[
            pl.BlockSpec((rows, d), lambda j: (0, 0)),
            pl.BlockSpec((d, tn), lambda j: (0, j)),
            pl.BlockSpec((1, tn), lambda j: (0, j)),
        ],
        out_specs=pl.BlockSpec((rows, tn), lambda j: (0, j)),
        out_shape=jax.ShapeDtypeStruct((rows, n_out), F32),
        name="adaln",
    )(cp, w, b.reshape(1, n_out))
    return out[:bsz]


def _inproj_kernel(x_ref, mod_ref, nw_ref, wm_ref, wg_ref, cw_ref, qkva_ref, z_ref, ab_ref, qkvb_ref, raw_ref,
                   *, a_qkv, a_z, tiles_per_seq):
    i = pl.program_id(0)
    tm = x_ref.shape[0]
    dk = A_HEAD_DIM
    a_width = a_qkv // 3
    halo = 8
    tn = 512

    @pl.when(lax.rem(i, tiles_per_seq) == 0)
    def _():
        raw_ref[0:halo, :] = jnp.zeros((halo, a_qkv), F32)

    x = x_ref[...]
    sh = mod_ref[0, 0:1, :]
    sc = mod_ref[0, 1:2, :]
    h = _rms(x, nw_ref[0:1, :]) * (1.0 + sc) + sh
    hb = h.astype(BF16)

    def conv_silu(c0):
        acc = cw_ref[CONV_K - 1:CONV_K, c0:c0 + dk] * raw_ref[halo:halo + tm, c0:c0 + dk]
        for j in range(CONV_K - 1):
            start = halo - (CONV_K - 1) + j
            acc = acc + cw_ref[j:j + 1, c0:c0 + dk] * raw_ref[start:start + tm, c0:c0 + dk]
        y = acc * jax.nn.sigmoid(acc)
        if c0 < 2 * a_width:
            y = y * lax.rsqrt(jnp.sum(y * y, axis=-1, keepdims=True) + EPS)
        if c0 < a_width:
            y = y * (dk ** -0.5)
        qkva_ref[:, c0:c0 + dk] = y

    for c0 in range(0, a_qkv, tn):
        raw_ref[halo:halo + tm, c0:c0 + tn] = _dot(hb, wm_ref[:, c0:c0 + tn])
    slabs = list(range(0, a_qkv, dk))
    other = [("z", c0) for c0 in range(0, a_z, tn)] + [("b", c0) for c0 in range(0, qkvb_ref.shape[1], tn)]
    per = -(-len(slabs) // len(other))
    off = a_qkv + a_z
    for n_o, (kind, c0) in enumerate(other):
        for c in slabs[n_o * per:(n_o + 1) * per]:
            conv_silu(c)
        if kind == "z":
            z_ref[:, c0:c0 + tn] = _dot(hb, wm_ref[:, a_qkv + c0:a_qkv + c0 + tn])
        else:
            qkvb_ref[:, c0:c0 + tn] = _dot(hb, wm_ref[:, off + c0:off + c0 + tn]).astype(BF16)
    ab_ref[...] = _dot(hb, wg_ref[...])
    raw_ref[0:halo, :] = raw_ref[tm:tm + halo, :]


def _inproj(x2, mod3, norm_w, w_main, w_gate, conv_w, seq, a_qkv, a_z, b_qkv):
    n, d = x2.shape
    tm = 512
    tiles_per_seq = seq // tm
    kern = functools.partial(_inproj_kernel, a_qkv=a_qkv, a_z=a_z, tiles_per_seq=tiles_per_seq)
    return pl.pallas_call(
        kern,
        grid=(n // tm,),
        in_specs=[
            pl.BlockSpec((tm, d), lambda i: (i, 0)),
            pl.BlockSpec((1, 6, d), lambda i: (i // tiles_per_seq, 0, 0)),
            pl.BlockSpec((4, d), lambda i: (0, 0)),
            pl.BlockSpec(w_main.shape, lambda i: (0, 0)),
            pl.BlockSpec(w_gate.shape, lambda i: (0, 0)),
            pl.BlockSpec(conv_w.shape, lambda i: (0, 0)),
        ],
        out_specs=[
            pl.BlockSpec((tm, a_qkv), lambda i: (i, 0)),
            pl.BlockSpec((tm, a_z), lambda i: (i, 0)),
            pl.BlockSpec((tm, LANES), lambda i: (i, 0)),
            pl.BlockSpec((tm, b_qkv), lambda i: (i, 0)),
        ],
        out_shape=[
            jax.ShapeDtypeStruct((n, a_qkv), F32),
            jax.ShapeDtypeStruct((n, a_z), F32),
            jax.ShapeDtypeStruct((n, LANES), F32),
            jax.ShapeDtypeStruct((n, b_qkv), BF16),
        ],
        scratch_shapes=[pltpu.VMEM((tm + 8, a_qkv), F32)],
        compiler_params=pltpu.CompilerParams(
            dimension_semantics=("arbitrary",), vmem_limit_bytes=48 * MIB),
        name="inproj",
    )(x2, mod3, norm_w, w_main, w_gate, conv_w)


def _gdn_kernel(qkv_ref, z_ref, ab_ref, alog_ref, dtb_ref, anw_ref, o_ref, state_ref, *, n_heads):
    i = pl.program_id(1)
    tt = qkv_ref.shape[0]
    width = qkv_ref.shape[1]
    a_width = width // 3
    dk = A_HEAD_DIM
    sub = 2 * CHUNK
    per_sub = sub // CHUNK

    @pl.when(i == 0)
    def _():
        state_ref[...] = jnp.zeros_like(state_ref)

    row = lax.broadcasted_iota(jnp.int32, (sub, sub), 0)
    col = lax.broadcasted_iota(jnp.int32, (sub, sub), 1)
    shift = int(math.log2(CHUNK))
    same = (row >> shift) == (col >> shift)
    tri_incl = same & (col <= row)
    tri_strict = same & (col < row)
    eye = (row == col).astype(F32)
    pair = (row >> 1) == (col >> 1)
    couples = [((row >> (lv + 1)) == (col >> (lv + 1))) & ((row >> lv) != (col >> lv))
               for lv in range(1, shift)]
    col_chunk = lax.broadcasted_iota(jnp.int32, (dk, sub), 1) >> shift
    tri_incl_f = tri_incl.astype(F32)
    same_f = same.astype(F32)

    n_sub = tt // sub
    gates = []
    for p in range(n_sub):
        r0 = p * sub
        ab = ab_ref[r0:r0 + sub, :]
        gfull = -jnp.exp(alog_ref[...]) * jax.nn.softplus(ab + dtb_ref[...])
        bfull = jax.nn.sigmoid(ab)
        gcum = _dot_exact(tri_incl_f, gfull)
        gtot = _dot_exact(same_f, gfull)
        gates.append((gcum, gcum.T, gtot, bfull))

    items = [(p, h) for p in range(n_sub) for h in range(n_heads)]
    pre = {}
    for p, h in items:
        r0 = p * sub
        gcum, gcum_t, gtot, bfull = gates[p]
        gc = gcum[:, h:h + 1]
        beta = bfull[:, n_heads + h:n_heads + h + 1]
        q = qkv_ref[r0:r0 + sub, h * dk:(h + 1) * dk]
        k = qkv_ref[r0:r0 + sub, a_width + h * dk:a_width + (h + 1) * dk]
        v = qkv_ref[r0:r0 + sub, 2 * a_width + h * dk:2 * a_width + (h + 1) * dk]
        decay = jnp.exp(jnp.where(tri_incl, gc - gcum_t[h:h + 1, :], -jnp.inf))
        kb = k * beta
        k16 = k.astype(BF16)
        lower = jnp.where(tri_strict, _dot_nt(kb.astype(BF16), k16) * decay, 0.0)
        attn16 = (_dot_nt(q.astype(BF16), k16) * decay).astype(BF16)
        eg = jnp.exp(gc)
        rhs = jnp.concatenate([kb * eg, v * beta], axis=1).astype(BF16)
        kd_t = (k * jnp.exp(gtot[:, h:h + 1] - gc)).T
        kd_st = jnp.concatenate([jnp.where(col_chunk == c, kd_t, 0.0) for c in range(per_sub)],
                                axis=0).astype(BF16)
        pre[p, h] = dict(lower=lower, attn16=attn16, rhs=rhs, kd_st=kd_st, qeg=q * eg)

    tmat = {it: eye - jnp.where(pair, pre[it]["lower"], 0.0) for it in items}
    for couple in couples:
        t16 = {it: tmat[it].astype(BF16) for it in items}
        half = {it: _dot(t16[it], jnp.where(couple, pre[it]["lower"], 0.0).astype(BF16)) for it in items}
        tmat = {it: tmat[it] - _dot(half[it].astype(BF16), t16[it]) for it in items}

    wu16 = {it: _dot(tmat[it].astype(BF16), pre[it]["rhs"]).astype(BF16) for it in items}
    a_wu = {it: _dot(pre[it]["attn16"], wu16[it]) for it in items}
    k_wu = {it: _dot(pre[it]["kd_st"], wu16[it]) for it in items}
    qp16 = {it: (pre[it]["qeg"] - a_wu[it][:, :dk]).astype(BF16) for it in items}

    state = [state_ref[h] for h in range(n_heads)]
    for p in range(n_sub):
        r0 = p * sub
        gtot = gates[p][2]
        outs = [[] for _ in range(n_heads)]
        for c in range(per_sub):
            c0 = c * CHUNK
            for h in range(n_heads):
                kw = k_wu[p, h]
                m16 = kw[c * dk:(c + 1) * dk, :dk].astype(BF16)
                res = _dot(jnp.concatenate([m16, qp16[p, h][c0:c0 + CHUNK]], axis=0), state[h].astype(BF16))
                outs[h].append(res[dk:] + a_wu[p, h][c0:c0 + CHUNK, dk:])
                state[h] = (state[h] * jnp.exp(gtot[c0:c0 + 1, h:h + 1]) - res[:dk]
                            + kw[c * dk:(c + 1) * dk, dk:])
        for h in range(n_heads):
            o = jnp.concatenate(outs[h], axis=0)
            zh = z_ref[r0:r0 + sub, h * dk:(h + 1) * dk]
            o_ref[r0:r0 + sub, h * dk:(h + 1) * dk] = (
                _rms(o, anw_ref[...]) * (zh * jax.nn.sigmoid(zh))).astype(o_ref.dtype)
    for h in range(n_heads):
        state_ref[h] = state[h]


def _gdn(qkv_a, z, ab, a_log, dt_bias, a_norm_w, bsz, seq):
    n, width = qkv_a.shape
    a_width = width // 3
    n_heads = a_width // A_HEAD_DIM
    tt = 256
    nt = seq // tt
    alog = jnp.zeros((1, LANES), F32).at[0, :n_heads].set(a_log)
    dtb = jnp.zeros((1, LANES), F32).at[0, :n_heads].set(dt_bias)
    kern = functools.partial(_gdn_kernel, n_heads=n_heads)
    return pl.pallas_call(
        kern,
        grid=(bsz, nt),
        in_specs=[
            pl.BlockSpec((tt, width), lambda b, i: (b * nt + i, 0)),
            pl.BlockSpec((tt, a_width), lambda b, i: (b * nt + i, 0)),
            pl.BlockSpec((tt, LANES), lambda b, i: (b * nt + i, 0)),
            pl.BlockSpec((1, LANES), lambda b, i: (0, 0)),
            pl.BlockSpec((1, LANES), lambda b, i: (0, 0)),
            pl.BlockSpec((1, A_HEAD_DIM), lambda b, i: (0, 0)),
        ],
        out_specs=pl.BlockSpec((tt, a_width), lambda b, i: (b * nt + i, 0)),
        out_shape=jax.ShapeDtypeStruct((n, a_width), BF16),
        scratch_shapes=[pltpu.VMEM((n_heads, A_HEAD_DIM, A_HEAD_DIM), F32)],
        compiler_params=pltpu.CompilerParams(
            dimension_semantics=("arbitrary", "arbitrary"), vmem_limit_bytes=48 * MIB),
        name="gdn",
    )(qkv_a, z, ab, alog, dtb, a_norm_w.reshape(1, A_HEAD_DIM))


def _attn_kernel(q_ref, k0_ref, k1_ref, k2_ref, v0_ref, v1_ref, v2_ref, bias_ref, o_ref, *, n_heads):
    i = pl.program_id(1)
    tq = q_ref.shape[0]
    hd = B_HEAD_DIM
    per = LANES // hd
    lane = lax.broadcasted_iota(jnp.int32, (1, LANES), 1)
    k_refs = (k0_ref, k1_ref, k2_ref)
    v_refs = (v0_ref, v1_ref, v2_ref)
    n_kb = len(k_refs)
    for g in range(n_heads // per):
        cs = slice(g * LANES, (g + 1) * LANES)
        qg = q_ref[:, cs]
        ks = [r[:, cs] for r in k_refs]
        vs = [r[:, cs] for r in v_refs]
        out = jnp.zeros((tq, LANES), F32)
        for hh in range(per):
            h = g * per + hh
            in_head = (lane >= hh * hd) & (lane < (hh + 1) * hd)
            qh = jnp.where(in_head, qg, jnp.zeros_like(qg)) * (hd ** -0.5)
            s = []
            for j in range(n_kb):
                sj = _dot_nt(qh, ks[j]) + bias_ref[h, :, j * tq:(j + 1) * tq]
                if j < n_kb - 1:
                    sj = jnp.where(i >= n_kb - 1 - j, sj, NEG_BIG)
                s.append(sj)
            m = s[0].max(axis=-1, keepdims=True)
            for sj in s[1:]:
                m = jnp.maximum(m, sj.max(axis=-1, keepdims=True))
            p = [jnp.exp(sj - m) for sj in s]
            den = p[0].sum(axis=-1, keepdims=True)
            for pj in p[1:]:
                den = den + pj.sum(axis=-1, keepdims=True)
            acc = _dot(p[0].astype(BF16), vs[0])
            for j in range(1, n_kb):
                acc = acc + _dot(p[j].astype(BF16), vs[j])
            out = jnp.where(in_head, acc / den, out)
        o_ref[:, cs] = out.astype(o_ref.dtype)


def _band_bias(rel_bias, tq, n_kb):
    n_h = rel_bias.shape[0]
    back = (n_kb - 1) * tq
    nk = n_kb * tq
    span = tq + nk - 1
    lo = (nk - 1 - back) - REL_CLIP
    hi = span - lo - (2 * REL_CLIP + 1)
    by_offset = jnp.concatenate([jnp.broadcast_to(rel_bias[:, :1], (n_h, lo)), rel_bias,
                                 jnp.broadcast_to(rel_bias[:, -1:], (n_h, hi))], axis=1).astype(F32)
    rev = jnp.pad(by_offset[:, ::-1], ((0, 0), (0, 1)))
    skew = jnp.broadcast_to(rev[:, None, :], (n_h, tq, span + 1)).reshape(n_h, tq * (span + 1))
    skew = skew[:, :tq * span].reshape(n_h, tq, span)
    table = skew[:, :, tq - 1:tq - 1 + nk]
    qc = jnp.arange(tq)[:, None] // CHUNK
    kc = jnp.arange(nk)[None, :] // CHUNK - back // CHUNK
    allowed = (kc <= qc) & (kc >= qc - B_PREV_CHUNKS)
    return jnp.where(allowed[None], table, NEG_BIG)


def _band_attn(qkv_b, rel_bias, bsz, seq):
    n, width = qkv_b.shape
    b_width = width // 3
    n_heads = b_width // B_HEAD_DIM
    tq = 256
    n_kb = 3
    assert (n_kb - 1) * tq == B_PREV_CHUNKS * CHUNK
    nt = seq // tq
    bias = _band_bias(rel_bias, tq, n_kb)
    kern = functools.partial(_attn_kernel, n_heads=n_heads)

    def kv_spec(colblk, back):
        return pl.BlockSpec((tq, b_width), lambda b, i: (b * nt + jnp.maximum(i - back, 0), colblk))

    return pl.pallas_call(
        kern,
        grid=(bsz, nt),
        in_specs=[
            pl.BlockSpec((tq, b_width), lambda b, i: (b * nt + i, 0)),
            kv_spec(1, 2), kv_spec(1, 1), kv_spec(1, 0),
            kv_spec(2, 2), kv_spec(2, 1), kv_spec(2, 0),
            pl.BlockSpec(bias.shape, lambda b, i: (0, 0, 0)),
        ],
        out_specs=pl.BlockSpec((tq, b_width), lambda b, i: (b * nt + i, 0)),
        out_shape=jax.ShapeDtypeStruct((n, b_width), BF16),
        compiler_params=pltpu.CompilerParams(
            dimension_semantics=("parallel", "parallel"), vmem_limit_bytes=48 * MIB),
        name="band_attn",
    )(qkv_b, qkv_b, qkv_b, qkv_b, qkv_b, qkv_b, qkv_b, bias)


def _outproj_kernel(oa_ref, ob_ref, x_ref, mod_ref, nw_ref, woa_ref, wob_ref, wr_ref, br_ref,
                    x1_ref, h2_ref, route_ref, gate_ref, cnt_ref, carry_ref, *, n_experts):
    i = pl.program_id(0)
    tm = x_ref.shape[0]

    @pl.when(i == 0)
    def _():
        carry_ref[...] = jnp.zeros_like(carry_ref)

    y = _dot(oa_ref[...], woa_ref[...]) + _dot(ob_ref[...], wob_ref[...])
    ga1 = mod_ref[0, 2:3, :]
    sh2 = mod_ref[0, 3:4, :]
    sc2 = mod_ref[0, 4:5, :]
    x1 = x_ref[...] + ga1 * _rms(y, nw_ref[1:2, :])
    x1_ref[...] = x1
    h2 = _rms(x1, nw_ref[2:3, :]) * (1.0 + sc2) + sh2
    h2_ref[...] = h2

    h_hi = h2.astype(BF16)
    h_lo = (h2 - h_hi.astype(F32)).astype(BF16)
    logits = _dot(h_hi, wr_ref[0]) + (_dot(h_hi, wr_ref[1]) + _dot(h_lo, wr_ref[0])) + br_ref[...]
    lane_i = lax.broadcasted_iota(jnp.int32, (tm, LANES), 1)
    lane = lane_i.astype(F32)
    lg = jnp.where(lane_i < n_experts, logits, -jnp.inf)
    vals, idxs = [], []
    for _ in range(TOP_K):
        m = lg.max(axis=-1, keepdims=True)
        idx = jnp.where(lg == m, lane, float(LANES)).min(axis=-1, keepdims=True)
        vals.append(m)
        idxs.append(idx)
        lg = jnp.where(lane == idx, -jnp.inf, lg)
    ex = [jnp.exp(v - vals[0]) for v in vals]
    den = ex[0]
    for e in ex[1:]:
        den = den + e

    onehot = jnp.zeros((tm, LANES), F32)
    for idx in idxs:
        onehot = onehot + (lane == idx).astype(F32)
    row = lax.broadcasted_iota(jnp.int32, (tm, tm), 0)
    col = lax.broadcasted_iota(jnp.int32, (tm, tm), 1)
    before = (col < row).astype(BF16)
    cum = _dot(before, onehot.astype(BF16)) + carry_ref[0:1, :]
    carry = carry_ref[0:1, :] + onehot.sum(axis=0, keepdims=True)
    carry_ref[...] = jnp.broadcast_to(carry, carry_ref.shape)
    cnt_ref[...] = jnp.broadcast_to(carry, cnt_ref.shape)

    route = jnp.zeros((tm, LANES), F32)
    gate = jnp.zeros((tm, LANES), F32)
    for k in range(TOP_K):
        rank = jnp.where(lane == idxs[k], cum, 0.0).sum(axis=-1, keepdims=True)
        route = jnp.where(lane_i == k, idxs[k], route)
        route = jnp.where(lane_i == TOP_K + k, rank, route)
        gate = jnp.where(lane_i == k, ex[k] / den, gate)
    route_ref[...] = route.astype(jnp.int32)
    gate_ref[...] = gate


def _outproj(o_a, o_b, x2, mod3, norm_w, w_out_a, w_out_b, w_router, b_router, seq):
    n, d = x2.shape
    n_experts = w_router.shape[1]
    tm = 512
    tiles_per_seq = seq // tm
    wr32 = jnp.zeros((d, LANES), F32).at[:, :n_experts].set(w_router)
    wr_hi = wr32.astype(BF16)
    wr = jnp.stack([wr_hi, (wr32 - wr_hi.astype(F32)).astype(BF16)])
    br = jnp.zeros((1, LANES), F32).at[0, :n_experts].set(b_router)
    kern = functools.partial(_outproj_kernel, n_experts=n_experts)
    aw = o_a.shape[1]
    bw = o_b.shape[1]
    return pl.pallas_call(
        kern,
        grid=(n // tm,),
        in_specs=[
            pl.BlockSpec((tm, aw), lambda i: (i, 0)),
            pl.BlockSpec((tm, bw), lambda i: (i, 0)),
            pl.BlockSpec((tm, d), lambda i: (i, 0)),
            pl.BlockSpec((1, 6, d), lambda i: (i // tiles_per_seq, 0, 0)),
            pl.BlockSpec((4, d), lambda i: (0, 0)),
            pl.BlockSpec((aw, d), lambda i: (0, 0)),
            pl.BlockSpec((bw, d), lambda i: (0, 0)),
            pl.BlockSpec((2, d, LANES), lambda i: (0, 0, 0)),
            pl.BlockSpec((1, LANES), lambda i: (0, 0)),
        ],
        out_specs=[
            pl.BlockSpec((tm, d), lambda i: (i, 0)),
            pl.BlockSpec((tm, d), lambda i: (i, 0)),
            pl.BlockSpec((tm, LANES), lambda i: (i, 0)),
            pl.BlockSpec((tm, LANES), lambda i: (i, 0)),
            pl.BlockSpec((8, LANES), lambda i: (0, 0)),
        ],
        out_shape=[
            jax.ShapeDtypeStruct((n, d), F32),
            jax.ShapeDtypeStruct((n, d), F32),
            jax.ShapeDtypeStruct((n, LANES), jnp.int32),
            jax.ShapeDtypeStruct((n, LANES), F32),
            jax.ShapeDtypeStruct((8, LANES), F32),
        ],
        scratch_shapes=[pltpu.VMEM((8, LANES), F32)],
        compiler_params=pltpu.CompilerParams(
            dimension_semantics=("arbitrary",), vmem_limit_bytes=48 * MIB),
        name="outproj_router",
    )(o_a, o_b, x2, mod3, norm_w, w_out_a, w_out_b, wr, br)


def _dispatch_kernel(pend_ref, vend_ref, nu_ref, dest_ref, h_ref, xs_hbm, stage, zeros, sem, zsem, tsem,
                     *, n_experts, blk):
    i = pl.program_id(0)
    n_tiles = pl.num_programs(0)
    tm, d = h_ref.shape
    slot = lax.rem(i, 2)
    piece = zeros.shape[0]
    shift = int(math.log2(piece))
    n_pieces = xs_hbm.shape[0] // piece

    def wait_rows(s):
        for _ in range(TOP_K):
            pltpu.make_async_copy(stage.at[s], xs_hbm.at[pl.ds(0, tm), :], sem.at[s]).wait()

    def clear(sem_ref):
        def body(p, carry):
            start = pl.multiple_of(p * piece, piece)
            pltpu.make_async_copy(zeros, xs_hbm.at[pl.ds(start, piece), :], sem_ref).start()
            return carry
        return body

    def wait_clear(sem_ref):
        def body(p, carry):
            pltpu.make_async_copy(zeros, xs_hbm.at[pl.ds(0, piece), :], sem_ref).wait()
            return carry
        return body

    @pl.when(i == 0)
    def _():
        zeros[...] = jnp.zeros_like(zeros)
        for e in range(n_experts):
            lax.fori_loop(vend_ref[e] >> shift, pend_ref[e] >> shift, clear(zsem), 0)
        lax.fori_loop((nu_ref[0] * blk) >> shift, n_pieces, clear(tsem), 0)
        for e in range(n_experts):
            lax.fori_loop(vend_ref[e] >> shift, pend_ref[e] >> shift, wait_clear(zsem), 0)

    @pl.when(i >= 2)
    def _():
        wait_rows(slot)

    stage[slot] = h_ref[...]

    for t in range(tm):
        for k in range(TOP_K):
            s = dest_ref[0, 0, t * TOP_K + k]
            pltpu.make_async_copy(stage.at[slot, pl.ds(t, 1), :], xs_hbm.at[pl.ds(s, 1), :],
                                  sem.at[slot]).start(priority=k % 2)

    @pl.when(i == n_tiles - 1)
    def _():
        wait_rows(slot)

        @pl.when(n_tiles > 1)
        def _():
            wait_rows(1 - slot)

        lax.fori_loop((nu_ref[0] * blk) >> shift, n_pieces, wait_clear(tsem), 0)


def _dispatch(h2, dest, pend, vend, n_used, cap):
    n, d = h2.shape
    n_experts = pend.shape[0]
    tm = 256
    n_tiles = n // tm
    kern = functools.partial(_dispatch_kernel, n_experts=n_experts, blk=MOE_BLOCK)
    grid_spec = pltpu.PrefetchScalarGridSpec(
        num_scalar_prefetch=3,
        grid=(n_tiles,),
        in_specs=[
            pl.BlockSpec((1, 1, tm * TOP_K), lambda i, pe, pa, nu: (i, 0, 0), memory_space=pltpu.SMEM),
            pl.BlockSpec((tm, d), lambda i, pe, pa, nu: (i, 0)),
        ],
        out_specs=pl.BlockSpec(memory_space=pl.ANY),
        scratch_shapes=[
            pltpu.VMEM((2, tm, d), F32),
            pltpu.VMEM((CLEAR_ROWS, d), F32),
            pltpu.SemaphoreType.DMA((2,)),
            pltpu.SemaphoreType.DMA(()),
            pltpu.SemaphoreType.DMA(()),
        ],
    )
    return pl.pallas_call(
        kern,
        grid_spec=grid_spec,
        out_shape=jax.ShapeDtypeStruct((cap, d), F32),
        compiler_params=pltpu.CompilerParams(
            dimension_semantics=("arbitrary",), vmem_limit_bytes=32 * MIB),
        name="dispatch",
    )(pend, vend, n_used, dest.reshape(n_tiles, 1, tm * TOP_K), h2)


def _expert_kernel(be_ref, nu_ref, nv_ref, nx_ref, xs_ref, wup_hbm, bup_ref, wdn_hbm, bdn_ref, y_ref,
                   up_stage, dn_stage, wsem, wup16, wdn16):
    j = pl.program_id(0)
    blk, d = xs_ref.shape
    f = wdn16.shape[0]
    n_valid = nv_ref[j]
    half = blk // 2

    def weight_copies(e):
        return (pltpu.make_async_copy(wup_hbm.at[e], up_stage, wsem.at[0]),
                pltpu.make_async_copy(wdn_hbm.at[e], dn_stage, wsem.at[1]))

    prev = be_ref[jnp.maximum(j - 1, 0)]
    first_of_expert = jnp.logical_and(n_valid > 0, jnp.logical_or(j == 0, be_ref[j] != prev))

    @pl.when(jnp.logical_and(j == 0, n_valid > 0))
    def _():
        for c in weight_copies(be_ref[0]):
            c.start()

    @pl.when(first_of_expert)
    def _():
        for c in weight_copies(be_ref[j]):
            c.wait()
        step = 128
        for r0 in range(0, d, step):
            wup16[r0:r0 + step, :] = up_stage[r0:r0 + step, :].astype(BF16)
        for r0 in range(0, f, step):
            wdn16[r0:r0 + step, :] = dn_stage[r0:r0 + step, :].astype(BF16)

        @pl.when(nx_ref[j] >= 0)
        def _():
            for c in weight_copies(nx_ref[j]):
                c.start()

    def mlp(rows):
        xb = xs_ref[0:rows, :].astype(BF16)
        fc = min(f, 512)
        acc = None
        for c0 in range(0, f, fc):
            glu = _dot(xb, wup16[:, c0:c0 + fc]) + bup_ref[0, :, c0:c0 + fc]
            lin = _dot(xb, wup16[:, f + c0:f + c0 + fc]) + bup_ref[0, :, f + c0:f + c0 + fc]
            glu = jnp.minimum(glu, SWIGLU_LIMIT)
            lin = jnp.clip(lin, -SWIGLU_LIMIT, SWIGLU_LIMIT)
            act = glu * jax.nn.sigmoid(SWIGLU_ALPHA * glu) * (lin + 1.0)
            part = _dot(act.astype(BF16), wdn16[c0:c0 + fc, :])
            acc = part if acc is None else acc + part
        y_ref[0:rows, :] = acc + bdn_ref[0]

    @pl.when(n_valid > half)
    def _():
        mlp(blk)

    @pl.when(jnp.logical_and(n_valid > 0, n_valid <= half))
    def _():
        mlp(half)
        y_ref[half:, :] = jnp.zeros((blk - half, d), y_ref.dtype)

    @pl.when(n_valid == 0)
    def _():
        y_ref[...] = jnp.zeros_like(y_ref)


def _experts(xs, block_e, n_used, n_valid, next_e, w_up, b_up, w_down, b_down):
    cap, d = xs.shape
    n_exp, _, f2 = w_up.shape
    f = w_down.shape[1]
    blk = MOE_BLOCK
    n_blocks = cap // blk
    grid_spec = pltpu.PrefetchScalarGridSpec(
        num_scalar_prefetch=4,
        grid=(n_blocks,),
        in_specs=[
            pl.BlockSpec((blk, d), lambda j, be, nu, nv, nx: (jnp.minimum(j, jnp.maximum(nu[0] - 1, 0)), 0)),
            pl.BlockSpec(memory_space=pl.ANY),
            pl.BlockSpec((1, 1, f2), lambda j, be, nu, nv, nx: (be[j], 0, 0)),
            pl.BlockSpec(memory_space=pl.ANY),
            pl.BlockSpec((1, 1, d), lambda j, be, nu, nv, nx: (be[j], 0, 0)),
        ],
        out_specs=pl.BlockSpec((blk, d), lambda j, be, nu, nv, nx: (j, 0)),
        scratch_shapes=[
            pltpu.VMEM((d, f2), F32),
            pltpu.VMEM((f, d), F32),
            pltpu.SemaphoreType.DMA((2,)),
            pltpu.VMEM((d, f2), BF16),
            pltpu.VMEM((f, d), BF16),
        ],
    )
    return pl.pallas_call(
        _expert_kernel,
        grid_spec=grid_spec,
        out_shape=jax.ShapeDtypeStruct((cap, d), F32),
        compiler_params=pltpu.CompilerParams(
            dimension_semantics=("arbitrary",), vmem_limit_bytes=56 * MIB),
        name="experts",
    )(block_e, n_used, n_valid, next_e, xs, w_up, b_up.reshape(n_exp, 1, f2), w_down, b_down.reshape(n_exp, 1, d))


def _combine_kernel(dest_ref, dest_next_ref, y_hbm, gate_ref, x1_ref, mod_ref, nw_ref, o_ref, buf, sem):
    i = pl.program_id(0)
    n_tiles = pl.num_programs(0)
    tm, d = x1_ref.shape
    slot = lax.rem(i, 2)

    def start_gather(idx_ref, dst_slot):
        for t in range(tm):
            for k in range(TOP_K):
                s = idx_ref[0, 0, t * TOP_K + k]
                pltpu.make_async_copy(y_hbm.at[pl.ds(s, 1), :], buf.at[dst_slot, k, pl.ds(t, 1), :],
                                      sem.at[dst_slot]).start(priority=k % 2)

    @pl.when(i == 0)
    def _():
        start_gather(dest_ref, 0)

    @pl.when(i + 1 < n_tiles)
    def _():
        start_gather(dest_next_ref, 1 - slot)

    for k in range(TOP_K):
        pltpu.make_async_copy(y_hbm.at[pl.ds(0, tm), :], buf.at[slot, k], sem.at[slot]).wait()

    gate = gate_ref[...]
    ysum = gate[:, 0:1] * buf[slot, 0]
    for k in range(1, TOP_K):
        ysum = ysum + gate[:, k:k + 1] * buf[slot, k]
    ga2 = mod_ref[0, 5:6, :]
    o_ref[...] = x1_ref[...] + ga2 * _rms(ysum, nw_ref[3:4, :])


def _combine(y_sorted, dest, gates, x1, mod3, norm_w, seq):
    n, d = x1.shape
    tm = 256
    n_tiles = n // tm
    tiles_per_seq = seq // tm
    dest3 = dest.reshape(n_tiles, 1, tm * TOP_K)
    return pl.pallas_call(
        _combine_kernel,
        grid=(n_tiles,),
        in_specs=[
            pl.BlockSpec((1, 1, tm * TOP_K), lambda i: (i, 0, 0), memory_space=pltpu.SMEM),
            pl.BlockSpec((1, 1, tm * TOP_K), lambda i: (jnp.minimum(i + 1, n_tiles - 1), 0, 0),
                         memory_space=pltpu.SMEM),
            pl.BlockSpec(memory_space=pl.ANY),
            pl.BlockSpec((tm, LANES), lambda i: (i, 0)),
            pl.BlockSpec((tm, d), lambda i: (i, 0)),
            pl.BlockSpec((1, 6, d), lambda i: (i // tiles_per_seq, 0, 0)),
            pl.BlockSpec((4, d), lambda i: (0, 0)),
        ],
        out_specs=pl.BlockSpec((tm, d), lambda i: (i, 0)),
        out_shape=jax.ShapeDtypeStruct((n, d), F32),
        scratch_shapes=[
            pltpu.VMEM((2, TOP_K, tm, d), F32),
            pltpu.SemaphoreType.DMA((2,)),
        ],
        compiler_params=pltpu.CompilerParams(
            dimension_semantics=("arbitrary",), vmem_limit_bytes=32 * MIB),
        name="combine",
    )(dest3, dest3, y_sorted, gates, x1, mod3, norm_w)


def _layer(x, c, w_ada, b_ada, norm_w, w_in, conv_w, a_log, dt_bias, a_norm_w, rel_bias, w_out,
           w_router, b_router, w_up, b_up, w_down, b_down):
    bsz, seq, d = x.shape
    n = bsz * seq
    a_width = conv_w.shape[1] // 3
    a_heads = a_log.shape[0]
    b_width = w_out.shape[0] - a_width
    n_experts = w_router.shape[1]
    off_gate = 4 * a_width
    off_b = off_gate + 2 * a_heads

    mod3 = _adaln(c, w_ada, b_ada).reshape(bsz, 6, d)
    x2 = x.reshape(n, d)

    w_main = jnp.concatenate([w_in[:, :off_gate], w_in[:, off_b:]], axis=1).astype(BF16)
    w_gate = jnp.zeros((d, LANES), F32).at[:, :2 * a_heads].set(w_in[:, off_gate:off_b]).astype(BF16)
    qkv_a, z_a, ab, qkv_b = _inproj(x2, mod3, norm_w, w_main, w_gate, conv_w, seq, 3 * a_width, a_width,
                                    3 * b_width)

    o_a = _gdn(qkv_a, z_a, ab, a_log, dt_bias, a_norm_w, bsz, seq)
    o_b = _band_attn(qkv_b, rel_bias, bsz, seq)

    w_out16 = w_out.astype(BF16)
    x1, h2, route, gates, cnt = _outproj(o_a, o_b, x2, mod3, norm_w, w_out16[:a_width], w_out16[a_width:],
                                         w_router, b_router, seq)

    top_idx = route[:, :TOP_K]
    rank = route[:, TOP_K:2 * TOP_K]
    counts = cnt[0, :n_experts].astype(jnp.int32)
    padded = (counts + MOE_BLOCK - 1) // MOE_BLOCK * MOE_BLOCK
    pend = jnp.cumsum(padded)
    pstart = pend - padded
    expert_ids = jnp.arange(n_experts, dtype=jnp.int32)
    dest = jnp.sum(jnp.where(top_idx[..., None] == expert_ids, pstart, 0), axis=-1) + rank
    n_assign = n * TOP_K
    n_blocks = -(-n_assign // MOE_BLOCK) + n_experts
    cap = n_blocks * MOE_BLOCK
    block_start = jnp.arange(n_blocks, dtype=jnp.int32) * MOE_BLOCK
    block_e = jnp.minimum(jnp.sum(pend[None, :] <= block_start[:, None], axis=1), n_experts - 1).astype(jnp.int32)
    n_used = (pend[-1:] // MOE_BLOCK).astype(jnp.int32)
    dest = dest.astype(jnp.int32)

    vend = (pstart + counts).astype(jnp.int32)
    block_vend = jnp.sum(jnp.where(block_e[:, None] == expert_ids, vend, 0), axis=1)
    n_valid = jnp.clip(block_vend - block_start, 0, MOE_BLOCK).astype(jnp.int32)

    xs = _dispatch(h2, dest, pend.astype(jnp.int32), vend, n_used, cap)
    blocks = jnp.arange(n_blocks, dtype=jnp.int32)
    later = (blocks[None, :] > blocks[:, None]) & (block_e[None, :] != block_e[:, None]) & (n_valid[None, :] > 0)
    first_later = jnp.min(jnp.where(later, blocks[None, :], n_blocks), axis=1)
    next_e = jnp.sum(jnp.where(blocks[None, :] == first_later[:, None], block_e[None, :] + 1, 0), axis=1) - 1
    y_sorted = _experts(xs, block_e, n_used, n_valid, next_e.astype(jnp.int32), w_up, b_up, w_down, b_down)
    out = _combine(y_sorted, dest, gates, x1, mod3, norm_w, seq)
    return out.reshape(bsz, seq, d)


def kernel(x, c, w_ada, b_ada, norm_w, w_in, conv_w, a_log, dt_bias, a_norm_w, rel_bias, w_out,
           w_router, b_router, w_up, b_up, w_down, b_down):
    for l in range(w_ada.shape[0]):
        x = _layer(x, c, w_ada[l], b_ada[l], norm_w[l], w_in[l], conv_w[l], a_log[l], dt_bias[l],
                   a_norm_w[l], rel_bias[l], w_out[l], w_router[l], b_router[l], w_up[l], b_up[l],
                   w_down[l], b_down[l])
    return x
```

```python
import functools
import math

import jax
import jax.numpy as jnp
from jax import lax
from jax.experimental import pallas as pl
from jax.experimental.pallas import tpu as pltpu

F32 = jnp.float32
BF16 = jnp.bfloat16
HIGHEST = lax.Precision.HIGHEST

EPS = 1e-6
CHUNK = 64
CONV_K = 4
A_HEAD_DIM = 128
B_HEAD_DIM = 64
B_PREV_CHUNKS = 8
REL_CLIP = 128
TOP_K = 4
SWIGLU_ALPHA = 1.702
SWIGLU_LIMIT = 7.0
MOE_BLOCK = 512
CLEAR_ROWS = 64
LANES = 128
NEG_BIG = -1e30

MIB = 1024 * 1024


def _dot(a, b):
    return jnp.dot(a, b, preferred_element_type=F32)


def _dot_nt(a, b):
    return lax.dot_general(a, b, (((1,), (1,)), ((), ())), preferred_element_type=F32)


def _dot_exact(a, b):
    return jnp.dot(a, b, precision=HIGHEST, preferred_element_type=F32)


def _rms(x, w):
    return x * lax.rsqrt(jnp.mean(x * x, axis=-1, keepdims=True) + EPS) * w


def _adaln_kernel(c_ref, w_ref, b_ref, o_ref):
    cs = c_ref[...]
    cs = cs * jax.nn.sigmoid(cs)
    o_ref[...] = _dot_exact(cs, w_ref[...]) + b_ref[...]


def _adaln(c, w, b):
    bsz, d = c.shape
    n_out = w.shape[1]
    rows = 8
    cp = jnp.zeros((rows, d), F32).at[:bsz].set(c)
    tn = 512
    out = pl.pallas_call(
        _adaln_kernel,
        grid=(n_out // tn,),
        in_specs=[
            pl.BlockSpec((rows, d), lambda j: (0, 0)),
            pl.BlockSpec((d, tn), lambda j: (0, j)),
            pl.BlockSpec((1, tn), lambda j: (0, j)),
        ],
        out_specs=pl.BlockSpec((rows, tn), lambda j: (0, j)),
        out_shape=jax.ShapeDtypeStruct((rows, n_out), F32),
        name="adaln",
    )(cp, w, b.reshape(1, n_out))
    return out[:bsz]


def _inproj_kernel(x_ref, mod_ref, nw_ref, wm_ref, wg_ref, cw_ref, qkva_ref, z_ref, ab_ref, qkvb_ref, raw_ref,
                   *, a_qkv, a_z, tiles_per_seq):
    i = pl.program_id(0)
    tm = x_ref.shape[0]
    dk = A_HEAD_DIM
    a_width = a_qkv // 3
    halo = 8
    tn = 512

    @pl.when(lax.rem(i, tiles_per_seq) == 0)
    def _():
        raw_ref[0:halo, :] = jnp.zeros((halo, a_qkv), F32)

    x = x_ref[...]
    sh = mod_ref[0, 0:1, :]
    sc = mod_ref[0, 1:2, :]
    h = _rms(x, nw_ref[0:1, :]) * (1.0 + sc) + sh
    hb = h.astype(BF16)

    def conv_silu(c0):
        acc = cw_ref[CONV_K - 1:CONV_K, c0:c0 + dk] * raw_ref[halo:halo + tm, c0:c0 + dk]
        for j in range(CONV_K - 1):
            start = halo - (CONV_K - 1) + j
            acc = acc + cw_ref[j:j + 1, c0:c0 + dk] * raw_ref[start:start + tm, c0:c0 + dk]
        y = acc * jax.nn.sigmoid(acc)
        if c0 < 2 * a_width:
            y = y * lax.rsqrt(jnp.sum(y * y, axis=-1, keepdims=True) + EPS)
        if c0 < a_width:
            y = y * (dk ** -0.5)
        qkva_ref[:, c0:c0 + dk] = y

    for c0 in range(0, a_qkv, tn):
        raw_ref[halo:halo + tm, c0:c0 + tn] = _dot(hb, wm_ref[:, c0:c0 + tn])
    slabs = list(range(0, a_qkv, dk))
    other = [("z", c0) for c0 in range(0, a_z, tn)] + [("b", c0) for c0 in range(0, qkvb_ref.shape[1], tn)]
    per = -(-len(slabs) // len(other))
    off = a_qkv + a_z
    for n_o, (kind, c0) in enumerate(other):
        for c in slabs[n_o * per:(n_o + 1) * per]:
            conv_silu(c)
        if kind == "z":
            z_ref[:, c0:c0 + tn] = _dot(hb, wm_ref[:, a_qkv + c0:a_qkv + c0 + tn])
        else:
            qkvb_ref[:, c0:c0 + tn] = _dot(hb, wm_ref[:, off + c0:off + c0 + tn]).astype(BF16)
    ab_ref[...] = _dot(hb, wg_ref[...])
    raw_ref[0:halo, :] = raw_ref[tm:tm + halo, :]


def _inproj(x2, mod3, norm_w, w_main, w_gate, conv_w, seq, a_qkv, a_z, b_qkv):
    n, d = x2.shape
    tm = 512
    tiles_per_seq = seq // tm
    kern = functools.partial(_inproj_kernel, a_qkv=a_qkv, a_z=a_z, tiles_per_seq=tiles_per_seq)
    return pl.pallas_call(
        kern,
        grid=(n // tm,),
        in_specs=[
            pl.BlockSpec((tm, d), lambda i: (i, 0)),
            pl.BlockSpec((1, 6, d), lambda i: (i // tiles_per_seq, 0, 0)),
            pl.BlockSpec((4, d), lambda i: (0, 0)),
            pl.BlockSpec(w_main.shape, lambda i: (0, 0)),
            pl.BlockSpec(w_gate.shape, lambda i: (0, 0)),
            pl.BlockSpec(conv_w.shape, lambda i: (0, 0)),
        ],
        out_specs=[
            pl.BlockSpec((tm, a_qkv), lambda i: (i, 0)),
            pl.BlockSpec((tm, a_z), lambda i: (i, 0)),
            pl.BlockSpec((tm, LANES), lambda i: (i, 0)),
            pl.BlockSpec((tm, b_qkv), lambda i: (i, 0)),
        ],
        out_shape=[
            jax.ShapeDtypeStruct((n, a_qkv), F32),
            jax.ShapeDtypeStruct((n, a_z), F32),
            jax.ShapeDtypeStruct((n, LANES), F32),
            jax.ShapeDtypeStruct((n, b_qkv), BF16),
        ],
        scratch_shapes=[pltpu.VMEM((tm + 8, a_qkv), F32)],
        compiler_params=pltpu.CompilerParams(
            dimension_semantics=("arbitrary",), vmem_limit_bytes=48 * MIB),
        name="inproj",
    )(x2, mod3, norm_w, w_main, w_gate, conv_w)


def _gdn_kernel(qkv_ref, z_ref, ab_ref, alog_ref, dtb_ref, anw_ref, o_ref, state_ref, *, n_heads):
    i = pl.program_id(1)
    tt = qkv_ref.shape[0]
    width = qkv_ref.shape[1]
    a_width = width // 3
    dk = A_HEAD_DIM
    sub = 2 * CHUNK
    per_sub = sub // CHUNK

    @pl.when(i == 0)
    def _():
        state_ref[...] = jnp.zeros_like(state_ref)

    row = lax.broadcasted_iota(jnp.int32, (sub, sub), 0)
    col = lax.broadcasted_iota(jnp.int32, (sub, sub), 1)
    shift = int(math.log2(CHUNK))
    same = (row >> shift) == (col >> shift)
    tri_incl = same & (col <= row)
    tri_strict = same & (col < row)
    eye = (row == col).astype(F32)
    pair = (row >> 1) == (col >> 1)
    couples = [((row >> (lv + 1)) == (col >> (lv + 1))) & ((row >> lv) != (col >> lv))
               for lv in range(1, shift)]
    col_chunk = lax.broadcasted_iota(jnp.int32, (dk, sub), 1) >> shift
    tri_incl_f = tri_incl.astype(F32)
    same_f = same.astype(F32)

    n_sub = tt // sub
    gates = []
    for p in range(n_sub):
        r0 = p * sub
        ab = ab_ref[r0:r0 + sub, :]
        gfull = -jnp.exp(alog_ref[...]) * jax.nn.softplus(ab + dtb_ref[...])
        bfull = jax.nn.sigmoid(ab)
        gcum = _dot_exact(tri_incl_f, gfull)
        gtot = _dot_exact(same_f, gfull)
        gates.append((gcum, gcum.T, gtot, bfull))

    items = [(p, h) for p in range(n_sub) for h in range(n_heads)]
    pre = {}
    for p, h in items:
        r0 = p * sub
        gcum, gcum_t, gtot, bfull = gates[p]
        gc = gcum[:, h:h + 1]
        beta = bfull[:, n_heads + h:n_heads + h + 1]
        q = qkv_ref[r0:r0 + sub, h * dk:(h + 1) * dk]
        k = qkv_ref[r0:r0 + sub, a_width + h * dk:a_width + (h + 1) * dk]
        v = qkv_ref[r0:r0 + sub, 2 * a_width + h * dk:2 * a_width + (h + 1) * dk]
        decay = jnp.exp(jnp.where(tri_incl, gc - gcum_t[h:h + 1, :], -jnp.inf))
        kb = k * beta
        k16 = k.astype(BF16)
        lower = jnp.where(tri_strict, _dot_nt(kb.astype(BF16), k16) * decay, 0.0)
        attn16 = (_dot_nt(q.astype(BF16), k16) * decay).astype(BF16)
        eg = jnp.exp(gc)
        rhs = jnp.concatenate([kb * eg, v * beta], axis=1).astype(BF16)
        kd_t = (k * jnp.exp(gtot[:, h:h + 1] - gc)).T
        kd_st = jnp.concatenate([jnp.where(col_chunk == c, kd_t, 0.0) for c in range(per_sub)],
                                axis=0).astype(BF16)
        pre[p, h] = dict(lower=lower, attn16=attn16, rhs=rhs, kd_st=kd_st, qeg=q * eg)

    tmat = {it: eye - jnp.where(pair, pre[it]["lower"], 0.0) for it in items}
    for couple in couples:
        t16 = {it: tmat[it].astype(BF16) for it in items}
        half = {it: _dot(t16[it], jnp.where(couple, pre[it]["lower"], 0.0).astype(BF16)) for it in items}
        tmat = {it: tmat[it] - _dot(half[it].astype(BF16), t16[it]) for it in items}

    wu16 = {it: _dot(tmat[it].astype(BF16), pre[it]["rhs"]).astype(BF16) for it in items}
    a_wu = {it: _dot(pre[it]["attn16"], wu16[it]) for it in items}
    k_wu = {it: _dot(pre[it]["kd_st"], wu16[it]) for it in items}
    qp16 = {it: (pre[it]["qeg"] - a_wu[it][:, :dk]).astype(BF16) for it in items}

    state = [state_ref[h] for h in range(n_heads)]
    for p in range(n_sub):
        r0 = p * sub
        gtot = gates[p][2]
        outs = [[] for _ in range(n_heads)]
        for c in range(per_sub):
            c0 = c * CHUNK
            for h in range(n_heads):
                kw = k_wu[p, h]
                m16 = kw[c * dk:(c + 1) * dk, :dk].astype(BF16)
                res = _dot(jnp.concatenate([m16, qp16[p, h][c0:c0 + CHUNK]], axis=0), state[h].astype(BF16))
                outs[h].append(res[dk:] + a_wu[p, h][c0:c0 + CHUNK, dk:])
                state[h] = (state[h] * jnp.exp(gtot[c0:c0 + 1, h:h + 1]) - res[:dk]
                            + kw[c * dk:(c + 1) * dk, dk:])
        for h in range(n_heads):
            o = jnp.concatenate(outs[h], axis=0)
            zh = z_ref[r0:r0 + sub, h * dk:(h + 1) * dk]
            o_ref[r0:r0 + sub, h * dk:(h + 1) * dk] = (
                _rms(o, anw_ref[...]) * (zh * jax.nn.sigmoid(zh))).astype(o_ref.dtype)
    for h in range(n_heads):
        state_ref[h] = state[h]


def _gdn(qkv_a, z, ab, a_log, dt_bias, a_norm_w, bsz, seq):
    n, width = qkv_a.shape
    a_width = width // 3
    n_heads = a_width // A_HEAD_DIM
    tt = 256
    nt = seq // tt
    alog = jnp.zeros((1, LANES), F32).at[0, :n_heads].set(a_log)
    dtb = jnp.zeros((1, LANES), F32).at[0, :n_heads].set(dt_bias)
    kern = functools.partial(_gdn_kernel, n_heads=n_heads)
    return pl.pallas_call(
        kern,
        grid=(bsz, nt),
        in_specs=[
            pl.BlockSpec((tt, width), lambda b, i: (b * nt + i, 0)),
            pl.BlockSpec((tt, a_width), lambda b, i: (b * nt + i, 0)),
            pl.BlockSpec((tt, LANES), lambda b, i: (b * nt + i, 0)),
            pl.BlockSpec((1, LANES), lambda b, i: (0, 0)),
            pl.BlockSpec((1, LANES), lambda b, i: (0, 0)),
            pl.BlockSpec((1, A_HEAD_DIM), lambda b, i: (0, 0)),
        ],
        out_specs=pl.BlockSpec((tt, a_width), lambda b, i: (b * nt + i, 0)),
        out_shape=jax.ShapeDtypeStruct((n, a_width), BF16),
        scratch_shapes=[pltpu.VMEM((n_heads, A_HEAD_DIM, A_HEAD_DIM), F32)],
        compiler_params=pltpu.CompilerParams(
            dimension_semantics=("arbitrary", "arbitrary"), vmem_limit_bytes=48 * MIB),
        name="gdn",
    )(qkv_a, z, ab, alog, dtb, a_norm_w.reshape(1, A_HEAD_DIM))


def _attn_kernel(q_ref, k0_ref, k1_ref, k2_ref, v0_ref, v1_ref, v2_ref, bias_ref, o_ref, *, n_heads):
    i = pl.program_id(1)
    tq = q_ref.shape[0]
    hd = B_HEAD_DIM
    per = LANES // hd
    lane = lax.broadcasted_iota(jnp.int32, (1, LANES), 1)
    k_refs = (k0_ref, k1_ref, k2_ref)
    v_refs = (v0_ref, v1_ref, v2_ref)
    n_kb = len(k_refs)
    heads = range(n_heads)

    def lanes_of(h):
        return slice((h // per) * LANES, (h // per + 1) * LANES)

    def in_head(h):
        hh = h % per
        return (lane >= hh * hd) & (lane < (hh + 1) * hd)

    scores = []
    for h in heads:
        qg = q_ref[:, lanes_of(h)]
        qh = jnp.where(in_head(h), qg, jnp.zeros_like(qg)) * (hd ** -0.5)
        s = []
        for j in range(n_kb):
            sj = _dot_nt(qh, k_refs[j][:, lanes_of(h)]) + bias_ref[h, :, j * tq:(j + 1) * tq]
            if j < n_kb - 1:
                sj = jnp.where(i >= n_kb - 1 - j, sj, NEG_BIG)
            s.append(sj)
        scores.append(s)
    probs, dens = [], []
    for h in heads:
        s = scores[h]
        top = s[0]
        for sj in s[1:]:
            top = jnp.maximum(top, sj)
        m = top.max(axis=-1, keepdims=True)
        p = [jnp.exp(sj - m) for sj in s]
        tot = p[0]
        for pj in p[1:]:
            tot = tot + pj
        probs.append([pj.astype(BF16) for pj in p])
        dens.append(tot.sum(axis=-1, keepdims=True))
    outs = []
    for h in heads:
        acc = _dot(probs[h][0], v_refs[0][:, lanes_of(h)])
        for j in range(1, n_kb):
            acc = acc + _dot(probs[h][j], v_refs[j][:, lanes_of(h)])
        outs.append(acc / dens[h])
    for g in range(n_heads // per):
        out = outs[g * per]
        for hh in range(1, per):
            out = jnp.where(in_head(g * per + hh), outs[g * per + hh], out)
        o_ref[:, g * LANES:(g + 1) * LANES] = out.astype(o_ref.dtype)


def _band_bias(rel_bias, tq, n_kb):
    n_h = rel_bias.shape[0]
    back = (n_kb - 1) * tq
    nk = n_kb * tq
    span = tq + nk - 1
    lo = (nk - 1 - back) - REL_CLIP
    hi = span - lo - (2 * REL_CLIP + 1)
    by_offset = jnp.concatenate([jnp.broadcast_to(rel_bias[:, :1], (n_h, lo)), rel_bias,
                                 jnp.broadcast_to(rel_bias[:, -1:], (n_h, hi))], axis=1).astype(F32)
    rev = jnp.pad(by_offset[:, ::-1], ((0, 0), (0, 1)))
    skew = jnp.broadcast_to(rev[:, None, :], (n_h, tq, span + 1)).reshape(n_h, tq * (span + 1))
    skew = skew[:, :tq * span].reshape(n_h, tq, span)
    table = skew[:, :, tq - 1:tq - 1 + nk]
    qc = jnp.arange(tq)[:, None] // CHUNK
    kc = jnp.arange(nk)[None, :] // CHUNK - back // CHUNK
    allowed = (kc <= qc) & (kc >= qc - B_PREV_CHUNKS)
    return jnp.where(allowed[None], table, NEG_BIG)


def _band_attn(qkv_b, rel_bias, bsz, seq):
    n, width = qkv_b.shape
    b_width = width // 3
    n_heads = b_width // B_HEAD_DIM
    tq = 256
    n_kb = 3
    assert (n_kb - 1) * tq == B_PREV_CHUNKS * CHUNK
    nt = seq // tq
    bias = _band_bias(rel_bias, tq, n_kb)
    kern = functools.partial(_attn_kernel, n_heads=n_heads)

    def kv_spec(colblk, back):
        return pl.BlockSpec((tq, b_width), lambda b, i: (b * nt + jnp.maximum(i - back, 0), colblk))

    return pl.pallas_call(
        kern,
        grid=(bsz, nt),
        in_specs=[
            pl.BlockSpec((tq, b_width), lambda b, i: (b * nt + i, 0)),
            kv_spec(1, 2), kv_spec(1, 1), kv_spec(1, 0),
            kv_spec(2, 2), kv_spec(2, 1), kv_spec(2, 0),
            pl.BlockSpec(bias.shape, lambda b, i: (0, 0, 0)),
        ],
        out_specs=pl.BlockSpec((tq, b_width), lambda b, i: (b * nt + i, 0)),
        out_shape=jax.ShapeDtypeStruct((n, b_width), BF16),
        compiler_params=pltpu.CompilerParams(
            dimension_semantics=("parallel", "parallel"), vmem_limit_bytes=48 * MIB),
        name="band_attn",
    )(qkv_b, qkv_b, qkv_b, qkv_b, qkv_b, qkv_b, qkv_b, bias)


def _outproj_kernel(oa_ref, ob_ref, x_ref, mod_ref, nw_ref, woa_ref, wob_ref, wr_ref, br_ref,
                    x1_ref, h2_ref, route_ref, gate_ref, cnt_ref, carry_ref, *, n_experts):
    i = pl.program_id(0)
    tm = x_ref.shape[0]
    sub = 256
    parts = [slice(r0, r0 + sub) for r0 in range(0, tm, sub)]

    @pl.when(i == 0)
    def _():
        carry_ref[...] = jnp.zeros_like(carry_ref)

    ga1 = mod_ref[0, 2:3, :]
    sh2 = mod_ref[0, 3:4, :]
    sc2 = mod_ref[0, 4:5, :]
    ys = [_dot(oa_ref[r, :], woa_ref[...]) + _dot(ob_ref[r, :], wob_ref[...]) for r in parts]
    h2s = []
    for r, y in zip(parts, ys):
        x1 = x_ref[r, :] + ga1 * _rms(y, nw_ref[1:2, :])
        x1_ref[r, :] = x1
        h2 = _rms(x1, nw_ref[2:3, :]) * (1.0 + sc2) + sh2
        h2_ref[r, :] = h2
        h2s.append(h2)

    logits = []
    for h2 in h2s:
        h_hi = h2.astype(BF16)
        h_lo = (h2 - h_hi.astype(F32)).astype(BF16)
        logits.append(_dot(h_hi, wr_ref[0]) + (_dot(h_hi, wr_ref[1]) + _dot(h_lo, wr_ref[0])) + br_ref[...])

    lane_i = lax.broadcasted_iota(jnp.int32, (sub, LANES), 1)
    lane = lane_i.astype(F32)
    lgs = [jnp.where(lane_i < n_experts, lg, -jnp.inf) for lg in logits]
    vals = [[] for _ in parts]
    idxs = [[] for _ in parts]
    for _ in range(TOP_K):
        for p in range(len(parts)):
            m = lgs[p].max(axis=-1, keepdims=True)
            idx = jnp.where(lgs[p] == m, lane, float(LANES)).min(axis=-1, keepdims=True)
            vals[p].append(m)
            idxs[p].append(idx)
            lgs[p] = jnp.where(lane == idx, -jnp.inf, lgs[p])

    row = lax.broadcasted_iota(jnp.int32, (sub, sub), 0)
    col = lax.broadcasted_iota(jnp.int32, (sub, sub), 1)
    before = (col < row).astype(BF16)
    carry = carry_ref[0:1, :]
    for p, r in enumerate(parts):
        ex = [jnp.exp(v - vals[p][0]) for v in vals[p]]
        den = ex[0]
        for e in ex[1:]:
            den = den + e
        onehot = jnp.zeros((sub, LANES), F32)
        for idx in idxs[p]:
            onehot = onehot + (lane == idx).astype(F32)
        cum = _dot(before, onehot.astype(BF16)) + carry
        carry = carry + onehot.sum(axis=0, keepdims=True)
        route = jnp.zeros((sub, LANES), F32)
        gate = jnp.zeros((sub, LANES), F32)
        for k in range(TOP_K):
            rank = jnp.where(lane == idxs[p][k], cum, 0.0).sum(axis=-1, keepdims=True)
            route = jnp.where(lane_i == k, idxs[p][k], route)
            route = jnp.where(lane_i == TOP_K + k, rank, route)
            gate = jnp.where(lane_i == k, ex[k] / den, gate)
        route_ref[r, :] = route.astype(jnp.int32)
        gate_ref[r, :] = gate
    carry_ref[...] = jnp.broadcast_to(carry, carry_ref.shape)
    cnt_ref[...] = jnp.broadcast_to(carry, cnt_ref.shape)


def _outproj(o_a, o_b, x2, mod3, norm_w, w_out_a, w_out_b, w_router, b_router, seq):
    n, d = x2.shape
    n_experts = w_router.shape[1]
    tm = 512
    tiles_per_seq = seq // tm
    wr32 = jnp.zeros((d, LANES), F32).at[:, :n_experts].set(w_router)
    wr_hi = wr32.astype(BF16)
    wr = jnp.stack([wr_hi, (wr32 - wr_hi.astype(F32)).astype(BF16)])
    br = jnp.zeros((1, LANES), F32).at[0, :n_experts].set(b_router)
    kern = functools.partial(_outproj_kernel, n_experts=n_experts)
    aw = o_a.shape[1]
    bw = o_b.shape[1]
    return pl.pallas_call(
        kern,
        grid=(n // tm,),
        in_specs=[
            pl.BlockSpec((tm, aw), lambda i: (i, 0)),
            pl.BlockSpec((tm, bw), lambda i: (i, 0)),
            pl.BlockSpec((tm, d), lambda i: (i, 0)),
            pl.BlockSpec((1, 6, d), lambda i: (i // tiles_per_seq, 0, 0)),
            pl.BlockSpec((4, d), lambda i: (0, 0)),
            pl.BlockSpec((aw, d), lambda i: (0, 0)),
            pl.BlockSpec((bw, d), lambda i: (0, 0)),
            pl.BlockSpec((2, d, LANES), lambda i: (0, 0, 0)),
            pl.BlockSpec((1, LANES), lambda i: (0, 0)),
        ],
        out_specs=[
            pl.BlockSpec((tm, d), lambda i: (i, 0)),
            pl.BlockSpec((tm, d), lambda i: (i, 0)),
            pl.BlockSpec((tm, LANES), lambda i: (i, 0)),
            pl.BlockSpec((tm, LANES), lambda i: (i, 0)),
            pl.BlockSpec((8, LANES), lambda i: (0, 0)),
        ],
        out_shape=[
            jax.ShapeDtypeStruct((n, d), F32),
            jax.ShapeDtypeStruct((n, d), F32),
            jax.ShapeDtypeStruct((n, LANES), jnp.int32),
            jax.ShapeDtypeStruct((n, LANES), F32),
            jax.ShapeDtypeStruct((8, LANES), F32),
        ],
        scratch_shapes=[pltpu.VMEM((8, LANES), F32)],
        compiler_params=pltpu.CompilerParams(
            dimension_semantics=("arbitrary",), vmem_limit_bytes=48 * MIB),
        name="outproj_router",
    )(o_a, o_b, x2, mod3, norm_w, w_out_a, w_out_b, wr, br)


def _dispatch_kernel(pend_ref, vend_ref, nu_ref, dest_ref, h_ref, xs_hbm, stage, zeros, sem, zsem, tsem,
                     *, n_experts, blk):
    i = pl.program_id(0)
    n_tiles = pl.num_programs(0)
    tm, d = h_ref.shape
    slot = lax.rem(i, 2)
    piece = zeros.shape[0]
    shift = int(math.log2(piece))
    n_pieces = xs_hbm.shape[0] // piece

    def wait_rows(s):
        for _ in range(TOP_K):
            pltpu.make_async_copy(stage.at[s], xs_hbm.at[pl.ds(0, tm), :], sem.at[s]).wait()

    def clear(sem_ref):
        def body(p, carry):
            start = pl.multiple_of(p * piece, piece)
            pltpu.make_async_copy(zeros, xs_hbm.at[pl.ds(start, piece), :], sem_ref).start()
            return carry
        return body

    def wait_clear(sem_ref):
        def body(p, carry):
            pltpu.make_async_copy(zeros, xs_hbm.at[pl.ds(0, piece), :], sem_ref).wait()
            return carry
        return body

    @pl.when(i == 0)
    def _():
        zeros[...] = jnp.zeros_like(zeros)
        for e in range(n_experts):
            lax.fori_loop(vend_ref[e] >> shift, pend_ref[e] >> shift, clear(zsem), 0)
        lax.fori_loop((nu_ref[0] * blk) >> shift, n_pieces, clear(tsem), 0)
        for e in range(n_experts):
            lax.fori_loop(vend_ref[e] >> shift, pend_ref[e] >> shift, wait_clear(zsem), 0)

    @pl.when(i >= 2)
    def _():
        wait_rows(slot)

    stage[slot] = h_ref[...]

    for t in range(tm):
        for k in range(TOP_K):
            s = dest_ref[0, 0, t * TOP_K + k]
            pltpu.make_async_copy(stage.at[slot, pl.ds(t, 1), :], xs_hbm.at[pl.ds(s, 1), :],
                                  sem.at[slot]).start(priority=k % 2)

    @pl.when(i == n_tiles - 1)
    def _():
        wait_rows(slot)

        @pl.when(n_tiles > 1)
        def _():
            wait_rows(1 - slot)

        lax.fori_loop((nu_ref[0] * blk) >> shift, n_pieces, wait_clear(tsem), 0)


def _dispatch(h2, dest, pend, vend, n_used, cap):
    n, d = h2.shape
    n_experts = pend.shape[0]
    tm = 256
    n_tiles = n // tm
    kern = functools.partial(_dispatch_kernel, n_experts=n_experts, blk=MOE_BLOCK)
    grid_spec = pltpu.PrefetchScalarGridSpec(
        num_scalar_prefetch=3,
        grid=(n_tiles,),
        in_specs=[
            pl.BlockSpec((1, 1, tm * TOP_K), lambda i, pe, pa, nu: (i, 0, 0), memory_space=pltpu.SMEM),
            pl.BlockSpec((tm, d), lambda i, pe, pa, nu: (i, 0)),
        ],
        out_specs=pl.BlockSpec(memory_space=pl.ANY),
        scratch_shapes=[
            pltpu.VMEM((2, tm, d), F32),
            pltpu.VMEM((CLEAR_ROWS, d), F32),
            pltpu.SemaphoreType.DMA((2,)),
            pltpu.SemaphoreType.DMA(()),
            pltpu.SemaphoreType.DMA(()),
        ],
    )
    return pl.pallas_call(
        kern,
        grid_spec=grid_spec,
        out_shape=jax.ShapeDtypeStruct((cap, d), F32),
        compiler_params=pltpu.CompilerParams(
            dimension_semantics=("arbitrary",), vmem_limit_bytes=32 * MIB),
        name="dispatch",
    )(pend, vend, n_used, dest.reshape(n_tiles, 1, tm * TOP_K), h2)


def _expert_kernel(be_ref, nu_ref, nv_ref, nx_ref, xs_ref, wup_hbm, bup_ref, wdn_hbm, bdn_ref, y_ref,
                   up_stage, dn_stage, wsem, wup16, wdn16):
    j = pl.program_id(0)
    blk, d = xs_ref.shape
    f = wdn16.shape[0]
    n_valid = nv_ref[j]
    half = blk // 2

    def weight_copies(e):
        return (pltpu.make_async_copy(wup_hbm.at[e], up_stage, wsem.at[0]),
                pltpu.make_async_copy(wdn_hbm.at[e], dn_stage, wsem.at[1]))

    prev = be_ref[jnp.maximum(j - 1, 0)]
    first_of_expert = jnp.logical_and(n_valid > 0, jnp.logical_or(j == 0, be_ref[j] != prev))

    @pl.when(jnp.logical_and(j == 0, n_valid > 0))
    def _():
        for c in weight_copies(be_ref[0]):
            c.start()

    @pl.when(first_of_expert)
    def _():
        for c in weight_copies(be_ref[j]):
            c.wait()
        step = 128
        for r0 in range(0, d, step):
            wup16[r0:r0 + step, :] = up_stage[r0:r0 + step, :].astype(BF16)
        for r0 in range(0, f, step):
            wdn16[r0:r0 + step, :] = dn_stage[r0:r0 + step, :].astype(BF16)

        @pl.when(nx_ref[j] >= 0)
        def _():
            for c in weight_copies(nx_ref[j]):
                c.start()

    def mlp(rows):
        xb = xs_ref[0:rows, :].astype(BF16)
        fc = min(f, 512)
        acc = None
        for c0 in range(0, f, fc):
            glu = _dot(xb, wup16[:, c0:c0 + fc]) + bup_ref[0, :, c0:c0 + fc]
            lin = _dot(xb, wup16[:, f + c0:f + c0 + fc]) + bup_ref[0, :, f + c0:f + c0 + fc]
            glu = jnp.minimum(glu, SWIGLU_LIMIT)
            lin = jnp.clip(lin, -SWIGLU_LIMIT, SWIGLU_LIMIT)
            act = glu * jax.nn.sigmoid(SWIGLU_ALPHA * glu) * (lin + 1.0)
            part = _dot(act.astype(BF16), wdn16[c0:c0 + fc, :])
            acc = part if acc is None else acc + part
        y_ref[0:rows, :] = acc + bdn_ref[0]

    @pl.when(n_valid > half)
    def _():
        mlp(blk)

    @pl.when(jnp.logical_and(n_valid > 0, n_valid <= half))
    def _():
        mlp(half)
        y_ref[half:, :] = jnp.zeros((blk - half, d), y_ref.dtype)

    @pl.when(n_valid == 0)
    def _():
        y_ref[...] = jnp.zeros_like(y_ref)


def _experts(xs, block_e, n_used, n_valid, next_e, w_up, b_up, w_down, b_down):
    cap, d = xs.shape
    n_exp, _, f2 = w_up.shape
    f = w_down.shape[1]
    blk = MOE_BLOCK
    n_blocks = cap // blk
    grid_spec = pltpu.PrefetchScalarGridSpec(
        num_scalar_prefetch=4,
        grid=(n_blocks,),
        in_specs=[
            pl.BlockSpec((blk, d), lambda j, be, nu, nv, nx: (jnp.minimum(j, jnp.maximum(nu[0] - 1, 0)), 0)),
            pl.BlockSpec(memory_space=pl.ANY),
            pl.BlockSpec((1, 1, f2), lambda j, be, nu, nv, nx: (be[j], 0, 0)),
            pl.BlockSpec(memory_space=pl.ANY),
            pl.BlockSpec((1, 1, d), lambda j, be, nu, nv, nx: (be[j], 0, 0)),
        ],
        out_specs=pl.BlockSpec((blk, d), lambda j, be, nu, nv, nx: (j, 0)),
        scratch_shapes=[
            pltpu.VMEM((d, f2), F32),
            pltpu.VMEM((f, d), F32),
            pltpu.SemaphoreType.DMA((2,)),
            pltpu.VMEM((d, f2), BF16),
            pltpu.VMEM((f, d), BF16),
        ],
    )
    return pl.pallas_call(
        _expert_kernel,
        grid_spec=grid_spec,
        out_shape=jax.ShapeDtypeStruct((cap, d), F32),
        compiler_params=pltpu.CompilerParams(
            dimension_semantics=("arbitrary",), vmem_limit_bytes=56 * MIB),
        name="experts",
    )(block_e, n_used, n_valid, next_e, xs, w_up, b_up.reshape(n_exp, 1, f2), w_down, b_down.reshape(n_exp, 1, d))


def _combine_kernel(dest_ref, dest_next_ref, y_hbm, gate_ref, x1_ref, mod_ref, nw_ref, o_ref, buf, sem):
    i = pl.program_id(0)
    n_tiles = pl.num_programs(0)
    tm, d = x1_ref.shape
    slot = lax.rem(i, 2)

    def start_gather(idx_ref, dst_slot):
        for t in range(tm):
            for k in range(TOP_K):
                s = idx_ref[0, 0, t * TOP_K + k]
                pltpu.make_async_copy(y_hbm.at[pl.ds(s, 1), :], buf.at[dst_slot, k, pl.ds(t, 1), :],
                                      sem.at[dst_slot]).start(priority=k % 2)

    @pl.when(i == 0)
    def _():
        start_gather(dest_ref, 0)

    @pl.when(i + 1 < n_tiles)
    def _():
        start_gather(dest_next_ref, 1 - slot)

    for k in range(TOP_K):
        pltpu.make_async_copy(y_hbm.at[pl.ds(0, tm), :], buf.at[slot, k], sem.at[slot]).wait()

    gate = gate_ref[...]
    ysum = gate[:, 0:1] * buf[slot, 0]
    for k in range(1, TOP_K):
        ysum = ysum + gate[:, k:k + 1] * buf[slot, k]
    ga2 = mod_ref[0, 5:6, :]
    o_ref[...] = x1_ref[...] + ga2 * _rms(ysum, nw_ref[3:4, :])


def _combine(y_sorted, dest, gates, x1, mod3, norm_w, seq):
    n, d = x1.shape
    tm = 256
    n_tiles = n // tm
    tiles_per_seq = seq // tm
    dest3 = dest.reshape(n_tiles, 1, tm * TOP_K)
    return pl.pallas_call(
        _combine_kernel,
        grid=(n_tiles,),
        in_specs=[
            pl.BlockSpec((1, 1, tm * TOP_K), lambda i: (i, 0, 0), memory_space=pltpu.SMEM),
            pl.BlockSpec((1, 1, tm * TOP_K), lambda i: (jnp.minimum(i + 1, n_tiles - 1), 0, 0),
                         memory_space=pltpu.SMEM),
            pl.BlockSpec(memory_space=pl.ANY),
            pl.BlockSpec((tm, LANES), lambda i: (i, 0)),
            pl.BlockSpec((tm, d), lambda i: (i, 0)),
            pl.BlockSpec((1, 6, d), lambda i: (i // tiles_per_seq, 0, 0)),
            pl.BlockSpec((4, d), lambda i: (0, 0)),
        ],
        out_specs=pl.BlockSpec((tm, d), lambda i: (i, 0)),
        out_shape=jax.ShapeDtypeStruct((n, d), F32),
        scratch_shapes=[
            pltpu.VMEM((2, TOP_K, tm, d), F32),
            pltpu.SemaphoreType.DMA((2,)),
        ],
        compiler_params=pltpu.CompilerParams(
            dimension_semantics=("arbitrary",), vmem_limit_bytes=32 * MIB),
        name="combine",
    )(dest3, dest3, y_sorted, gates, x1, mod3, norm_w)


def _layer(x, c, w_ada, b_ada, norm_w, w_in, conv_w, a_log, dt_bias, a_norm_w, rel_bias, w_out,
           w_router, b_router, w_up, b_up, w_down, b_down):
    bsz, seq, d = x.shape
    n = bsz * seq
    a_width = conv_w.shape[1] // 3
    a_heads = a_log.shape[0]
    b_width = w_out.shape[0] - a_width
    n_experts = w_router.shape[1]
    off_gate = 4 * a_width
    off_b = off_gate + 2 * a_heads

    mod3 = _adaln(c, w_ada, b_ada).reshape(bsz, 6, d)
    x2 = x.reshape(n, d)

    w_main = jnp.concatenate([w_in[:, :off_gate], w_in[:, off_b:]], axis=1).astype(BF16)
    w_gate = jnp.zeros((d, LANES), F32).at[:, :2 * a_heads].set(w_in[:, off_gate:off_b]).astype(BF16)
    qkv_a, z_a, ab, qkv_b = _inproj(x2, mod3, norm_w, w_main, w_gate, conv_w, seq, 3 * a_width, a_width,
                                    3 * b_width)

    o_a = _gdn(qkv_a, z_a, ab, a_log, dt_bias, a_norm_w, bsz, seq)
    o_b = _band_attn(qkv_b, rel_bias, bsz, seq)

    w_out16 = w_out.astype(BF16)
    x1, h2, route, gates, cnt = _outproj(o_a, o_b, x2, mod3, norm_w, w_out16[:a_width], w_out16[a_width:],
                                         w_router, b_router, seq)

    top_idx = route[:, :TOP_K]
    rank = route[:, TOP_K:2 * TOP_K]
    counts = cnt[0, :n_experts].astype(jnp.int32)
    padded = (counts + MOE_BLOCK - 1) // MOE_BLOCK * MOE_BLOCK
    pend = jnp.cumsum(padded)
    pstart = pend - padded
    expert_ids = jnp.arange(n_experts, dtype=jnp.int32)
    dest = jnp.sum(jnp.where(top_idx[..., None] == expert_ids, pstart, 0), axis=-1) + rank
    n_assign = n * TOP_K
    n_blocks = -(-n_assign // MOE_BLOCK) + n_experts
    cap = n_blocks * MOE_BLOCK
    block_start = jnp.arange(n_blocks, dtype=jnp.int32) * MOE_BLOCK
    block_e = jnp.minimum(jnp.sum(pend[None, :] <= block_start[:, None], axis=1), n_experts - 1).astype(jnp.int32)
    n_used = (pend[-1:] // MOE_BLOCK).astype(jnp.int32)
    dest = dest.astype(jnp.int32)

    vend = (pstart + counts).astype(jnp.int32)
    block_vend = jnp.sum(jnp.where(block_e[:, None] == expert_ids, vend, 0), axis=1)
    n_valid = jnp.clip(block_vend - block_start, 0, MOE_BLOCK).astype(jnp.int32)

    xs = _dispatch(h2, dest, pend.astype(jnp.int32), vend, n_used, cap)
    blocks = jnp.arange(n_blocks, dtype=jnp.int32)
    later = (blocks[None, :] > blocks[:, None]) & (block_e[None, :] != block_e[:, None]) & (n_valid[None, :] > 0)
    first_later = jnp.min(jnp.where(later, blocks[None, :], n_blocks), axis=1)
    next_e = jnp.sum(jnp.where(blocks[None, :] == first_later[:, None], block_e[None, :] + 1, 0), axis=1) - 1
    y_sorted = _experts(xs, block_e, n_used, n_valid, next_e.astype(jnp.int32), w_up, b_up, w_down, b_down)
    out = _combine(y_sorted, dest, gates, x1, mod3, norm_w, seq)
    return out.reshape(bsz, seq, d)


def kernel(x, c, w_ada, b_ada, norm_w, w_in, conv_w, a_log, dt_bias, a_norm_w, rel_bias, w_out,
           w_router, b_router, w_up, b_up, w_down, b_down):
    for l in range(w_ada.shape[0]):
        x = _layer(x, c, w_ada[l], b_ada[l], norm_w[l], w_in[l], conv_w[l], a_log[l], dt_bias[l],
                   a_norm_w[l], rel_bias[l], w_out[l], w_router[l], b_router[l], w_up[l], b_up[l],
                   w_down[l], b_down[l])
    return x
```

```python
import functools
import math

import jax
import jax.numpy as jnp
from jax import lax
from jax.experimental import pallas as pl
from jax.experimental.pallas import tpu as pltpu

F32 = jnp.float32
BF16 = jnp.bfloat16
HIGHEST = lax.Precision.HIGHEST

EPS = 1e-6
CHUNK = 64
CONV_K = 4
A_HEAD_DIM = 128
B_HEAD_DIM = 64
B_PREV_CHUNKS = 8
REL_CLIP = 128
TOP_K = 4
SWIGLU_ALPHA = 1.702
SWIGLU_LIMIT = 7.0
MOE_BLOCK = 512
CLEAR_ROWS = 64
LANES = 128
NEG_BIG = -1e30

MIB = 1024 * 1024


def _dot(a, b):
    return jnp.dot(a, b, preferred_element_type=F32)


def _dot_nt(a, b):
    return lax.dot_general(a, b, (((1,), (1,)), ((), ())), preferred_element_type=F32)


def _dot_exact(a, b):
    return jnp.dot(a, b, precision=HIGHEST, preferred_element_type=F32)


def _rms(x, w):
    return x * lax.rsqrt(jnp.mean(x * x, axis=-1, keepdims=True) + EPS) * w


def _adaln_kernel(c_ref, w_ref, b_ref, o_ref):
    cs = c_ref[...]
    cs = cs * jax.nn.sigmoid(cs)
    o_ref[...] = _dot_exact(cs, w_ref[...]) + b_ref[...]


def _adaln(c, w, b):
    bsz, d = c.shape
    n_out = w.shape[1]
    rows = 8
    cp = jnp.zeros((rows, d), F32).at[:bsz].set(c)
    tn = 512
    out = pl.pallas_call(
        _adaln_kernel,
        grid=(n_out // tn,),
        in_specs=[
            pl.BlockSpec((rows, d), lambda j: (0, 0)),
            pl.BlockSpec((d, tn), lambda j: (0, j)),
            pl.BlockSpec((1, tn), lambda j: (0, j)),
        ],
        out_specs=pl.BlockSpec((rows, tn), lambda j: (0, j)),
        out_shape=jax.ShapeDtypeStruct((rows, n_out), F32),
        name="adaln",
    )(cp, w, b.reshape(1, n_out))
    return out[:bsz]


def _inproj_kernel(x_ref, mod_ref, nw_ref, wm_ref, wg_ref, cw_ref, qkva_ref, z_ref, ab_ref, qkvb_ref, raw_ref,
                   *, a_qkv, a_z, tiles_per_seq):
    i = pl.program_id(0)
    tm = x_ref.shape[0]
    dk = A_HEAD_DIM
    a_width = a_qkv // 3
    halo = 8
    tn = 512

    @pl.when(lax.rem(i, tiles_per_seq) == 0)
    def _():
        raw_ref[0:halo, :] = jnp.zeros((halo, a_qkv), F32)

    x = x_ref[...]
    sh = mod_ref[0, 0:1, :]
    sc = mod_ref[0, 1:2, :]
    h = _rms(x, nw_ref[0:1, :]) * (1.0 + sc) + sh
    hb = h.astype(BF16)

    def conv_silu(c0):
        acc = cw_ref[CONV_K - 1:CONV_K, c0:c0 + dk] * raw_ref[halo:halo + tm, c0:c0 + dk]
        for j in range(CONV_K - 1):
            start = halo - (CONV_K - 1) + j
            acc = acc + cw_ref[j:j + 1, c0:c0 + dk] * raw_ref[start:start + tm, c0:c0 + dk]
        y = acc * jax.nn.sigmoid(acc)
        if c0 < 2 * a_width:
            y = y * lax.rsqrt(jnp.sum(y * y, axis=-1, keepdims=True) + EPS)
        if c0 < a_width:
            y = y * (dk ** -0.5)
        qkva_ref[:, c0:c0 + dk] = y

    for c0 in range(0, a_qkv, tn):
        raw_ref[halo:halo + tm, c0:c0 + tn] = _dot(hb, wm_ref[:, c0:c0 + tn])
    slabs = list(range(0, a_qkv, dk))
    other = [("z", c0) for c0 in range(0, a_z, tn)] + [("b", c0) for c0 in range(0, qkvb_ref.shape[1], tn)]
    per = -(-len(slabs) // len(other))
    off = a_qkv + a_z
    for n_o, (kind, c0) in enumerate(other):
        for c in slabs[n_o * per:(n_o + 1) * per]:
            conv_silu(c)
        if kind == "z":
            z_ref[:, c0:c0 + tn] = _dot(hb, wm_ref[:, a_qkv + c0:a_qkv + c0 + tn])
        else:
            qkvb_ref[:, c0:c0 + tn] = _dot(hb, wm_ref[:, off + c0:off + c0 + tn]).astype(BF16)
    ab_ref[...] = _dot(hb, wg_ref[...])
    raw_ref[0:halo, :] = raw_ref[tm:tm + halo, :]


def _inproj(x2, mod3, norm_w, w_main, w_gate, conv_w, seq, a_qkv, a_z, b_qkv):
    n, d = x2.shape
    tm = 512
    tiles_per_seq = seq // tm
    kern = functools.partial(_inproj_kernel, a_qkv=a_qkv, a_z=a_z, tiles_per_seq=tiles_per_seq)
    return pl.pallas_call(
        kern,
        grid=(n // tm,),
        in_specs=[
            pl.BlockSpec((tm, d), lambda i: (i, 0)),
            pl.BlockSpec((1, 6, d), lambda i: (i // tiles_per_seq, 0, 0)),
            pl.BlockSpec((4, d), lambda i: (0, 0)),
            pl.BlockSpec(w_main.shape, lambda i: (0, 0)),
            pl.BlockSpec(w_gate.shape, lambda i: (0, 0)),
            pl.BlockSpec(conv_w.shape, lambda i: (0, 0)),
        ],
        out_specs=[
            pl.BlockSpec((tm, a_qkv), lambda i: (i, 0)),
            pl.BlockSpec((tm, a_z), lambda i: (i, 0)),
            pl.BlockSpec((tm, LANES), lambda i: (i, 0)),
            pl.BlockSpec((tm, b_qkv), lambda i: (i, 0)),
        ],
        out_shape=[
            jax.ShapeDtypeStruct((n, a_qkv), F32),
            jax.ShapeDtypeStruct((n, a_z), F32),
            jax.ShapeDtypeStruct((n, LANES), F32),
            jax.ShapeDtypeStruct((n, b_qkv), BF16),
        ],
        scratch_shapes=[pltpu.VMEM((tm + 8, a_qkv), F32)],
        compiler_params=pltpu.CompilerParams(
            dimension_semantics=("arbitrary",), vmem_limit_bytes=48 * MIB),
        name="inproj",
    )(x2, mod3, norm_w, w_main, w_gate, conv_w)


def _gdn_kernel(qkv_ref, z_ref, ab_ref, alog_ref, dtb_ref, anw_ref, o_ref, state_ref, *, n_heads):
    i = pl.program_id(1)
    tt = qkv_ref.shape[0]
    width = qkv_ref.shape[1]
    a_width = width // 3
    dk = A_HEAD_DIM
    sub = 2 * CHUNK
    per_sub = sub // CHUNK

    @pl.when(i == 0)
    def _():
        state_ref[...] = jnp.zeros_like(state_ref)

    row = lax.broadcasted_iota(jnp.int32, (sub, sub), 0)
    col = lax.broadcasted_iota(jnp.int32, (sub, sub), 1)
    shift = int(math.log2(CHUNK))
    same = (row >> shift) == (col >> shift)
    tri_incl = same & (col <= row)
    tri_strict = same & (col < row)
    eye = (row == col).astype(F32)
    pair = (row >> 1) == (col >> 1)
    couples = [((row >> (lv + 1)) == (col >> (lv + 1))) & ((row >> lv) != (col >> lv))
               for lv in range(1, shift)]
    col_chunk = lax.broadcasted_iota(jnp.int32, (dk, sub), 1) >> shift
    tri_incl_f = tri_incl.astype(F32)
    same_f = same.astype(F32)

    n_sub = tt // sub
    gates = []
    for p in range(n_sub):
        r0 = p * sub
        ab = ab_ref[r0:r0 + sub, :]
        gfull = -jnp.exp(alog_ref[...]) * jax.nn.softplus(ab + dtb_ref[...])
        bfull = jax.nn.sigmoid(ab)
        gcum = _dot_exact(tri_incl_f, gfull)
        gtot = _dot_exact(same_f, gfull)
        gates.append((gcum, gcum.T, gtot, bfull))

    items = [(p, h) for p in range(n_sub) for h in range(n_heads)]
    pre = {}
    for p, h in items:
        r0 = p * sub
        gcum, gcum_t, gtot, bfull = gates[p]
        gc = gcum[:, h:h + 1]
        beta = bfull[:, n_heads + h:n_heads + h + 1]
        q = qkv_ref[r0:r0 + sub, h * dk:(h + 1) * dk]
        k = qkv_ref[r0:r0 + sub, a_width + h * dk:a_width + (h + 1) * dk]
        v = qkv_ref[r0:r0 + sub, 2 * a_width + h * dk:2 * a_width + (h + 1) * dk]
        decay = jnp.exp(jnp.where(tri_incl, gc - gcum_t[h:h + 1, :], -jnp.inf))
        kb = k * beta
        k16 = k.astype(BF16)
        lower = jnp.where(tri_strict, _dot_nt(kb.astype(BF16), k16) * decay, 0.0)
        attn16 = (_dot_nt(q.astype(BF16), k16) * decay).astype(BF16)
        eg = jnp.exp(gc)
        rhs = jnp.concatenate([kb * eg, v * beta], axis=1).astype(BF16)
        kd_t = (k * jnp.exp(gtot[:, h:h + 1] - gc)).T
        kd_st = jnp.concatenate([jnp.where(col_chunk == c, kd_t, 0.0) for c in range(per_sub)],
                                axis=0).astype(BF16)
        pre[p, h] = dict(lower=lower, attn16=attn16, rhs=rhs, kd_st=kd_st, qeg=q * eg)

    tmat = {it: eye - jnp.where(pair, pre[it]["lower"], 0.0) for it in items}
    for couple in couples:
        t16 = {it: tmat[it].astype(BF16) for it in items}
        half = {it: _dot(t16[it], jnp.where(couple, pre[it]["lower"], 0.0).astype(BF16)) for it in items}
        tmat = {it: tmat[it] - _dot(half[it].astype(BF16), t16[it]) for it in items}

    wu16 = {it: _dot(tmat[it].astype(BF16), pre[it]["rhs"]).astype(BF16) for it in items}
    a_wu = {it: _dot(pre[it]["attn16"], wu16[it]) for it in items}
    k_wu = {it: _dot(pre[it]["kd_st"], wu16[it]) for it in items}
    qp16 = {it: (pre[it]["qeg"] - a_wu[it][:, :dk]).astype(BF16) for it in items}

    state = [state_ref[h] for h in range(n_heads)]
    for p in range(n_sub):
        r0 = p * sub
        gtot = gates[p][2]
        outs = [[] for _ in range(n_heads)]
        for c in range(per_sub):
            c0 = c * CHUNK
            for h in range(n_heads):
                kw = k_wu[p, h]
                m16 = kw[c * dk:(c + 1) * dk, :dk].astype(BF16)
                res = _dot(jnp.concatenate([m16, qp16[p, h][c0:c0 + CHUNK]], axis=0), state[h].astype(BF16))
                outs[h].append(res[dk:] + a_wu[p, h][c0:c0 + CHUNK, dk:])
                state[h] = (state[h] * jnp.exp(gtot[c0:c0 + 1, h:h + 1]) - res[:dk]
                            + kw[c * dk:(c + 1) * dk, dk:])
        for h in range(n_heads):
            o = jnp.concatenate(outs[h], axis=0)
            zh = z_ref[r0:r0 + sub, h * dk:(h + 1) * dk]
            o_ref[r0:r0 + sub, h * dk:(h + 1) * dk] = (
                _rms(o, anw_ref[...]) * (zh * jax.nn.sigmoid(zh))).astype(o_ref.dtype)
    for h in range(n_heads):
        state_ref[h] = state[h]


def _gdn(qkv_a, z, ab, a_log, dt_bias, a_norm_w, bsz, seq):
    n, width = qkv_a.shape
    a_width = width // 3
    n_heads = a_width // A_HEAD_DIM
    tt = 512
    nt = seq // tt
    alog = jnp.zeros((1, LANES), F32).at[0, :n_heads].set(a_log)
    dtb = jnp.zeros((1, LANES), F32).at[0, :n_heads].set(dt_bias)
    kern = functools.partial(_gdn_kernel, n_heads=n_heads)
    return pl.pallas_call(
        kern,
        grid=(bsz, nt),
        in_specs=[
            pl.BlockSpec((tt, width), lambda b, i: (b * nt + i, 0)),
            pl.BlockSpec((tt, a_width), lambda b, i: (b * nt + i, 0)),
            pl.BlockSpec((tt, LANES), lambda b, i: (b * nt + i, 0)),
            pl.BlockSpec((1, LANES), lambda b, i: (0, 0)),
            pl.BlockSpec((1, LANES), lambda b, i: (0, 0)),
            pl.BlockSpec((1, A_HEAD_DIM), lambda b, i: (0, 0)),
        ],
        out_specs=pl.BlockSpec((tt, a_width), lambda b, i: (b * nt + i, 0)),
        out_shape=jax.ShapeDtypeStruct((n, a_width), BF16),
        scratch_shapes=[pltpu.VMEM((n_heads, A_HEAD_DIM, A_HEAD_DIM), F32)],
        compiler_params=pltpu.CompilerParams(
            dimension_semantics=("arbitrary", "arbitrary"), vmem_limit_bytes=48 * MIB),
        name="gdn",
    )(qkv_a, z, ab, alog, dtb, a_norm_w.reshape(1, A_HEAD_DIM))


def _attn_kernel(q_ref, *refs, n_heads, n_kb, n_q):
    n_win = n_q + n_kb - 1
    k_refs, v_refs = refs[:n_win], refs[n_win:2 * n_win]
    bias_ref, o_ref = refs[2 * n_win], refs[2 * n_win + 1]
    i = pl.program_id(1)
    tq = q_ref.shape[0] // n_q
    hd = B_HEAD_DIM
    per = LANES // hd
    lane = lax.broadcasted_iota(jnp.int32, (1, LANES), 1)
    items = [(t, h) for t in range(n_q) for h in range(n_heads)]

    def lanes_of(h):
        return slice((h // per) * LANES, (h // per + 1) * LANES)

    def in_head(h):
        hh = h % per
        return (lane >= hh * hd) & (lane < (hh + 1) * hd)

    scores = {}
    for t, h in items:
        qg = q_ref[t * tq:(t + 1) * tq, lanes_of(h)]
        qh = jnp.where(in_head(h), qg, jnp.zeros_like(qg)) * (hd ** -0.5)
        s = []
        for j in range(n_kb):
            sj = _dot_nt(qh, k_refs[t + j][:, lanes_of(h)]) + bias_ref[h, :, j * tq:(j + 1) * tq]
            if j < n_kb - 1:
                sj = jnp.where(i * n_q + t >= n_kb - 1 - j, sj, NEG_BIG)
            s.append(sj)
        scores[t, h] = s
    probs, dens = {}, {}
    for it in items:
        s = scores[it]
        top = s[0]
        for sj in s[1:]:
            top = jnp.maximum(top, sj)
        m = top.max(axis=-1, keepdims=True)
        p = [jnp.exp(sj - m) for sj in s]
        tot = p[0]
        for pj in p[1:]:
            tot = tot + pj
        probs[it] = [pj.astype(BF16) for pj in p]
        dens[it] = tot.sum(axis=-1, keepdims=True)
    outs = {}
    for t, h in items:
        acc = _dot(probs[t, h][0], v_refs[t][:, lanes_of(h)])
        for j in range(1, n_kb):
            acc = acc + _dot(probs[t, h][j], v_refs[t + j][:, lanes_of(h)])
        outs[t, h] = acc / dens[t, h]
    for t in range(n_q):
        for g in range(n_heads // per):
            out = outs[t, g * per]
            for hh in range(1, per):
                out = jnp.where(in_head(g * per + hh), outs[t, g * per + hh], out)
            o_ref[t * tq:(t + 1) * tq, g * LANES:(g + 1) * LANES] = out.astype(o_ref.dtype)


def _band_bias(rel_bias, tq, n_kb):
    n_h = rel_bias.shape[0]
    back = (n_kb - 1) * tq
    nk = n_kb * tq
    span = tq + nk - 1
    lo = (nk - 1 - back) - REL_CLIP
    hi = span - lo - (2 * REL_CLIP + 1)
    by_offset = jnp.concatenate([jnp.broadcast_to(rel_bias[:, :1], (n_h, lo)), rel_bias,
                                 jnp.broadcast_to(rel_bias[:, -1:], (n_h, hi))], axis=1).astype(F32)
    rev = jnp.pad(by_offset[:, ::-1], ((0, 0), (0, 1)))
    skew = jnp.broadcast_to(rev[:, None, :], (n_h, tq, span + 1)).reshape(n_h, tq * (span + 1))
    skew = skew[:, :tq * span].reshape(n_h, tq, span)
    table = skew[:, :, tq - 1:tq - 1 + nk]
    qc = jnp.arange(tq)[:, None] // CHUNK
    kc = jnp.arange(nk)[None, :] // CHUNK - back // CHUNK
    allowed = (kc <= qc) & (kc >= qc - B_PREV_CHUNKS)
    return jnp.where(allowed[None], table, NEG_BIG)


def _band_attn(qkv_b, rel_bias, bsz, seq):
    n, width = qkv_b.shape
    b_width = width // 3
    n_heads = b_width // B_HEAD_DIM
    tq = 256
    n_kb = 3
    n_q = 2
    assert (n_kb - 1) * tq == B_PREV_CHUNKS * CHUNK
    nt = seq // tq
    steps = nt // n_q
    n_win = n_q + n_kb - 1
    bias = _band_bias(rel_bias, tq, n_kb)
    kern = functools.partial(_attn_kernel, n_heads=n_heads, n_kb=n_kb, n_q=n_q)

    def kv_spec(colblk, w):
        return pl.BlockSpec(
            (tq, b_width), lambda b, i: (b * nt + jnp.maximum(i * n_q - (n_kb - 1) + w, 0), colblk))

    return pl.pallas_call(
        kern,
        grid=(bsz, steps),
        in_specs=[pl.BlockSpec((n_q * tq, b_width), lambda b, i: (b * steps + i, 0))]
        + [kv_spec(1, w) for w in range(n_win)] + [kv_spec(2, w) for w in range(n_win)]
        + [pl.BlockSpec(bias.shape, lambda b, i: (0, 0, 0))],
        out_specs=pl.BlockSpec((n_q * tq, b_width), lambda b, i: (b * steps + i, 0)),
        out_shape=jax.ShapeDtypeStruct((n, b_width), BF16),
        compiler_params=pltpu.CompilerParams(
            dimension_semantics=("parallel", "parallel"), vmem_limit_bytes=48 * MIB),
        name="band_attn",
    )(*([qkv_b] * (1 + 2 * n_win)), bias)


def _outproj_kernel(oa_ref, ob_ref, x_ref, mod_ref, nw_ref, woa_ref, wob_ref, wr_ref, br_ref,
                    x1_ref, h2_ref, route_ref, gate_ref, cnt_ref, carry_ref, *, n_experts):
    i = pl.program_id(0)
    tm = x_ref.shape[0]
    sub = 256
    parts = [slice(r0, r0 + sub) for r0 in range(0, tm, sub)]

    @pl.when(i == 0)
    def _():
        carry_ref[...] = jnp.zeros_like(carry_ref)

    ga1 = mod_ref[0, 2:3, :]
    sh2 = mod_ref[0, 3:4, :]
    sc2 = mod_ref[0, 4:5, :]
    ys = [_dot(oa_ref[r, :], woa_ref[...]) + _dot(ob_ref[r, :], wob_ref[...]) for r in parts]
    h2s = []
    for r, y in zip(parts, ys):
        x1 = x_ref[r, :] + ga1 * _rms(y, nw_ref[1:2, :])
        x1_ref[r, :] = x1
        h2 = _rms(x1, nw_ref[2:3, :]) * (1.0 + sc2) + sh2
        h2_ref[r, :] = h2
        h2s.append(h2)

    logits = []
    for h2 in h2s:
        h_hi = h2.astype(BF16)
        h_lo = (h2 - h_hi.astype(F32)).astype(BF16)
        logits.append(_dot(h_hi, wr_ref[0]) + (_dot(h_hi, wr_ref[1]) + _dot(h_lo, wr_ref[0])) + br_ref[...])

    lane_i = lax.broadcasted_iota(jnp.int32, (sub, LANES), 1)
    lane = lane_i.astype(F32)
    lgs = [jnp.where(lane_i < n_experts, lg, -jnp.inf) for lg in logits]
    vals = [[] for _ in parts]
    idxs = [[] for _ in parts]
    for _ in range(TOP_K):
        for p in range(len(parts)):
            m = lgs[p].max(axis=-1, keepdims=True)
            idx = jnp.where(lgs[p] == m, lane, float(LANES)).min(axis=-1, keepdims=True)
            vals[p].append(m)
            idxs[p].append(idx)
            lgs[p] = jnp.where(lane == idx, -jnp.inf, lgs[p])

    row = lax.broadcasted_iota(jnp.int32, (sub, sub), 0)
    col = lax.broadcasted_iota(jnp.int32, (sub, sub), 1)
    before = (col < row).astype(BF16)
    carry = carry_ref[0:1, :]
    for p, r in enumerate(parts):
        ex = [jnp.exp(v - vals[p][0]) for v in vals[p]]
        den = ex[0]
        for e in ex[1:]:
            den = den + e
        onehot = jnp.zeros((sub, LANES), F32)
        for idx in idxs[p]:
            onehot = onehot + (lane == idx).astype(F32)
        cum = _dot(before, onehot.astype(BF16)) + carry
        carry = carry + onehot.sum(axis=0, keepdims=True)
        route = jnp.zeros((sub, LANES), F32)
        gate = jnp.zeros((sub, LANES), F32)
        for k in range(TOP_K):
            rank = jnp.where(lane == idxs[p][k], cum, 0.0).sum(axis=-1, keepdims=True)
            route = jnp.where(lane_i == k, idxs[p][k], route)
            route = jnp.where(lane_i == TOP_K + k, rank, route)
            gate = jnp.where(lane_i == k, ex[k] / den, gate)
        route_ref[r, :] = route.astype(jnp.int32)
        gate_ref[r, :] = gate
    carry_ref[...] = jnp.broadcast_to(carry, carry_ref.shape)
    cnt_ref[...] = jnp.broadcast_to(carry, cnt_ref.shape)


def _outproj(o_a, o_b, x2, mod3, norm_w, w_out_a, w_out_b, w_router, b_router, seq):
    n, d = x2.shape
    n_experts = w_router.shape[1]
    tm = 1024
    tiles_per_seq = seq // tm
    wr32 = jnp.zeros((d, LANES), F32).at[:, :n_experts].set(w_router)
    wr_hi = wr32.astype(BF16)
    wr = jnp.stack([wr_hi, (wr32 - wr_hi.astype(F32)).astype(BF16)])
    br = jnp.zeros((1, LANES), F32).at[0, :n_experts].set(b_router)
    kern = functools.partial(_outproj_kernel, n_experts=n_experts)
    aw = o_a.shape[1]
    bw = o_b.shape[1]
    return pl.pallas_call(
        kern,
        grid=(n // tm,),
        in_specs=[
            pl.BlockSpec((tm, aw), lambda i: (i, 0)),
            pl.BlockSpec((tm, bw), lambda i: (i, 0)),
            pl.BlockSpec((tm, d), lambda i: (i, 0)),
            pl.BlockSpec((1, 6, d), lambda i: (i // tiles_per_seq, 0, 0)),
            pl.BlockSpec((4, d), lambda i: (0, 0)),
            pl.BlockSpec((aw, d), lambda i: (0, 0)),
            pl.BlockSpec((bw, d), lambda i: (0, 0)),
            pl.BlockSpec((2, d, LANES), lambda i: (0, 0, 0)),
            pl.BlockSpec((1, LANES), lambda i: (0, 0)),
        ],
        out_specs=[
            pl.BlockSpec((tm, d), lambda i: (i, 0)),
            pl.BlockSpec((tm, d), lambda i: (i, 0)),
            pl.BlockSpec((tm, LANES), lambda i: (i, 0)),
            pl.BlockSpec((tm, LANES), lambda i: (i, 0)),
            pl.BlockSpec((8, LANES), lambda i: (0, 0)),
        ],
        out_shape=[
            jax.ShapeDtypeStruct((n, d), F32),
            jax.ShapeDtypeStruct((n, d), F32),
            jax.ShapeDtypeStruct((n, LANES), jnp.int32),
            jax.ShapeDtypeStruct((n, LANES), F32),
            jax.ShapeDtypeStruct((8, LANES), F32),
        ],
        scratch_shapes=[pltpu.VMEM((8, LANES), F32)],
        compiler_params=pltpu.CompilerParams(
            dimension_semantics=("arbitrary",), vmem_limit_bytes=48 * MIB),
        name="outproj_router",
    )(o_a, o_b, x2, mod3, norm_w, w_out_a, w_out_b, wr, br)


def _dispatch_kernel(pend_ref, vend_ref, nu_ref, dest_ref, h_ref, xs_hbm, stage, zeros, sem, zsem, tsem,
                     *, n_experts, blk):
    i = pl.program_id(0)
    n_tiles = pl.num_programs(0)
    tm, d = h_ref.shape
    slot = lax.rem(i, 2)
    piece = zeros.shape[0]
    shift = int(math.log2(piece))
    n_pieces = xs_hbm.shape[0] // piece

    def wait_rows(s):
        for _ in range(TOP_K):
            pltpu.make_async_copy(stage.at[s], xs_hbm.at[pl.ds(0, tm), :], sem.at[s]).wait()

    def clear(sem_ref):
        def body(p, carry):
            start = pl.multiple_of(p * piece, piece)
            pltpu.make_async_copy(zeros, xs_hbm.at[pl.ds(start, piece), :], sem_ref).start()
            return carry
        return body

    def wait_clear(sem_ref):
        def body(p, carry):
            pltpu.make_async_copy(zeros, xs_hbm.at[pl.ds(0, piece), :], sem_ref).wait()
            return carry
        return body

    @pl.when(i == 0)
    def _():
        zeros[...] = jnp.zeros_like(zeros)
        for e in range(n_experts):
            lax.fori_loop(vend_ref[e] >> shift, pend_ref[e] >> shift, clear(zsem), 0)
        lax.fori_loop((nu_ref[0] * blk) >> shift, n_pieces, clear(tsem), 0)
        for e in range(n_experts):
            lax.fori_loop(vend_ref[e] >> shift, pend_ref[e] >> shift, wait_clear(zsem), 0)

    @pl.when(i >= 2)
    def _():
        wait_rows(slot)

    stage[slot] = h_ref[...]

    for t in range(tm):
        for k in range(TOP_K):
            s = dest_ref[0, 0, t * TOP_K + k]
            pltpu.make_async_copy(stage.at[slot, pl.ds(t, 1), :], xs_hbm.at[pl.ds(s, 1), :],
                                  sem.at[slot]).start(priority=k % 2)

    @pl.when(i == n_tiles - 1)
    def _():
        wait_rows(slot)

        @pl.when(n_tiles > 1)
        def _():
            wait_rows(1 - slot)

        lax.fori_loop((nu_ref[0] * blk) >> shift, n_pieces, wait_clear(tsem), 0)


def _dispatch(h2, dest, pend, vend, n_used, cap):
    n, d = h2.shape
    n_experts = pend.shape[0]
    tm = 256
    n_tiles = n // tm
    kern = functools.partial(_dispatch_kernel, n_experts=n_experts, blk=MOE_BLOCK)
    grid_spec = pltpu.PrefetchScalarGridSpec(
        num_scalar_prefetch=3,
        grid=(n_tiles,),
        in_specs=[
            pl.BlockSpec((1, 1, tm * TOP_K), lambda i, pe, pa, nu: (i, 0, 0), memory_space=pltpu.SMEM),
            pl.BlockSpec((tm, d), lambda i, pe, pa, nu: (i, 0)),
        ],
        out_specs=pl.BlockSpec(memory_space=pl.ANY),
        scratch_shapes=[
            pltpu.VMEM((2, tm, d), F32),
            pltpu.VMEM((CLEAR_ROWS, d), F32),
            pltpu.SemaphoreType.DMA((2,)),
            pltpu.SemaphoreType.DMA(()),
            pltpu.SemaphoreType.DMA(()),
        ],
    )
    return pl.pallas_call(
        kern,
        grid_spec=grid_spec,
        out_shape=jax.ShapeDtypeStruct((cap, d), F32),
        compiler_params=pltpu.CompilerParams(
            dimension_semantics=("arbitrary",), vmem_limit_bytes=32 * MIB),
        name="dispatch",
    )(pend, vend, n_used, dest.reshape(n_tiles, 1, tm * TOP_K), h2)


def _expert_kernel(be_ref, nu_ref, nv_ref, nx_ref, xs_ref, wup_hbm, bup_ref, wdn_hbm, bdn_ref, y_ref,
                   up_stage, dn_stage, wsem, wup16, wdn16):
    j = pl.program_id(0)
    blk, d = xs_ref.shape
    f = wdn16.shape[0]
    n_valid = nv_ref[j]
    half = blk // 2

    def weight_copies(e):
        return (pltpu.make_async_copy(wup_hbm.at[e], up_stage, wsem.at[0]),
                pltpu.make_async_copy(wdn_hbm.at[e], dn_stage, wsem.at[1]))

    prev = be_ref[jnp.maximum(j - 1, 0)]
    first_of_expert = jnp.logical_and(n_valid > 0, jnp.logical_or(j == 0, be_ref[j] != prev))

    @pl.when(jnp.logical_and(j == 0, n_valid > 0))
    def _():
        for c in weight_copies(be_ref[0]):
            c.start()

    @pl.when(first_of_expert)
    def _():
        for c in weight_copies(be_ref[j]):
            c.wait()
        step = 128
        for r0 in range(0, d, step):
            wup16[r0:r0 + step, :] = up_stage[r0:r0 + step, :].astype(BF16)
        for r0 in range(0, f, step):
            wdn16[r0:r0 + step, :] = dn_stage[r0:r0 + step, :].astype(BF16)

        @pl.when(nx_ref[j] >= 0)
        def _():
            for c in weight_copies(nx_ref[j]):
                c.start()

    def mlp(rows):
        xb = xs_ref[0:rows, :].astype(BF16)
        fc = min(f, 512)
        acc = None
        for c0 in range(0, f, fc):
            glu = _dot(xb, wup16[:, c0:c0 + fc]) + bup_ref[0, :, c0:c0 + fc]
            lin = _dot(xb, wup16[:, f + c0:f + c0 + fc]) + bup_ref[0, :, f + c0:f + c0 + fc]
            glu = jnp.minimum(glu, SWIGLU_LIMIT)
            lin = jnp.clip(lin, -SWIGLU_LIMIT, SWIGLU_LIMIT)
            act = glu * jax.nn.sigmoid(SWIGLU_ALPHA * glu) * (lin + 1.0)
            part = _dot(act.astype(BF16), wdn16[c0:c0 + fc, :])
            acc = part if acc is None else acc + part
        y_ref[0:rows, :] = acc + bdn_ref[0]

    @pl.when(n_valid > half)
    def _():
        mlp(blk)

    @pl.when(jnp.logical_and(n_valid > 0, n_valid <= half))
    def _():
        mlp(half)
        y_ref[half:, :] = jnp.zeros((blk - half, d), y_ref.dtype)

    @pl.when(n_valid == 0)
    def _():
        y_ref[...] = jnp.zeros_like(y_ref)


def _experts(xs, block_e, n_used, n_valid, next_e, w_up, b_up, w_down, b_down):
    cap, d = xs.shape
    n_exp, _, f2 = w_up.shape
    f = w_down.shape[1]
    blk = MOE_BLOCK
    n_blocks = cap // blk
    grid_spec = pltpu.PrefetchScalarGridSpec(
        num_scalar_prefetch=4,
        grid=(n_blocks,),
        in_specs=[
            pl.BlockSpec((blk, d), lambda j, be, nu, nv, nx: (jnp.minimum(j, jnp.maximum(nu[0] - 1, 0)), 0)),
            pl.BlockSpec(memory_space=pl.ANY),
            pl.BlockSpec((1, 1, f2), lambda j, be, nu, nv, nx: (be[j], 0, 0)),
            pl.BlockSpec(memory_space=pl.ANY),
            pl.BlockSpec((1, 1, d), lambda j, be, nu, nv, nx: (be[j], 0, 0)),
        ],
        out_specs=pl.BlockSpec((blk, d), lambda j, be, nu, nv, nx: (j, 0)),
        scratch_shapes=[
            pltpu.VMEM((d, f2), F32),
            pltpu.VMEM((f, d), F32),
            pltpu.SemaphoreType.DMA((2,)),
            pltpu.VMEM((d, f2), BF16),
            pltpu.VMEM((f, d), BF16),
        ],
    )
    return pl.pallas_call(
        _expert_kernel,
        grid_spec=grid_spec,
        out_shape=jax.ShapeDtypeStruct((cap, d), F32),
        compiler_params=pltpu.CompilerParams(
            dimension_semantics=("arbitrary",), vmem_limit_bytes=56 * MIB),
        name="experts",
    )(block_e, n_used, n_valid, next_e, xs, w_up, b_up.reshape(n_exp, 1, f2), w_down, b_down.reshape(n_exp, 1, d))


def _combine_kernel(dest_ref, dest_next_ref, y_hbm, gate_ref, x1_ref, mod_ref, nw_ref, o_ref, buf, sem):
    i = pl.program_id(0)
    n_tiles = pl.num_programs(0)
    tm, d = x1_ref.shape
    slot = lax.rem(i, 2)

    def start_gather(idx_ref, dst_slot):
        for t in range(tm):
            for k in range(TOP_K):
                s = idx_ref[0, 0, t * TOP_K + k]
                pltpu.make_async_copy(y_hbm.at[pl.ds(s, 1), :], buf.at[dst_slot, k, pl.ds(t, 1), :],
                                      sem.at[dst_slot]).start(priority=k % 2)

    @pl.when(i == 0)
    def _():
        start_gather(dest_ref, 0)

    @pl.when(i + 1 < n_tiles)
    def _():
        start_gather(dest_next_ref, 1 - slot)

    for k in range(TOP_K):
        pltpu.make_async_copy(y_hbm.at[pl.ds(0, tm), :], buf.at[slot, k], sem.at[slot]).wait()

    gate = gate_ref[...]
    ysum = gate[:, 0:1] * buf[slot, 0]
    for k in range(1, TOP_K):
        ysum = ysum + gate[:, k:k + 1] * buf[slot, k]
    ga2 = mod_ref[0, 5:6, :]
    o_ref[...] = x1_ref[...] + ga2 * _rms(ysum, nw_ref[3:4, :])


def _combine(y_sorted, dest, gates, x1, mod3, norm_w, seq):
    n, d = x1.shape
    tm = 256
    n_tiles = n // tm
    tiles_per_seq = seq // tm
    dest3 = dest.reshape(n_tiles, 1, tm * TOP_K)
    return pl.pallas_call(
        _combine_kernel,
        grid=(n_tiles,),
        in_specs=[
            pl.BlockSpec((1, 1, tm * TOP_K), lambda i: (i, 0, 0), memory_space=pltpu.SMEM),
            pl.BlockSpec((1, 1, tm * TOP_K), lambda i: (jnp.minimum(i + 1, n_tiles - 1), 0, 0),
                         memory_space=pltpu.SMEM),
            pl.BlockSpec(memory_space=pl.ANY),
            pl.BlockSpec((tm, LANES), lambda i: (i, 0)),
            pl.BlockSpec((tm, d), lambda i: (i, 0)),
            pl.BlockSpec((1, 6, d), lambda i: (i // tiles_per_seq, 0, 0)),
            pl.BlockSpec((4, d), lambda i: (0, 0)),
        ],
        out_specs=pl.BlockSpec((tm, d), lambda i: (i, 0)),
        out_shape=jax.ShapeDtypeStruct((n, d), F32),
        scratch_shapes=[
            pltpu.VMEM((2, TOP_K, tm, d), F32),
            pltpu.SemaphoreType.DMA((2,)),
        ],
        compiler_params=pltpu.CompilerParams(
            dimension_semantics=("arbitrary",), vmem_limit_bytes=32 * MIB),
        name="combine",
    )(dest3, dest3, y_sorted, gates, x1, mod3, norm_w)


def _layer(x, c, w_ada, b_ada, norm_w, w_in, conv_w, a_log, dt_bias, a_norm_w, rel_bias, w_out,
           w_router, b_router, w_up, b_up, w_down, b_down):
    bsz, seq, d = x.shape
    n = bsz * seq
    a_width = conv_w.shape[1] // 3
    a_heads = a_log.shape[0]
    b_width = w_out.shape[0] - a_width
    n_experts = w_router.shape[1]
    off_gate = 4 * a_width
    off_b = off_gate + 2 * a_heads

    mod3 = _adaln(c, w_ada, b_ada).reshape(bsz, 6, d)
    x2 = x.reshape(n, d)

    w_main = jnp.concatenate([w_in[:, :off_gate], w_in[:, off_b:]], axis=1).astype(BF16)
    w_gate = jnp.zeros((d, LANES), F32).at[:, :2 * a_heads].set(w_in[:, off_gate:off_b]).astype(BF16)
    qkv_a, z_a, ab, qkv_b = _inproj(x2, mod3, norm_w, w_main, w_gate, conv_w, seq, 3 * a_width, a_width,
                                    3 * b_width)

    o_a = _gdn(qkv_a, z_a, ab, a_log, dt_bias, a_norm_w, bsz, seq)
    o_b = _band_attn(qkv_b, rel_bias, bsz, seq)

    w_out16 = w_out.astype(BF16)
    x1, h2, route, gates, cnt = _outproj(o_a, o_b, x2, mod3, norm_w, w_out16[:a_width], w_out16[a_width:],
                                         w_router, b_router, seq)

    top_idx = route[:, :TOP_K]
    rank = route[:, TOP_K:2 * TOP_K]
    counts = cnt[0, :n_experts].astype(jnp.int32)
    padded = (counts + MOE_BLOCK - 1) // MOE_BLOCK * MOE_BLOCK
    pend = jnp.cumsum(padded)
    pstart = pend - padded
    expert_ids = jnp.arange(n_experts, dtype=jnp.int32)
    dest = jnp.sum(jnp.where(top_idx[..., None] == expert_ids, pstart, 0), axis=-1) + rank
    n_assign = n * TOP_K
    n_blocks = -(-n_assign // MOE_BLOCK) + n_experts
    cap = n_blocks * MOE_BLOCK
    block_start = jnp.arange(n_blocks, dtype=jnp.int32) * MOE_BLOCK
    block_e = jnp.minimum(jnp.sum(pend[None, :] <= block_start[:, None], axis=1), n_experts - 1).astype(jnp.int32)
    n_used = (pend[-1:] // MOE_BLOCK).astype(jnp.int32)
    dest = dest.astype(jnp.int32)

    vend = (pstart + counts).astype(jnp.int32)
    block_vend = jnp.sum(jnp.where(block_e[:, None] == expert_ids, vend, 0), axis=1)
    n_valid = jnp.clip(block_vend - block_start, 0, MOE_BLOCK).astype(jnp.int32)

    xs = _dispatch(h2, dest, pend.astype(jnp.int32), vend, n_used, cap)
    blocks = jnp.arange(n_blocks, dtype=jnp.int32)
    later = (blocks[None, :] > blocks[:, None]) & (block_e[None, :] != block_e[:, None]) & (n_valid[None, :] > 0)
    first_later = jnp.min(jnp.where(later, blocks[None, :], n_blocks), axis=1)
    next_e = jnp.sum(jnp.where(blocks[None, :] == first_later[:, None], block_e[None, :] + 1, 0), axis=1) - 1
    y_sorted = _experts(xs, block_e, n_used, n_valid, next_e.astype(jnp.int32), w_up, b_up, w_down, b_down)
    out = _combine(y_sorted, dest, gates, x1, mod3, norm_w, seq)
    return out.reshape(bsz, seq, d)


def kernel(x, c, w_ada, b_ada, norm_w, w_in, conv_w, a_log, dt_bias, a_norm_w, rel_bias, w_out,
           w_router, b_router, w_up, b_up, w_down, b_down):
    for l in range(w_ada.shape[0]):
        x = _layer(x, c, w_ada[l], b_ada[l], norm_w[l], w_in[l], conv_w[l], a_log[l], dt_bias[l],
                   a_norm_w[l], rel_bias[l], w_out[l], w_router[l], b_router[l], w_up[l], b_up[l],
                   w_down[l], b_down[l])
    return x
```

```python
import functools
import math

import jax
import jax.numpy as jnp
from jax import lax
from jax.experimental import pallas as pl
from jax.experimental.pallas import tpu as pltpu

F32 = jnp.float32
BF16 = jnp.bfloat16
HIGHEST = lax.Precision.HIGHEST

EPS = 1e-6
CHUNK = 64
CONV_K = 4
A_HEAD_DIM = 128
B_HEAD_DIM = 64
B_PREV_CHUNKS = 8
REL_CLIP = 128
TOP_K = 4
SWIGLU_ALPHA = 1.702
SWIGLU_LIMIT = 7.0
MOE_BLOCK = 512
CLEAR_ROWS = 64
LANES = 128
NEG_BIG = -1e30

MIB = 1024 * 1024


def _dot(a, b):
    return jnp.dot(a, b, preferred_element_type=F32)


def _dot_nt(a, b):
    return lax.dot_general(a, b, (((1,), (1,)), ((), ())), preferred_element_type=F32)


def _dot_exact(a, b):
    return jnp.dot(a, b, precision=HIGHEST, preferred_element_type=F32)


def _rms(x, w):
    return x * lax.rsqrt(jnp.mean(x * x, axis=-1, keepdims=True) + EPS) * w


def _adaln_kernel(ct_ref, w_ref, b_ref, o_ref, *, bsz):
    ct = ct_ref[...]
    cs = ct * jax.nn.sigmoid(ct)
    w = w_ref[...]
    rows = [jnp.sum(cs[:, b:b + 1] * w, axis=0, keepdims=True) for b in range(bsz)]
    rows.append(jnp.zeros((o_ref.shape[0] - bsz, w.shape[1]), F32))
    o_ref[...] = jnp.concatenate(rows, axis=0) + b_ref[...]


def _adaln(c, w, b):
    bsz, d = c.shape
    n_out = w.shape[1]
    rows = 8
    ct = jnp.pad(c.T, ((0, 0), (0, LANES - bsz)))
    tn = 512
    out = pl.pallas_call(
        functools.partial(_adaln_kernel, bsz=bsz),
        grid=(n_out // tn,),
        in_specs=[
            pl.BlockSpec((d, LANES), lambda j: (0, 0)),
            pl.BlockSpec((d, tn), lambda j: (0, j)),
            pl.BlockSpec((1, tn), lambda j: (0, j)),
        ],
        out_specs=pl.BlockSpec((rows, tn), lambda j: (0, j)),
        out_shape=jax.ShapeDtypeStruct((rows, n_out), F32),
        name="adaln",
    )(ct, w, b.reshape(1, n_out))
    return out[:bsz]


def _inproj_kernel(x_ref, mod_ref, nw_ref, wm_ref, wg_ref, cw_ref, qkva_ref, z_ref, ab_ref, qkvb_ref, raw_ref,
                   *, a_qkv, a_z, tiles_per_seq):
    i = pl.program_id(0)
    tm = x_ref.shape[0]
    dk = A_HEAD_DIM
    a_width = a_qkv // 3
    halo = 8
    tn = 512

    @pl.when(lax.rem(i, tiles_per_seq) == 0)
    def _():
        raw_ref[0:halo, :] = jnp.zeros((halo, a_qkv), F32)

    x = x_ref[...]
    sh = mod_ref[0, 0:1, :]
    sc = mod_ref[0, 1:2, :]
    h = _rms(x, nw_ref[0:1, :]) * (1.0 + sc) + sh
    hb = h.astype(BF16)

    def conv_silu(c0):
        acc = cw_ref[CONV_K - 1:CONV_K, c0:c0 + dk] * raw_ref[halo:halo + tm, c0:c0 + dk]
        for j in range(CONV_K - 1):
            start = halo - (CONV_K - 1) + j
            acc = acc + cw_ref[j:j + 1, c0:c0 + dk] * raw_ref[start:start + tm, c0:c0 + dk]
        y = acc * jax.nn.sigmoid(acc)
        if c0 < 2 * a_width:
            y = y * lax.rsqrt(jnp.sum(y * y, axis=-1, keepdims=True) + EPS)
        if c0 < a_width:
            y = y * (dk ** -0.5)
        qkva_ref[:, c0:c0 + dk] = y

    for c0 in range(0, a_qkv, tn):
        raw_ref[halo:halo + tm, c0:c0 + tn] = _dot(hb, wm_ref[:, c0:c0 + tn])
    slabs = list(range(0, a_qkv, dk))
    other = [("z", c0) for c0 in range(0, a_z, tn)] + [("b", c0) for c0 in range(0, qkvb_ref.shape[1], tn)]
    per = -(-len(slabs) // len(other))
    off = a_qkv + a_z
    for n_o, (kind, c0) in enumerate(other):
        for c in slabs[n_o * per:(n_o + 1) * per]:
            conv_silu(c)
        if kind == "z":
            z_ref[:, c0:c0 + tn] = _dot(hb, wm_ref[:, a_qkv + c0:a_qkv + c0 + tn])
        else:
            qkvb_ref[:, c0:c0 + tn] = _dot(hb, wm_ref[:, off + c0:off + c0 + tn]).astype(BF16)
    ab_ref[...] = _dot(hb, wg_ref[...])
    raw_ref[0:halo, :] = raw_ref[tm:tm + halo, :]


def _inproj(x2, mod3, norm_w, w_main, w_gate, conv_w, seq, a_qkv, a_z, b_qkv):
    n, d = x2.shape
    tm = 512
    tiles_per_seq = seq // tm
    kern = functools.partial(_inproj_kernel, a_qkv=a_qkv, a_z=a_z, tiles_per_seq=tiles_per_seq)
    return pl.pallas_call(
        kern,
        grid=(n // tm,),
        in_specs=[
            pl.BlockSpec((tm, d), lambda i: (i, 0)),
            pl.BlockSpec((1, 6, d), lambda i: (i // tiles_per_seq, 0, 0)),
            pl.BlockSpec((4, d), lambda i: (0, 0)),
            pl.BlockSpec(w_main.shape, lambda i: (0, 0)),
            pl.BlockSpec(w_gate.shape, lambda i: (0, 0)),
            pl.BlockSpec(conv_w.shape, lambda i: (0, 0)),
        ],
        out_specs=[
            pl.BlockSpec((tm, a_qkv), lambda i: (i, 0)),
            pl.BlockSpec((tm, a_z), lambda i: (i, 0)),
            pl.BlockSpec((tm, LANES), lambda i: (i, 0)),
            pl.BlockSpec((tm, b_qkv), lambda i: (i, 0)),
        ],
        out_shape=[
            jax.ShapeDtypeStruct((n, a_qkv), F32),
            jax.ShapeDtypeStruct((n, a_z), F32),
            jax.ShapeDtypeStruct((n, LANES), F32),
            jax.ShapeDtypeStruct((n, b_qkv), BF16),
        ],
        scratch_shapes=[pltpu.VMEM((tm + 8, a_qkv), F32)],
        compiler_params=pltpu.CompilerParams(
            dimension_semantics=("arbitrary",), vmem_limit_bytes=48 * MIB),
        name="inproj",
    )(x2, mod3, norm_w, w_main, w_gate, conv_w)


def _gdn_kernel(qkv_ref, z_ref, ab_ref, alog_ref, dtb_ref, anw_ref, o_ref, state_ref, *, n_heads):
    i = pl.program_id(1)
    tt = qkv_ref.shape[0]
    width = qkv_ref.shape[1]
    a_width = width // 3
    dk = A_HEAD_DIM
    sub = 2 * CHUNK
    per_sub = sub // CHUNK

    @pl.when(i == 0)
    def _():
        state_ref[...] = jnp.zeros_like(state_ref)

    row = lax.broadcasted_iota(jnp.int32, (sub, sub), 0)
    col = lax.broadcasted_iota(jnp.int32, (sub, sub), 1)
    shift = int(math.log2(CHUNK))
    same = (row >> shift) == (col >> shift)
    tri_incl = same & (col <= row)
    tri_strict = same & (col < row)
    eye = (row == col).astype(F32)
    pair = (row >> 1) == (col >> 1)
    couples = [((row >> (lv + 1)) == (col >> (lv + 1))) & ((row >> lv) != (col >> lv))
               for lv in range(1, shift)]
    col_chunk = lax.broadcasted_iota(jnp.int32, (dk, sub), 1) >> shift
    tri_incl_f = tri_incl.astype(F32)
    same_f = same.astype(F32)

    n_sub = tt // sub
    gates = []
    for p in range(n_sub):
        r0 = p * sub
        ab = ab_ref[r0:r0 + sub, :]
        gfull = -jnp.exp(alog_ref[...]) * jax.nn.softplus(ab + dtb_ref[...])
        bfull = jax.nn.sigmoid(ab)
        gcum = _dot_exact(tri_incl_f, gfull)
        gtot = _dot_exact(same_f, gfull)
        gates.append((gcum, gcum.T, gtot, bfull))

    items = [(p, h) for p in range(n_sub) for h in range(n_heads)]
    pre = {}
    for p, h in items:
        r0 = p * sub
        gcum, gcum_t, gtot, bfull = gates[p]
        gc = gcum[:, h:h + 1]
        beta = bfull[:, n_heads + h:n_heads + h + 1]
        q = qkv_ref[r0:r0 + sub, h * dk:(h + 1) * dk]
        k = qkv_ref[r0:r0 + sub, a_width + h * dk:a_width + (h + 1) * dk]
        v = qkv_ref[r0:r0 + sub, 2 * a_width + h * dk:2 * a_width + (h + 1) * dk]
        decay = jnp.exp(jnp.where(tri_incl, gc - gcum_t[h:h + 1, :], -jnp.inf))
        kb = k * beta
        k16 = k.astype(BF16)
        lower = jnp.where(tri_strict, _dot_nt(kb.astype(BF16), k16) * decay, 0.0)
        attn16 = (_dot_nt(q.astype(BF16), k16) * decay).astype(BF16)
        eg = jnp.exp(gc)
        rhs = jnp.concatenate([kb * eg, v * beta], axis=1).astype(BF16)
        kd_t = (k * jnp.exp(gtot[:, h:h + 1] - gc)).T
        kd_st = jnp.concatenate([jnp.where(col_chunk == c, kd_t, 0.0) for c in range(per_sub)],
                                axis=0).astype(BF16)
        pre[p, h] = dict(lower=lower, attn16=attn16, rhs=rhs, kd_st=kd_st, qeg=q * eg)

    tmat = {it: eye - jnp.where(pair, pre[it]["lower"], 0.0) for it in items}
    for couple in couples:
        t16 = {it: tmat[it].astype(BF16) for it in items}
        half = {it: _dot(t16[it], jnp.where(couple, pre[it]["lower"], 0.0).astype(BF16)) for it in items}
        tmat = {it: tmat[it] - _dot(half[it].astype(BF16), t16[it]) for it in items}

    wu16 = {it: _dot(tmat[it].astype(BF16), pre[it]["rhs"]).astype(BF16) for it in items}
    a_wu = {it: _dot(pre[it]["attn16"], wu16[it]) for it in items}
    k_wu = {it: _dot(pre[it]["kd_st"], wu16[it]) for it in items}
    qp16 = {it: (pre[it]["qeg"] - a_wu[it][:, :dk]).astype(BF16) for it in items}

    state = [state_ref[h] for h in range(n_heads)]
    for p in range(n_sub):
        r0 = p * sub
        gtot = gates[p][2]
        outs = [[] for _ in range(n_heads)]
        for c in range(per_sub):
            c0 = c * CHUNK
            for h in range(n_heads):
                kw = k_wu[p, h]
                m16 = kw[c * dk:(c + 1) * dk, :dk].astype(BF16)
                res = _dot(jnp.concatenate([m16, qp16[p, h][c0:c0 + CHUNK]], axis=0), state[h].astype(BF16))
                outs[h].append(res[dk:] + a_wu[p, h][c0:c0 + CHUNK, dk:])
                state[h] = (state[h] * jnp.exp(gtot[c0:c0 + 1, h:h + 1]) - res[:dk]
                            + kw[c * dk:(c + 1) * dk, dk:])
        for h in range(n_heads):
            o = jnp.concatenate(outs[h], axis=0)
            zh = z_ref[r0:r0 + sub, h * dk:(h + 1) * dk]
            o_ref[r0:r0 + sub, h * dk:(h + 1) * dk] = (
                _rms(o, anw_ref[...]) * (zh * jax.nn.sigmoid(zh))).astype(o_ref.dtype)
    for h in range(n_heads):
        state_ref[h] = state[h]


def _gdn(qkv_a, z, ab, a_log, dt_bias, a_norm_w, bsz, seq):
    n, width = qkv_a.shape
    a_width = width // 3
    n_heads = a_width // A_HEAD_DIM
    tt = 512
    nt = seq // tt
    alog = jnp.zeros((1, LANES), F32).at[0, :n_heads].set(a_log)
    dtb = jnp.zeros((1, LANES), F32).at[0, :n_heads].set(dt_bias)
    kern = functools.partial(_gdn_kernel, n_heads=n_heads)
    return pl.pallas_call(
        kern,
        grid=(bsz, nt),
        in_specs=[
            pl.BlockSpec((tt, width), lambda b, i: (b * nt + i, 0)),
            pl.BlockSpec((tt, a_width), lambda b, i: (b * nt + i, 0)),
            pl.BlockSpec((tt, LANES), lambda b, i: (b * nt + i, 0)),
            pl.BlockSpec((1, LANES), lambda b, i: (0, 0)),
            pl.BlockSpec((1, LANES), lambda b, i: (0, 0)),
            pl.BlockSpec((1, A_HEAD_DIM), lambda b, i: (0, 0)),
        ],
        out_specs=pl.BlockSpec((tt, a_width), lambda b, i: (b * nt + i, 0)),
        out_shape=jax.ShapeDtypeStruct((n, a_width), BF16),
        scratch_shapes=[pltpu.VMEM((n_heads, A_HEAD_DIM, A_HEAD_DIM), F32)],
        compiler_params=pltpu.CompilerParams(
            dimension_semantics=("arbitrary", "arbitrary"), vmem_limit_bytes=48 * MIB),
        name="gdn",
    )(qkv_a, z, ab, alog, dtb, a_norm_w.reshape(1, A_HEAD_DIM))


def _attn_kernel(q_ref, *refs, n_heads, n_kb, n_q):
    n_win = n_q + n_kb - 1
    k_refs, v_refs = refs[:n_win], refs[n_win:2 * n_win]
    bias_ref, o_ref = refs[2 * n_win], refs[2 * n_win + 1]
    i = pl.program_id(1)
    tq = q_ref.shape[0] // n_q
    hd = B_HEAD_DIM
    per = LANES // hd
    lane = lax.broadcasted_iota(jnp.int32, (1, LANES), 1)
    items = [(t, h) for t in range(n_q) for h in range(n_heads)]

    def lanes_of(h):
        return slice((h // per) * LANES, (h // per + 1) * LANES)

    def in_head(h):
        hh = h % per
        return (lane >= hh * hd) & (lane < (hh + 1) * hd)

    scores = {}
    for t, h in items:
        qg = q_ref[t * tq:(t + 1) * tq, lanes_of(h)]
        qh = jnp.where(in_head(h), qg, jnp.zeros_like(qg)) * (hd ** -0.5)
        s = []
        for j in range(n_kb):
            sj = _dot_nt(qh, k_refs[t + j][:, lanes_of(h)]) + bias_ref[h, :, j * tq:(j + 1) * tq]
            if j < n_kb - 1:
                sj = jnp.where(i * n_q + t >= n_kb - 1 - j, sj, NEG_BIG)
            s.append(sj)
        scores[t, h] = s
    probs, dens = {}, {}
    for it in items:
        s = scores[it]
        top = s[0]
        for sj in s[1:]:
            top = jnp.maximum(top, sj)
        m = top.max(axis=-1, keepdims=True)
        p = [jnp.exp(sj - m) for sj in s]
        tot = p[0]
        for pj in p[1:]:
            tot = tot + pj
        probs[it] = [pj.astype(BF16) for pj in p]
        dens[it] = tot.sum(axis=-1, keepdims=True)
    outs = {}
    for t, h in items:
        acc = _dot(probs[t, h][0], v_refs[t][:, lanes_of(h)])
        for j in range(1, n_kb):
            acc = acc + _dot(probs[t, h][j], v_refs[t + j][:, lanes_of(h)])
        outs[t, h] = acc / dens[t, h]
    for t in range(n_q):
        for g in range(n_heads // per):
            out = outs[t, g * per]
            for hh in range(1, per):
                out = jnp.where(in_head(g * per + hh), outs[t, g * per + hh], out)
            o_ref[t * tq:(t + 1) * tq, g * LANES:(g + 1) * LANES] = out.astype(o_ref.dtype)


def _band_bias(rel_bias, tq, n_kb):
    n_h = rel_bias.shape[0]
    back = (n_kb - 1) * tq
    nk = n_kb * tq
    span = tq + nk - 1
    lo = (nk - 1 - back) - REL_CLIP
    hi = span - lo - (2 * REL_CLIP + 1)
    by_offset = jnp.concatenate([jnp.broadcast_to(rel_bias[:, :1], (n_h, lo)), rel_bias,
                                 jnp.broadcast_to(rel_bias[:, -1:], (n_h, hi))], axis=1).astype(F32)
    rev = jnp.pad(by_offset[:, ::-1], ((0, 0), (0, 1)))
    skew = jnp.broadcast_to(rev[:, None, :], (n_h, tq, span + 1)).reshape(n_h, tq * (span + 1))
    skew = skew[:, :tq * span].reshape(n_h, tq, span)
    table = skew[:, :, tq - 1:tq - 1 + nk]
    qc = jnp.arange(tq)[:, None] // CHUNK
    kc = jnp.arange(nk)[None, :] // CHUNK - back // CHUNK
    allowed = (kc <= qc) & (kc >= qc - B_PREV_CHUNKS)
    return jnp.where(allowed[None], table, NEG_BIG)


def _band_attn(qkv_b, rel_bias, bsz, seq):
    n, width = qkv_b.shape
    b_width = width // 3
    n_heads = b_width // B_HEAD_DIM
    tq = 256
    n_kb = 3
    n_q = 2
    assert (n_kb - 1) * tq == B_PREV_CHUNKS * CHUNK
    nt = seq // tq
    steps = nt // n_q
    n_win = n_q + n_kb - 1
    bias = _band_bias(rel_bias, tq, n_kb)
    kern = functools.partial(_attn_kernel, n_heads=n_heads, n_kb=n_kb, n_q=n_q)

    def kv_spec(colblk, w):
        return pl.BlockSpec(
            (tq, b_width), lambda b, i: (b * nt + jnp.maximum(i * n_q - (n_kb - 1) + w, 0), colblk))

    return pl.pallas_call(
        kern,
        grid=(bsz, steps),
        in_specs=[pl.BlockSpec((n_q * tq, b_width), lambda b, i: (b * steps + i, 0))]
        + [kv_spec(1, w) for w in range(n_win)] + [kv_spec(2, w) for w in range(n_win)]
        + [pl.BlockSpec(bias.shape, lambda b, i: (0, 0, 0))],
        out_specs=pl.BlockSpec((n_q * tq, b_width), lambda b, i: (b * steps + i, 0)),
        out_shape=jax.ShapeDtypeStruct((n, b_width), BF16),
        compiler_params=pltpu.CompilerParams(
            dimension_semantics=("parallel", "parallel"), vmem_limit_bytes=48 * MIB),
        name="band_attn",
    )(*([qkv_b] * (1 + 2 * n_win)), bias)


def _outproj_kernel(oa_ref, ob_ref, x_ref, mod_ref, nw_ref, woa_ref, wob_ref, wr_ref, br_ref,
                    x1_ref, h2_ref, route_ref, gate_ref, cnt_ref, carry_ref, *, n_experts):
    i = pl.program_id(0)
    tm = x_ref.shape[0]
    sub = 256
    parts = [slice(r0, r0 + sub) for r0 in range(0, tm, sub)]

    @pl.when(i == 0)
    def _():
        carry_ref[...] = jnp.zeros_like(carry_ref)

    ga1 = mod_ref[0, 2:3, :]
    sh2 = mod_ref[0, 3:4, :]
    sc2 = mod_ref[0, 4:5, :]
    ys = [_dot(oa_ref[r, :], woa_ref[...]) + _dot(ob_ref[r, :], wob_ref[...]) for r in parts]
    h2s = []
    for r, y in zip(parts, ys):
        x1 = x_ref[r, :] + ga1 * _rms(y, nw_ref[1:2, :])
        x1_ref[r, :] = x1
        h2 = _rms(x1, nw_ref[2:3, :]) * (1.0 + sc2) + sh2
        h2_ref[r, :] = h2
        h2s.append(h2)

    logits = []
    for h2 in h2s:
        h_hi = h2.astype(BF16)
        h_lo = (h2 - h_hi.astype(F32)).astype(BF16)
        logits.append(_dot(h_hi, wr_ref[0]) + (_dot(h_hi, wr_ref[1]) + _dot(h_lo, wr_ref[0])) + br_ref[...])

    lane_i = lax.broadcasted_iota(jnp.int32, (sub, LANES), 1)
    lane = lane_i.astype(F32)
    lgs = [jnp.where(lane_i < n_experts, lg, -jnp.inf) for lg in logits]
    vals = [[] for _ in parts]
    idxs = [[] for _ in parts]
    for _ in range(TOP_K):
        for p in range(len(parts)):
            m = lgs[p].max(axis=-1, keepdims=True)
            idx = jnp.where(lgs[p] == m, lane, float(LANES)).min(axis=-1, keepdims=True)
            vals[p].append(m)
            idxs[p].append(idx)
            lgs[p] = jnp.where(lane == idx, -jnp.inf, lgs[p])

    row = lax.broadcasted_iota(jnp.int32, (sub, sub), 0)
    col = lax.broadcasted_iota(jnp.int32, (sub, sub), 1)
    before = (col < row).astype(BF16)
    carry = carry_ref[0:1, :]
    for p, r in enumerate(parts):
        ex = [jnp.exp(v - vals[p][0]) for v in vals[p]]
        den = ex[0]
        for e in ex[1:]:
            den = den + e
        onehot = jnp.zeros((sub, LANES), F32)
        for idx in idxs[p]:
            onehot = onehot + (lane == idx).astype(F32)
        cum = _dot(before, onehot.astype(BF16)) + carry
        carry = carry + onehot.sum(axis=0, keepdims=True)
        route = jnp.zeros((sub, LANES), F32)
        gate = jnp.zeros((sub, LANES), F32)
        for k in range(TOP_K):
            rank = jnp.where(lane == idxs[p][k], cum, 0.0).sum(axis=-1, keepdims=True)
            route = jnp.where(lane_i == k, idxs[p][k], route)
            route = jnp.where(lane_i == TOP_K + k, rank, route)
            gate = jnp.where(lane_i == k, ex[k] / den, gate)
        route_ref[r, :] = route.astype(jnp.int32)
        gate_ref[r, :] = gate
    carry_ref[...] = jnp.broadcast_to(carry, carry_ref.shape)
    cnt_ref[...] = jnp.broadcast_to(carry, cnt_ref.shape)


def _outproj(o_a, o_b, x2, mod3, norm_w, w_out_a, w_out_b, w_router, b_router, seq):
    n, d = x2.shape
    n_experts = w_router.shape[1]
    tm = 1024
    tiles_per_seq = seq // tm
    wr32 = jnp.zeros((d, LANES), F32).at[:, :n_experts].set(w_router)
    wr_hi = wr32.astype(BF16)
    wr = jnp.stack([wr_hi, (wr32 - wr_hi.astype(F32)).astype(BF16)])
    br = jnp.zeros((1, LANES), F32).at[0, :n_experts].set(b_router)
    kern = functools.partial(_outproj_kernel, n_experts=n_experts)
    aw = o_a.shape[1]
    bw = o_b.shape[1]
    return pl.pallas_call(
        kern,
        grid=(n // tm,),
        in_specs=[
            pl.BlockSpec((tm, aw), lambda i: (i, 0)),
            pl.BlockSpec((tm, bw), lambda i: (i, 0)),
            pl.BlockSpec((tm, d), lambda i: (i, 0)),
            pl.BlockSpec((1, 6, d), lambda i: (i // tiles_per_seq, 0, 0)),
            pl.BlockSpec((4, d), lambda i: (0, 0)),
            pl.BlockSpec((aw, d), lambda i: (0, 0)),
            pl.BlockSpec((bw, d), lambda i: (0, 0)),
            pl.BlockSpec((2, d, LANES), lambda i: (0, 0, 0)),
            pl.BlockSpec((1, LANES), lambda i: (0, 0)),
        ],
        out_specs=[
            pl.BlockSpec((tm, d), lambda i: (i, 0)),
            pl.BlockSpec((tm, d), lambda i: (i, 0)),
            pl.BlockSpec((tm, LANES), lambda i: (i, 0)),
            pl.BlockSpec((tm, LANES), lambda i: (i, 0)),
            pl.BlockSpec((8, LANES), lambda i: (0, 0)),
        ],
        out_shape=[
            jax.ShapeDtypeStruct((n, d), F32),
            jax.ShapeDtypeStruct((n, d), F32),
            jax.ShapeDtypeStruct((n, LANES), jnp.int32),
            jax.ShapeDtypeStruct((n, LANES), F32),
            jax.ShapeDtypeStruct((8, LANES), F32),
        ],
        scratch_shapes=[pltpu.VMEM((8, LANES), F32)],
        compiler_params=pltpu.CompilerParams(
            dimension_semantics=("arbitrary",), vmem_limit_bytes=48 * MIB),
        name="outproj_router",
    )(o_a, o_b, x2, mod3, norm_w, w_out_a, w_out_b, wr, br)


def _dispatch_kernel(pend_ref, vend_ref, nu_ref, dest_ref, h_ref, xs_hbm, stage, zeros, sem, zsem, tsem,
                     *, n_experts, blk):
    i = pl.program_id(0)
    n_tiles = pl.num_programs(0)
    tm, d = h_ref.shape
    slot = lax.rem(i, 2)
    piece = zeros.shape[0]
    shift = int(math.log2(piece))
    n_pieces = xs_hbm.shape[0] // piece

    def wait_rows(s):
        for _ in range(TOP_K):
            pltpu.make_async_copy(stage.at[s], xs_hbm.at[pl.ds(0, tm), :], sem.at[s]).wait()

    def clear(sem_ref):
        def body(p, carry):
            start = pl.multiple_of(p * piece, piece)
            pltpu.make_async_copy(zeros, xs_hbm.at[pl.ds(start, piece), :], sem_ref).start()
            return carry
        return body

    def wait_clear(sem_ref):
        def body(p, carry):
            pltpu.make_async_copy(zeros, xs_hbm.at[pl.ds(0, piece), :], sem_ref).wait()
            return carry
        return body

    @pl.when(i == 0)
    def _():
        zeros[...] = jnp.zeros_like(zeros)
        for e in range(n_experts):
            lax.fori_loop(vend_ref[e] >> shift, pend_ref[e] >> shift, clear(zsem), 0)
        lax.fori_loop((nu_ref[0] * blk) >> shift, n_pieces, clear(tsem), 0)
        for e in range(n_experts):
            lax.fori_loop(vend_ref[e] >> shift, pend_ref[e] >> shift, wait_clear(zsem), 0)

    @pl.when(i >= 2)
    def _():
        wait_rows(slot)

    stage[slot] = h_ref[...]

    for t in range(tm):
        for k in range(TOP_K):
            s = dest_ref[0, 0, t * TOP_K + k]
            pltpu.make_async_copy(stage.at[slot, pl.ds(t, 1), :], xs_hbm.at[pl.ds(s, 1), :],
                                  sem.at[slot]).start(priority=k % 2)

    @pl.when(i == n_tiles - 1)
    def _():
        wait_rows(slot)

        @pl.when(n_tiles > 1)
        def _():
            wait_rows(1 - slot)

        lax.fori_loop((nu_ref[0] * blk) >> shift, n_pieces, wait_clear(tsem), 0)


def _dispatch(h2, dest, pend, vend, n_used, cap):
    n, d = h2.shape
    n_experts = pend.shape[0]
    tm = 256
    n_tiles = n // tm
    kern = functools.partial(_dispatch_kernel, n_experts=n_experts, blk=MOE_BLOCK)
    grid_spec = pltpu.PrefetchScalarGridSpec(
        num_scalar_prefetch=3,
        grid=(n_tiles,),
        in_specs=[
            pl.BlockSpec((1, 1, tm * TOP_K), lambda i, pe, pa, nu: (i, 0, 0), memory_space=pltpu.SMEM),
            pl.BlockSpec((tm, d), lambda i, pe, pa, nu: (i, 0)),
        ],
        out_specs=pl.BlockSpec(memory_space=pl.ANY),
        scratch_shapes=[
            pltpu.VMEM((2, tm, d), F32),
            pltpu.VMEM((CLEAR_ROWS, d), F32),
            pltpu.SemaphoreType.DMA((2,)),
            pltpu.SemaphoreType.DMA(()),
            pltpu.SemaphoreType.DMA(()),
        ],
    )
    return pl.pallas_call(
        kern,
        grid_spec=grid_spec,
        out_shape=jax.ShapeDtypeStruct((cap, d), F32),
        compiler_params=pltpu.CompilerParams(
            dimension_semantics=("arbitrary",), vmem_limit_bytes=32 * MIB),
        name="dispatch",
    )(pend, vend, n_used, dest.reshape(n_tiles, 1, tm * TOP_K), h2)


def _expert_kernel(be_ref, nu_ref, nv_ref, nx_ref, xs_ref, wup_hbm, bup_ref, wdn_hbm, bdn_ref, y_ref,
                   up_stage, dn_stage, wsem, wup16, wdn16):
    j = pl.program_id(0)
    blk, d = xs_ref.shape
    f = wdn16.shape[0]
    n_valid = nv_ref[j]
    half = blk // 2

    def weight_copies(e):
        return (pltpu.make_async_copy(wup_hbm.at[e], up_stage, wsem.at[0]),
                pltpu.make_async_copy(wdn_hbm.at[e], dn_stage, wsem.at[1]))

    prev = be_ref[jnp.maximum(j - 1, 0)]
    first_of_expert = jnp.logical_and(n_valid > 0, jnp.logical_or(j == 0, be_ref[j] != prev))

    @pl.when(jnp.logical_and(j == 0, n_valid > 0))
    def _():
        for c in weight_copies(be_ref[0]):
            c.start()

    @pl.when(first_of_expert)
    def _():
        for c in weight_copies(be_ref[j]):
            c.wait()
        step = 128
        for r0 in range(0, d, step):
            wup16[r0:r0 + step, :] = up_stage[r0:r0 + step, :].astype(BF16)
        for r0 in range(0, f, step):
            wdn16[r0:r0 + step, :] = dn_stage[r0:r0 + step, :].astype(BF16)

        @pl.when(nx_ref[j] >= 0)
        def _():
            for c in weight_copies(nx_ref[j]):
                c.start()

    def mlp(rows):
        xb = xs_ref[0:rows, :].astype(BF16)
        fc = min(f, 512)
        acc = None
        for c0 in range(0, f, fc):
            glu = _dot(xb, wup16[:, c0:c0 + fc]) + bup_ref[0, :, c0:c0 + fc]
            lin = _dot(xb, wup16[:, f + c0:f + c0 + fc]) + bup_ref[0, :, f + c0:f + c0 + fc]
            glu = jnp.minimum(glu, SWIGLU_LIMIT)
            lin = jnp.clip(lin, -SWIGLU_LIMIT, SWIGLU_LIMIT)
            act = glu * jax.nn.sigmoid(SWIGLU_ALPHA * glu) * (lin + 1.0)
            part = _dot(act.astype(BF16), wdn16[c0:c0 + fc, :])
            acc = part if acc is None else acc + part
        y_ref[0:rows, :] = acc + bdn_ref[0]

    @pl.when(n_valid > half)
    def _():
        mlp(blk)

    @pl.when(jnp.logical_and(n_valid > 0, n_valid <= half))
    def _():
        mlp(half)
        y_ref[half:, :] = jnp.zeros((blk - half, d), y_ref.dtype)

    @pl.when(n_valid == 0)
    def _():
        y_ref[...] = jnp.zeros_like(y_ref)


def _experts(xs, block_e, n_used, n_valid, next_e, w_up, b_up, w_down, b_down):
    cap, d = xs.shape
    n_exp, _, f2 = w_up.shape
    f = w_down.shape[1]
    blk = MOE_BLOCK
    n_blocks = cap // blk
    grid_spec = pltpu.PrefetchScalarGridSpec(
        num_scalar_prefetch=4,
        grid=(n_blocks,),
        in_specs=[
            pl.BlockSpec((blk, d), lambda j, be, nu, nv, nx: (jnp.minimum(j, jnp.maximum(nu[0] - 1, 0)), 0)),
            pl.BlockSpec(memory_space=pl.ANY),
            pl.BlockSpec((1, 1, f2), lambda j, be, nu, nv, nx: (be[j], 0, 0)),
            pl.BlockSpec(memory_space=pl.ANY),
            pl.BlockSpec((1, 1, d), lambda j, be, nu, nv, nx: (be[j], 0, 0)),
        ],
        out_specs=pl.BlockSpec((blk, d), lambda j, be, nu, nv, nx: (j, 0)),
        scratch_shapes=[
            pltpu.VMEM((d, f2), F32),
            pltpu.VMEM((f, d), F32),
            pltpu.SemaphoreType.DMA((2,)),
            pltpu.VMEM((d, f2), BF16),
            pltpu.VMEM((f, d), BF16),
        ],
    )
    return pl.pallas_call(
        _expert_kernel,
        grid_spec=grid_spec,
        out_shape=jax.ShapeDtypeStruct((cap, d), F32),
        compiler_params=pltpu.CompilerParams(
            dimension_semantics=("arbitrary",), vmem_limit_bytes=56 * MIB),
        name="experts",
    )(block_e, n_used, n_valid, next_e, xs, w_up, b_up.reshape(n_exp, 1, f2), w_down, b_down.reshape(n_exp, 1, d))


def _combine_kernel(dest_ref, dest_next_ref, y_hbm, gate_ref, x1_ref, mod_ref, nw_ref, o_ref, buf, sem):
    i = pl.program_id(0)
    n_tiles = pl.num_programs(0)
    tm, d = x1_ref.shape
    slot = lax.rem(i, 2)

    def start_gather(idx_ref, dst_slot):
        for t in range(tm):
            for k in range(TOP_K):
                s = idx_ref[0, 0, t * TOP_K + k]
                pltpu.make_async_copy(y_hbm.at[pl.ds(s, 1), :], buf.at[dst_slot, k, pl.ds(t, 1), :],
                                      sem.at[dst_slot]).start(priority=k % 2)

    @pl.when(i == 0)
    def _():
        start_gather(dest_ref, 0)

    @pl.when(i + 1 < n_tiles)
    def _():
        start_gather(dest_next_ref, 1 - slot)

    for k in range(TOP_K):
        pltpu.make_async_copy(y_hbm.at[pl.ds(0, tm), :], buf.at[slot, k], sem.at[slot]).wait()

    gate = gate_ref[...]
    ysum = gate[:, 0:1] * buf[slot, 0]
    for k in range(1, TOP_K):
        ysum = ysum + gate[:, k:k + 1] * buf[slot, k]
    ga2 = mod_ref[0, 5:6, :]
    o_ref[...] = x1_ref[...] + ga2 * _rms(ysum, nw_ref[3:4, :])


def _combine(y_sorted, dest, gates, x1, mod3, norm_w, seq):
    n, d = x1.shape
    tm = 256
    n_tiles = n // tm
    tiles_per_seq = seq // tm
    dest3 = dest.reshape(n_tiles, 1, tm * TOP_K)
    return pl.pallas_call(
        _combine_kernel,
        grid=(n_tiles,),
        in_specs=[
            pl.BlockSpec((1, 1, tm * TOP_K), lambda i: (i, 0, 0), memory_space=pltpu.SMEM),
            pl.BlockSpec((1, 1, tm * TOP_K), lambda i: (jnp.minimum(i + 1, n_tiles - 1), 0, 0),
                         memory_space=pltpu.SMEM),
            pl.BlockSpec(memory_space=pl.ANY),
            pl.BlockSpec((tm, LANES), lambda i: (i, 0)),
            pl.BlockSpec((tm, d), lambda i: (i, 0)),
            pl.BlockSpec((1, 6, d), lambda i: (i // tiles_per_seq, 0, 0)),
            pl.BlockSpec((4, d), lambda i: (0, 0)),
        ],
        out_specs=pl.BlockSpec((tm, d), lambda i: (i, 0)),
        out_shape=jax.ShapeDtypeStruct((n, d), F32),
        scratch_shapes=[
            pltpu.VMEM((2, TOP_K, tm, d), F32),
            pltpu.SemaphoreType.DMA((2,)),
        ],
        compiler_params=pltpu.CompilerParams(
            dimension_semantics=("arbitrary",), vmem_limit_bytes=32 * MIB),
        name="combine",
    )(dest3, dest3, y_sorted, gates, x1, mod3, norm_w)


def _layer(x, c, w_ada, b_ada, norm_w, w_in, conv_w, a_log, dt_bias, a_norm_w, rel_bias, w_out,
           w_router, b_router, w_up, b_up, w_down, b_down):
    bsz, seq, d = x.shape
    n = bsz * seq
    a_width = conv_w.shape[1] // 3
    a_heads = a_log.shape[0]
    b_width = w_out.shape[0] - a_width
    n_experts = w_router.shape[1]
    off_gate = 4 * a_width
    off_b = off_gate + 2 * a_heads

    mod3 = _adaln(c, w_ada, b_ada).reshape(bsz, 6, d)
    x2 = x.reshape(n, d)

    w_main = jnp.concatenate([w_in[:, :off_gate], w_in[:, off_b:]], axis=1).astype(BF16)
    w_gate = jnp.pad(w_in[:, off_gate:off_b], ((0, 0), (0, LANES - 2 * a_heads))).astype(BF16)
    qkv_a, z_a, ab, qkv_b = _inproj(x2, mod3, norm_w, w_main, w_gate, conv_w, seq, 3 * a_width, a_width,
                                    3 * b_width)

    o_a = _gdn(qkv_a, z_a, ab, a_log, dt_bias, a_norm_w, bsz, seq)
    o_b = _band_attn(qkv_b, rel_bias, bsz, seq)

    x1, h2, route, gates, cnt = _outproj(o_a, o_b, x2, mod3, norm_w, w_out[:a_width].astype(BF16),
                                         w_out[a_width:].astype(BF16), w_router, b_router, seq)

    top_idx = route[:, :TOP_K]
    rank = route[:, TOP_K:2 * TOP_K]
    counts = cnt[0, :n_experts].astype(jnp.int32)
    padded = (counts + MOE_BLOCK - 1) // MOE_BLOCK * MOE_BLOCK
    pend = jnp.cumsum(padded)
    pstart = pend - padded
    expert_ids = jnp.arange(n_experts, dtype=jnp.int32)
    dest = jnp.sum(jnp.where(top_idx[..., None] == expert_ids, pstart, 0), axis=-1) + rank
    n_assign = n * TOP_K
    n_blocks = -(-n_assign // MOE_BLOCK) + n_experts
    cap = n_blocks * MOE_BLOCK
    block_start = jnp.arange(n_blocks, dtype=jnp.int32) * MOE_BLOCK
    block_e = jnp.minimum(jnp.sum(pend[None, :] <= block_start[:, None], axis=1), n_experts - 1).astype(jnp.int32)
    n_used = (pend[-1:] // MOE_BLOCK).astype(jnp.int32)
    dest = dest.astype(jnp.int32)

    vend = (pstart + counts).astype(jnp.int32)
    block_vend = jnp.sum(jnp.where(block_e[:, None] == expert_ids, vend, 0), axis=1)
    n_valid = jnp.clip(block_vend - block_start, 0, MOE_BLOCK).astype(jnp.int32)

    xs = _dispatch(h2, dest, pend.astype(jnp.int32), vend, n_used, cap)
    blocks = jnp.arange(n_blocks, dtype=jnp.int32)
    later = (blocks[None, :] > blocks[:, None]) & (block_e[None, :] != block_e[:, None]) & (n_valid[None, :] > 0)
    first_later = jnp.min(jnp.where(later, blocks[None, :], n_blocks), axis=1)
    next_e = jnp.sum(jnp.where(blocks[None, :] == first_later[:, None], block_e[None, :] + 1, 0), axis=1) - 1
    y_sorted = _experts(xs, block_e, n_used, n_valid, next_e.astype(jnp.int32), w_up, b_up, w_down, b_down)
    out = _combine(y_sorted, dest, gates, x1, mod3, norm_w, seq)
    return out.reshape(bsz, seq, d)


def kernel(x, c, w_ada, b_ada, norm_w, w_in, conv_w, a_log, dt_bias, a_norm_w, rel_bias, w_out,
           w_router, b_router, w_up, b_up, w_down, b_down):
    for l in range(w_ada.shape[0]):
        x = _layer(x, c, w_ada[l], b_ada[l], norm_w[l], w_in[l], conv_w[l], a_log[l], dt_bias[l],
                   a_norm_w[l], rel_bias[l], w_out[l], w_router[l], b_router[l], w_up[l], b_up[l],
                   w_down[l], b_down[l])
    return x
```

```python
import functools
import math

import jax
import jax.numpy as jnp
from jax import lax
from jax.experimental import pallas as pl
from jax.experimental.pallas import tpu as pltpu

F32 = jnp.float32
BF16 = jnp.bfloat16
HIGHEST = lax.Precision.HIGHEST

EPS = 1e-6
CHUNK = 64
CONV_K = 4
A_HEAD_DIM = 128
B_HEAD_DIM = 64
B_PREV_CHUNKS = 8
REL_CLIP = 128
TOP_K = 4
SWIGLU_ALPHA = 1.702
SWIGLU_LIMIT = 7.0
MOE_BLOCK = 512
CLEAR_ROWS = 64
LANES = 128
NEG_BIG = -1e30

MIB = 1024 * 1024


def _dot(a, b):
    return jnp.dot(a, b, preferred_element_type=F32)


def _dot_nt(a, b):
    return lax.dot_general(a, b, (((1,), (1,)), ((), ())), preferred_element_type=F32)


def _dot_exact(a, b):
    return jnp.dot(a, b, precision=HIGHEST, preferred_element_type=F32)


def _rms(x, w):
    return x * lax.rsqrt(jnp.mean(x * x, axis=-1, keepdims=True) + EPS) * w


def _adaln_kernel(ct_ref, w_ref, b_ref, o_ref, *, bsz):
    ct = ct_ref[...]
    cs = ct * jax.nn.sigmoid(ct)
    w = w_ref[...]
    rows = [jnp.sum(cs[:, b:b + 1] * w, axis=0, keepdims=True) for b in range(bsz)]
    rows.append(jnp.zeros((o_ref.shape[0] - bsz, w.shape[1]), F32))
    o_ref[...] = jnp.concatenate(rows, axis=0) + b_ref[...]


def _adaln(c, w, b):
    bsz, d = c.shape
    n_out = w.shape[1]
    rows = 8
    ct = jnp.pad(c.T, ((0, 0), (0, LANES - bsz)))
    tn = 512
    out = pl.pallas_call(
        functools.partial(_adaln_kernel, bsz=bsz),
        grid=(n_out // tn,),
        in_specs=[
            pl.BlockSpec((d, LANES), lambda j: (0, 0)),
            pl.BlockSpec((d, tn), lambda j: (0, j)),
            pl.BlockSpec((1, tn), lambda j: (0, j)),
        ],
        out_specs=pl.BlockSpec((rows, tn), lambda j: (0, j)),
        out_shape=jax.ShapeDtypeStruct((rows, n_out), F32),
        name="adaln",
    )(ct, w, b.reshape(1, n_out))
    return out[:bsz]


def _inproj_kernel(x_ref, mod_ref, nw_ref, wm_ref, wg_ref, cw_ref, qkva_ref, z_ref, ab_ref, qkvb_ref, raw_ref,
                   *, a_qkv, a_z, tiles_per_seq):
    i = pl.program_id(0)
    tm = x_ref.shape[0]
    dk = A_HEAD_DIM
    a_width = a_qkv // 3
    halo = 8
    tn = 512

    @pl.when(lax.rem(i, tiles_per_seq) == 0)
    def _():
        raw_ref[0:halo, :] = jnp.zeros((halo, a_qkv), F32)

    x = x_ref[...]
    sh = mod_ref[0, 0:1, :]
    sc = mod_ref[0, 1:2, :]
    h = _rms(x, nw_ref[0:1, :]) * (1.0 + sc) + sh
    hb = h.astype(BF16)

    def conv_silu(c0):
        acc = cw_ref[CONV_K - 1:CONV_K, c0:c0 + dk] * raw_ref[halo:halo + tm, c0:c0 + dk]
        for j in range(CONV_K - 1):
            start = halo - (CONV_K - 1) + j
            acc = acc + cw_ref[j:j + 1, c0:c0 + dk] * raw_ref[start:start + tm, c0:c0 + dk]
        y = acc * jax.nn.sigmoid(acc)
        if c0 < 2 * a_width:
            y = y * lax.rsqrt(jnp.sum(y * y, axis=-1, keepdims=True) + EPS)
        if c0 < a_width:
            y = y * (dk ** -0.5)
        qkva_ref[:, c0:c0 + dk] = y

    for c0 in range(0, a_qkv, tn):
        raw_ref[halo:halo + tm, c0:c0 + tn] = _dot(hb, wm_ref[:, c0:c0 + tn])
    slabs = list(range(0, a_qkv, dk))
    other = [("z", c0) for c0 in range(0, a_z, tn)] + [("b", c0) for c0 in range(0, qkvb_ref.shape[1], tn)]
    per = -(-len(slabs) // len(other))
    off = a_qkv + a_z
    for n_o, (kind, c0) in enumerate(other):
        for c in slabs[n_o * per:(n_o + 1) * per]:
            conv_silu(c)
        if kind == "z":
            z_ref[:, c0:c0 + tn] = _dot(hb, wm_ref[:, a_qkv + c0:a_qkv + c0 + tn])
        else:
            qkvb_ref[:, c0:c0 + tn] = _dot(hb, wm_ref[:, off + c0:off + c0 + tn]).astype(BF16)
    ab_ref[...] = _dot(hb, wg_ref[...])
    raw_ref[0:halo, :] = raw_ref[tm:tm + halo, :]


def _inproj(x2, mod3, norm_w, w_main, w_gate, conv_w, seq, a_qkv, a_z, b_qkv):
    n, d = x2.shape
    tm = 512
    tiles_per_seq = seq // tm
    kern = functools.partial(_inproj_kernel, a_qkv=a_qkv, a_z=a_z, tiles_per_seq=tiles_per_seq)
    return pl.pallas_call(
        kern,
        grid=(n // tm,),
        in_specs=[
            pl.BlockSpec((tm, d), lambda i: (i, 0)),
            pl.BlockSpec((1, 6, d), lambda i: (i // tiles_per_seq, 0, 0)),
            pl.BlockSpec((4, d), lambda i: (0, 0)),
            pl.BlockSpec(w_main.shape, lambda i: (0, 0)),
            pl.BlockSpec(w_gate.shape, lambda i: (0, 0)),
            pl.BlockSpec(conv_w.shape, lambda i: (0, 0)),
        ],
        out_specs=[
            pl.BlockSpec((tm, a_qkv), lambda i: (i, 0)),
            pl.BlockSpec((tm, a_z), lambda i: (i, 0)),
            pl.BlockSpec((tm, LANES), lambda i: (i, 0)),
            pl.BlockSpec((tm, b_qkv), lambda i: (i, 0)),
        ],
        out_shape=[
            jax.ShapeDtypeStruct((n, a_qkv), F32),
            jax.ShapeDtypeStruct((n, a_z), F32),
            jax.ShapeDtypeStruct((n, LANES), F32),
            jax.ShapeDtypeStruct((n, b_qkv), BF16),
        ],
        scratch_shapes=[pltpu.VMEM((tm + 8, a_qkv), F32)],
        compiler_params=pltpu.CompilerParams(
            dimension_semantics=("arbitrary",), vmem_limit_bytes=48 * MIB),
        name="inproj",
    )(x2, mod3, norm_w, w_main, w_gate, conv_w)


def _gdn_kernel(qkv_ref, z_ref, ab_ref, alog_ref, dtb_ref, anw_ref, o_ref, state_ref, *, n_heads):
    i = pl.program_id(1)
    tt = qkv_ref.shape[0]
    width = qkv_ref.shape[1]
    a_width = width // 3
    dk = A_HEAD_DIM
    sub = 2 * CHUNK
    per_sub = sub // CHUNK

    @pl.when(i == 0)
    def _():
        state_ref[...] = jnp.zeros_like(state_ref)

    row = lax.broadcasted_iota(jnp.int32, (sub, sub), 0)
    col = lax.broadcasted_iota(jnp.int32, (sub, sub), 1)
    shift = int(math.log2(CHUNK))
    same = (row >> shift) == (col >> shift)
    tri_incl = same & (col <= row)
    tri_strict = same & (col < row)
    eye = (row == col).astype(F32)
    pair = (row >> 1) == (col >> 1)
    couples = [((row >> (lv + 1)) == (col >> (lv + 1))) & ((row >> lv) != (col >> lv))
               for lv in range(1, shift)]
    col_chunk = lax.broadcasted_iota(jnp.int32, (dk, sub), 1) >> shift
    tri_incl_f = tri_incl.astype(F32)
    same_f = same.astype(F32)

    n_sub = tt // sub
    gates = []
    for p in range(n_sub):
        r0 = p * sub
        ab = ab_ref[r0:r0 + sub, :]
        gfull = -jnp.exp(alog_ref[...]) * jax.nn.softplus(ab + dtb_ref[...])
        bfull = jax.nn.sigmoid(ab)
        gcum = _dot_exact(tri_incl_f, gfull)
        gtot = _dot_exact(same_f, gfull)
        gates.append((gcum, gcum.T, gtot, bfull))

    items = [(p, h) for p in range(n_sub) for h in range(n_heads)]
    pre = {}
    for p, h in items:
        r0 = p * sub
        gcum, gcum_t, gtot, bfull = gates[p]
        gc = gcum[:, h:h + 1]
        beta = bfull[:, n_heads + h:n_heads + h + 1]
        q = qkv_ref[r0:r0 + sub, h * dk:(h + 1) * dk]
        k = qkv_ref[r0:r0 + sub, a_width + h * dk:a_width + (h + 1) * dk]
        v = qkv_ref[r0:r0 + sub, 2 * a_width + h * dk:2 * a_width + (h + 1) * dk]
        decay = jnp.exp(jnp.where(tri_incl, gc - gcum_t[h:h + 1, :], -jnp.inf))
        kb = k * beta
        k16 = k.astype(BF16)
        lower = jnp.where(tri_strict, _dot_nt(kb.astype(BF16), k16) * decay, 0.0)
        attn16 = (_dot_nt(q.astype(BF16), k16) * decay).astype(BF16)
        eg = jnp.exp(gc)
        rhs = jnp.concatenate([kb * eg, v * beta], axis=1).astype(BF16)
        kd_t = (k * jnp.exp(gtot[:, h:h + 1] - gc)).T
        kd_st = jnp.concatenate([jnp.where(col_chunk == c, kd_t, 0.0) for c in range(per_sub)],
                                axis=0).astype(BF16)
        pre[p, h] = dict(lower=lower, attn16=attn16, rhs=rhs, kd_st=kd_st, qeg=q * eg)

    tmat = {it: eye - jnp.where(pair, pre[it]["lower"], 0.0) for it in items}
    for couple in couples:
        t16 = {it: tmat[it].astype(BF16) for it in items}
        half = {it: _dot(t16[it], jnp.where(couple, pre[it]["lower"], 0.0).astype(BF16)) for it in items}
        tmat = {it: tmat[it] - _dot(half[it].astype(BF16), t16[it]) for it in items}

    wu16 = {it: _dot(tmat[it].astype(BF16), pre[it]["rhs"]).astype(BF16) for it in items}
    a_wu = {it: _dot(pre[it]["attn16"], wu16[it]) for it in items}
    k_wu = {it: _dot(pre[it]["kd_st"], wu16[it]) for it in items}
    qp16 = {it: (pre[it]["qeg"] - a_wu[it][:, :dk]).astype(BF16) for it in items}

    state = [state_ref[h] for h in range(n_heads)]
    for p in range(n_sub):
        r0 = p * sub
        gtot = gates[p][2]
        outs = [[] for _ in range(n_heads)]
        for c in range(per_sub):
            c0 = c * CHUNK
            for h in range(n_heads):
                kw = k_wu[p, h]
                m16 = kw[c * dk:(c + 1) * dk, :dk].astype(BF16)
                res = _dot(jnp.concatenate([m16, qp16[p, h][c0:c0 + CHUNK]], axis=0), state[h].astype(BF16))
                outs[h].append(res[dk:] + a_wu[p, h][c0:c0 + CHUNK, dk:])
                state[h] = (state[h] * jnp.exp(gtot[c0:c0 + 1, h:h + 1]) - res[:dk]
                            + kw[c * dk:(c + 1) * dk, dk:])
        for h in range(n_heads):
            o = jnp.concatenate(outs[h], axis=0)
            zh = z_ref[r0:r0 + sub, h * dk:(h + 1) * dk]
            o_ref[r0:r0 + sub, h * dk:(h + 1) * dk] = (
                _rms(o, anw_ref[...]) * (zh * jax.nn.sigmoid(zh))).astype(o_ref.dtype)
    for h in range(n_heads):
        state_ref[h] = state[h]


def _gdn(qkv_a, z, ab, a_log, dt_bias, a_norm_w, bsz, seq):
    n, width = qkv_a.shape
    a_width = width // 3
    n_heads = a_width // A_HEAD_DIM
    tt = 512
    nt = seq // tt
    alog = jnp.zeros((1, LANES), F32).at[0, :n_heads].set(a_log)
    dtb = jnp.zeros((1, LANES), F32).at[0, :n_heads].set(dt_bias)
    kern = functools.partial(_gdn_kernel, n_heads=n_heads)
    return pl.pallas_call(
        kern,
        grid=(bsz, nt),
        in_specs=[
            pl.BlockSpec((tt, width), lambda b, i: (b * nt + i, 0)),
            pl.BlockSpec((tt, a_width), lambda b, i: (b * nt + i, 0)),
            pl.BlockSpec((tt, LANES), lambda b, i: (b * nt + i, 0)),
            pl.BlockSpec((1, LANES), lambda b, i: (0, 0)),
            pl.BlockSpec((1, LANES), lambda b, i: (0, 0)),
            pl.BlockSpec((1, A_HEAD_DIM), lambda b, i: (0, 0)),
        ],
        out_specs=pl.BlockSpec((tt, a_width), lambda b, i: (b * nt + i, 0)),
        out_shape=jax.ShapeDtypeStruct((n, a_width), BF16),
        scratch_shapes=[pltpu.VMEM((n_heads, A_HEAD_DIM, A_HEAD_DIM), F32)],
        compiler_params=pltpu.CompilerParams(
            dimension_semantics=("arbitrary", "arbitrary"), vmem_limit_bytes=48 * MIB),
        name="gdn",
    )(qkv_a, z, ab, alog, dtb, a_norm_w.reshape(1, A_HEAD_DIM))


def _attn_kernel(q_ref, *refs, n_heads, n_kb, n_q):
    n_win = n_q + n_kb - 1
    k_refs, v_refs = refs[:n_win], refs[n_win:2 * n_win]
    rev_ref, o_ref, bias_ref = refs[2 * n_win], refs[2 * n_win + 1], refs[2 * n_win + 2]
    i = pl.program_id(1)
    tq = q_ref.shape[0] // n_q
    hd = B_HEAD_DIM

    @pl.when(jnp.logical_and(pl.program_id(0) == 0, i == 0))
    def _():
        nk = n_kb * tq
        span = rev_ref.shape[2]
        row = lax.broadcasted_iota(jnp.int32, (tq, nk), 0)
        col = lax.broadcasted_iota(jnp.int32, (tq, nk), 1)
        shift = int(math.log2(CHUNK))
        qc = row >> shift
        kc = (col >> shift) - (n_kb - 1) * tq // CHUNK
        allowed = (kc <= qc) & (kc >= qc - B_PREV_CHUNKS)
        for h in range(n_heads):
            wide = jnp.broadcast_to(rev_ref[h], (tq, span))
            skew = pltpu.roll(wide, span - (tq - 1), 1, stride=1, stride_axis=0)
            bias_ref[h] = jnp.where(allowed, skew[:, :nk], NEG_BIG)
    per = LANES // hd
    lane = lax.broadcasted_iota(jnp.int32, (1, LANES), 1)
    items = [(t, h) for t in range(n_q) for h in range(n_heads)]

    def lanes_of(h):
        return slice((h // per) * LANES, (h // per + 1) * LANES)

    def in_head(h):
        hh = h % per
        return (lane >= hh * hd) & (lane < (hh + 1) * hd)

    scores = {}
    for t, h in items:
        qg = q_ref[t * tq:(t + 1) * tq, lanes_of(h)]
        qh = jnp.where(in_head(h), qg, jnp.zeros_like(qg)) * (hd ** -0.5)
        s = []
        for j in range(n_kb):
            sj = _dot_nt(qh, k_refs[t + j][:, lanes_of(h)]) + bias_ref[h, :, j * tq:(j + 1) * tq]
            if j < n_kb - 1:
                sj = jnp.where(i * n_q + t >= n_kb - 1 - j, sj, NEG_BIG)
            s.append(sj)
        scores[t, h] = s
    probs, dens = {}, {}
    for it in items:
        s = scores[it]
        top = s[0]
        for sj in s[1:]:
            top = jnp.maximum(top, sj)
        m = top.max(axis=-1, keepdims=True)
        p = [jnp.exp(sj - m) for sj in s]
        tot = p[0]
        for pj in p[1:]:
            tot = tot + pj
        probs[it] = [pj.astype(BF16) for pj in p]
        dens[it] = tot.sum(axis=-1, keepdims=True)
    outs = {}
    for t, h in items:
        acc = _dot(probs[t, h][0], v_refs[t][:, lanes_of(h)])
        for j in range(1, n_kb):
            acc = acc + _dot(probs[t, h][j], v_refs[t + j][:, lanes_of(h)])
        outs[t, h] = acc / dens[t, h]
    for t in range(n_q):
        for g in range(n_heads // per):
            out = outs[t, g * per]
            for hh in range(1, per):
                out = jnp.where(in_head(g * per + hh), outs[t, g * per + hh], out)
            o_ref[t * tq:(t + 1) * tq, g * LANES:(g + 1) * LANES] = out.astype(o_ref.dtype)


def _bias_by_offset(rel_bias, tq, n_kb):
    n_h = rel_bias.shape[0]
    back = (n_kb - 1) * tq
    nk = n_kb * tq
    span = tq + nk - 1
    lo = (nk - 1 - back) - REL_CLIP
    hi = span - lo - (2 * REL_CLIP + 1)
    by_offset = jnp.concatenate([jnp.broadcast_to(rel_bias[:, :1], (n_h, lo)), rel_bias,
                                 jnp.broadcast_to(rel_bias[:, -1:], (n_h, hi))], axis=1).astype(F32)
    padded = -(-span // LANES) * LANES
    return jnp.pad(by_offset[:, ::-1], ((0, 0), (0, padded - span)))[:, None, :]


def _band_attn(qkv_b, rel_bias, bsz, seq):
    n, width = qkv_b.shape
    b_width = width // 3
    n_heads = b_width // B_HEAD_DIM
    tq = 256
    n_kb = 3
    n_q = 2
    assert (n_kb - 1) * tq == B_PREV_CHUNKS * CHUNK
    nt = seq // tq
    steps = nt // n_q
    n_win = n_q + n_kb - 1
    rev = _bias_by_offset(rel_bias, tq, n_kb)
    kern = functools.partial(_attn_kernel, n_heads=n_heads, n_kb=n_kb, n_q=n_q)

    def kv_spec(colblk, w):
        return pl.BlockSpec(
            (tq, b_width), lambda b, i: (b * nt + jnp.maximum(i * n_q - (n_kb - 1) + w, 0), colblk))

    return pl.pallas_call(
        kern,
        grid=(bsz, steps),
        in_specs=[pl.BlockSpec((n_q * tq, b_width), lambda b, i: (b * steps + i, 0))]
        + [kv_spec(1, w) for w in range(n_win)] + [kv_spec(2, w) for w in range(n_win)]
        + [pl.BlockSpec(rev.shape, lambda b, i: (0, 0, 0))],
        out_specs=pl.BlockSpec((n_q * tq, b_width), lambda b, i: (b * steps + i, 0)),
        out_shape=jax.ShapeDtypeStruct((n, b_width), BF16),
        scratch_shapes=[pltpu.VMEM((n_heads, tq, n_kb * tq), F32)],
        compiler_params=pltpu.CompilerParams(
            dimension_semantics=("arbitrary", "arbitrary"), vmem_limit_bytes=48 * MIB),
        name="band_attn",
    )(*([qkv_b] * (1 + 2 * n_win)), rev)


def _outproj_kernel(oa_ref, ob_ref, x_ref, mod_ref, nw_ref, woa_ref, wob_ref, wr_ref, br_ref,
                    x1_ref, h2_ref, route_ref, gate_ref, cnt_ref, carry_ref, *, n_experts):
    i = pl.program_id(0)
    tm = x_ref.shape[0]
    sub = 256
    parts = [slice(r0, r0 + sub) for r0 in range(0, tm, sub)]

    @pl.when(i == 0)
    def _():
        carry_ref[...] = jnp.zeros_like(carry_ref)

    ga1 = mod_ref[0, 2:3, :]
    sh2 = mod_ref[0, 3:4, :]
    sc2 = mod_ref[0, 4:5, :]
    ys = [_dot(oa_ref[r, :], woa_ref[...]) + _dot(ob_ref[r, :], wob_ref[...]) for r in parts]
    h2s = []
    for r, y in zip(parts, ys):
        x1 = x_ref[r, :] + ga1 * _rms(y, nw_ref[1:2, :])
        x1_ref[r, :] = x1
        h2 = _rms(x1, nw_ref[2:3, :]) * (1.0 + sc2) + sh2
        h2_ref[r, :] = h2
        h2s.append(h2)

    logits = []
    for h2 in h2s:
        h_hi = h2.astype(BF16)
        h_lo = (h2 - h_hi.astype(F32)).astype(BF16)
        logits.append(_dot(h_hi, wr_ref[0]) + (_dot(h_hi, wr_ref[1]) + _dot(h_lo, wr_ref[0])) + br_ref[...])

    lane_i = lax.broadcasted_iota(jnp.int32, (sub, LANES), 1)
    lane = lane_i.astype(F32)
    lgs = [jnp.where(lane_i < n_experts, lg, -jnp.inf) for lg in logits]
    vals = [[] for _ in parts]
    idxs = [[] for _ in parts]
    for _ in range(TOP_K):
        for p in range(len(parts)):
            m = lgs[p].max(axis=-1, keepdims=True)
            idx = jnp.where(lgs[p] == m, lane, float(LANES)).min(axis=-1, keepdims=True)
            vals[p].append(m)
            idxs[p].append(idx)
            lgs[p] = jnp.where(lane == idx, -jnp.inf, lgs[p])

    row = lax.broadcasted_iota(jnp.int32, (sub, sub), 0)
    col = lax.broadcasted_iota(jnp.int32, (sub, sub), 1)
    before = (col < row).astype(BF16)
    carry = carry_ref[0:1, :]
    for p, r in enumerate(parts):
        ex = [jnp.exp(v - vals[p][0]) for v in vals[p]]
        den = ex[0]
        for e in ex[1:]:
            den = den + e
        onehot = jnp.zeros((sub, LANES), F32)
        for idx in idxs[p]:
            onehot = onehot + (lane == idx).astype(F32)
        cum = _dot(before, onehot.astype(BF16)) + carry
        carry = carry + onehot.sum(axis=0, keepdims=True)
        route = jnp.zeros((sub, LANES), F32)
        gate = jnp.zeros((sub, LANES), F32)
        for k in range(TOP_K):
            rank = jnp.where(lane == idxs[p][k], cum, 0.0).sum(axis=-1, keepdims=True)
            route = jnp.where(lane_i == k, idxs[p][k], route)
            route = jnp.where(lane_i == TOP_K + k, rank, route)
            gate = jnp.where(lane_i == k, ex[k] / den, gate)
        route_ref[r, :] = route.astype(jnp.int32)
        gate_ref[r, :] = gate
    carry_ref[...] = jnp.broadcast_to(carry, carry_ref.shape)
    cnt_ref[...] = jnp.broadcast_to(carry, cnt_ref.shape)


def _outproj(o_a, o_b, x2, mod3, norm_w, w_out_a, w_out_b, w_router, b_router, seq):
    n, d = x2.shape
    n_experts = w_router.shape[1]
    tm = 1024
    tiles_per_seq = seq // tm
    wr32 = jnp.zeros((d, LANES), F32).at[:, :n_experts].set(w_router)
    wr_hi = wr32.astype(BF16)
    wr = jnp.stack([wr_hi, (wr32 - wr_hi.astype(F32)).astype(BF16)])
    br = jnp.zeros((1, LANES), F32).at[0, :n_experts].set(b_router)
    kern = functools.partial(_outproj_kernel, n_experts=n_experts)
    aw = o_a.shape[1]
    bw = o_b.shape[1]
    return pl.pallas_call(
        kern,
        grid=(n // tm,),
        in_specs=[
            pl.BlockSpec((tm, aw), lambda i: (i, 0)),
            pl.BlockSpec((tm, bw), lambda i: (i, 0)),
            pl.BlockSpec((tm, d), lambda i: (i, 0)),
            pl.BlockSpec((1, 6, d), lambda i: (i // tiles_per_seq, 0, 0)),
            pl.BlockSpec((4, d), lambda i: (0, 0)),
            pl.BlockSpec((aw, d), lambda i: (0, 0)),
            pl.BlockSpec((bw, d), lambda i: (0, 0)),
            pl.BlockSpec((2, d, LANES), lambda i: (0, 0, 0)),
            pl.BlockSpec((1, LANES), lambda i: (0, 0)),
        ],
        out_specs=[
            pl.BlockSpec((tm, d), lambda i: (i, 0)),
            pl.BlockSpec((tm, d), lambda i: (i, 0)),
            pl.BlockSpec((tm, LANES), lambda i: (i, 0)),
            pl.BlockSpec((tm, LANES), lambda i: (i, 0)),
            pl.BlockSpec((8, LANES), lambda i: (0, 0)),
        ],
        out_shape=[
            jax.ShapeDtypeStruct((n, d), F32),
            jax.ShapeDtypeStruct((n, d), F32),
            jax.ShapeDtypeStruct((n, LANES), jnp.int32),
            jax.ShapeDtypeStruct((n, LANES), F32),
            jax.ShapeDtypeStruct((8, LANES), F32),
        ],
        scratch_shapes=[pltpu.VMEM((8, LANES), F32)],
        compiler_params=pltpu.CompilerParams(
            dimension_semantics=("arbitrary",), vmem_limit_bytes=48 * MIB),
        name="outproj_router",
    )(o_a, o_b, x2, mod3, norm_w, w_out_a, w_out_b, wr, br)


def _dispatch_kernel(pend_ref, vend_ref, nu_ref, dest_ref, h_ref, xs_hbm, stage, zeros, sem, zsem, tsem,
                     *, n_experts, blk):
    i = pl.program_id(0)
    n_tiles = pl.num_programs(0)
    tm, d = h_ref.shape
    slot = lax.rem(i, 2)
    piece = zeros.shape[0]
    shift = int(math.log2(piece))
    n_pieces = xs_hbm.shape[0] // piece

    def wait_rows(s):
        for _ in range(TOP_K):
            pltpu.make_async_copy(stage.at[s], xs_hbm.at[pl.ds(0, tm), :], sem.at[s]).wait()

    def clear(sem_ref):
        def body(p, carry):
            start = pl.multiple_of(p * piece, piece)
            pltpu.make_async_copy(zeros, xs_hbm.at[pl.ds(start, piece), :], sem_ref).start()
            return carry
        return body

    def wait_clear(sem_ref):
        def body(p, carry):
            pltpu.make_async_copy(zeros, xs_hbm.at[pl.ds(0, piece), :], sem_ref).wait()
            return carry
        return body

    @pl.when(i == 0)
    def _():
        zeros[...] = jnp.zeros_like(zeros)
        for e in range(n_experts):
            lax.fori_loop(vend_ref[e] >> shift, pend_ref[e] >> shift, clear(zsem), 0)
        lax.fori_loop((nu_ref[0] * blk) >> shift, n_pieces, clear(tsem), 0)
        for e in range(n_experts):
            lax.fori_loop(vend_ref[e] >> shift, pend_ref[e] >> shift, wait_clear(zsem), 0)

    @pl.when(i >= 2)
    def _():
        wait_rows(slot)

    stage[slot] = h_ref[...]

    for t in range(tm):
        for k in range(TOP_K):
            s = dest_ref[0, 0, t * TOP_K + k]
            pltpu.make_async_copy(stage.at[slot, pl.ds(t, 1), :], xs_hbm.at[pl.ds(s, 1), :],
                                  sem.at[slot]).start(priority=k % 2)

    @pl.when(i == n_tiles - 1)
    def _():
        wait_rows(slot)

        @pl.when(n_tiles > 1)
        def _():
            wait_rows(1 - slot)

        lax.fori_loop((nu_ref[0] * blk) >> shift, n_pieces, wait_clear(tsem), 0)


def _dispatch(h2, dest, pend, vend, n_used, cap):
    n, d = h2.shape
    n_experts = pend.shape[0]
    tm = 256
    n_tiles = n // tm
    kern = functools.partial(_dispatch_kernel, n_experts=n_experts, blk=MOE_BLOCK)
    grid_spec = pltpu.PrefetchScalarGridSpec(
        num_scalar_prefetch=3,
        grid=(n_tiles,),
        in_specs=[
            pl.BlockSpec((1, 1, tm * TOP_K), lambda i, pe, pa, nu: (i, 0, 0), memory_space=pltpu.SMEM),
            pl.BlockSpec((tm, d), lambda i, pe, pa, nu: (i, 0)),
        ],
        out_specs=pl.BlockSpec(memory_space=pl.ANY),
        scratch_shapes=[
            pltpu.VMEM((2, tm, d), F32),
            pltpu.VMEM((CLEAR_ROWS, d), F32),
            pltpu.SemaphoreType.DMA((2,)),
            pltpu.SemaphoreType.DMA(()),
            pltpu.SemaphoreType.DMA(()),
        ],
    )
    return pl.pallas_call(
        kern,
        grid_spec=grid_spec,
        out_shape=jax.ShapeDtypeStruct((cap, d), F32),
        compiler_params=pltpu.CompilerParams(
            dimension_semantics=("arbitrary",), vmem_limit_bytes=32 * MIB),
        name="dispatch",
    )(pend, vend, n_used, dest.reshape(n_tiles, 1, tm * TOP_K), h2)


def _expert_kernel(be_ref, nu_ref, nv_ref, nx_ref, xs_ref, wup_hbm, bup_ref, wdn_hbm, bdn_ref, y_ref,
                   up_stage, dn_stage, wsem, wup16, wdn16):
    j = pl.program_id(0)
    blk, d = xs_ref.shape
    f = wdn16.shape[0]
    n_valid = nv_ref[j]
    half = blk // 2

    def weight_copies(e):
        return (pltpu.make_async_copy(wup_hbm.at[e], up_stage, wsem.at[0]),
                pltpu.make_async_copy(wdn_hbm.at[e], dn_stage, wsem.at[1]))

    prev = be_ref[jnp.maximum(j - 1, 0)]
    first_of_expert = jnp.logical_and(n_valid > 0, jnp.logical_or(j == 0, be_ref[j] != prev))

    @pl.when(jnp.logical_and(j == 0, n_valid > 0))
    def _():
        for c in weight_copies(be_ref[0]):
            c.start()

    @pl.when(first_of_expert)
    def _():
        for c in weight_copies(be_ref[j]):
            c.wait()
        step = 128
        for r0 in range(0, d, step):
            wup16[r0:r0 + step, :] = up_stage[r0:r0 + step, :].astype(BF16)
        for r0 in range(0, f, step):
            wdn16[r0:r0 + step, :] = dn_stage[r0:r0 + step, :].astype(BF16)

        @pl.when(nx_ref[j] >= 0)
        def _():
            for c in weight_copies(nx_ref[j]):
                c.start()

    def mlp(rows):
        xb = xs_ref[0:rows, :].astype(BF16)
        fc = min(f, 512)
        acc = None
        for c0 in range(0, f, fc):
            glu = _dot(xb, wup16[:, c0:c0 + fc]) + bup_ref[0, :, c0:c0 + fc]
            lin = _dot(xb, wup16[:, f + c0:f + c0 + fc]) + bup_ref[0, :, f + c0:f + c0 + fc]
            glu = jnp.minimum(glu, SWIGLU_LIMIT)
            lin = jnp.clip(lin, -SWIGLU_LIMIT, SWIGLU_LIMIT)
            act = glu * jax.nn.sigmoid(SWIGLU_ALPHA * glu) * (lin + 1.0)
            part = _dot(act.astype(BF16), wdn16[c0:c0 + fc, :])
            acc = part if acc is None else acc + part
        y_ref[0:rows, :] = acc + bdn_ref[0]

    @pl.when(n_valid > half)
    def _():
        mlp(blk)

    @pl.when(jnp.logical_and(n_valid > 0, n_valid <= half))
    def _():
        mlp(half)
        y_ref[half:, :] = jnp.zeros((blk - half, d), y_ref.dtype)

    @pl.when(n_valid == 0)
    def _():
        y_ref[...] = jnp.zeros_like(y_ref)


def _experts(xs, block_e, n_used, n_valid, next_e, w_up, b_up, w_down, b_down):
    cap, d = xs.shape
    n_exp, _, f2 = w_up.shape
    f = w_down.shape[1]
    blk = MOE_BLOCK
    n_blocks = cap // blk
    grid_spec = pltpu.PrefetchScalarGridSpec(
        num_scalar_prefetch=4,
        grid=(n_blocks,),
        in_specs=[
            pl.BlockSpec((blk, d), lambda j, be, nu, nv, nx: (jnp.minimum(j, jnp.maximum(nu[0] - 1, 0)), 0)),
            pl.BlockSpec(memory_space=pl.ANY),
            pl.BlockSpec((1, 1, f2), lambda j, be, nu, nv, nx: (be[j], 0, 0)),
            pl.BlockSpec(memory_space=pl.ANY),
            pl.BlockSpec((1, 1, d), lambda j, be, nu, nv, nx: (be[j], 0, 0)),
        ],
        out_specs=pl.BlockSpec((blk, d), lambda j, be, nu, nv, nx: (j, 0)),
        scratch_shapes=[
            pltpu.VMEM((d, f2), F32),
            pltpu.VMEM((f, d), F32),
            pltpu.SemaphoreType.DMA((2,)),
            pltpu.VMEM((d, f2), BF16),
            pltpu.VMEM((f, d), BF16),
        ],
    )
    return pl.pallas_call(
        _expert_kernel,
        grid_spec=grid_spec,
        out_shape=jax.ShapeDtypeStruct((cap, d), F32),
        compiler_params=pltpu.CompilerParams(
            dimension_semantics=("arbitrary",), vmem_limit_bytes=56 * MIB),
        name="experts",
    )(block_e, n_used, n_valid, next_e, xs, w_up, b_up.reshape(n_exp, 1, f2), w_down, b_down.reshape(n_exp, 1, d))


def _combine_kernel(dest_ref, dest_next_ref, y_hbm, gate_ref, x1_ref, mod_ref, nw_ref, o_ref, buf, sem):
    i = pl.program_id(0)
    n_tiles = pl.num_programs(0)
    tm, d = x1_ref.shape
    slot = lax.rem(i, 2)

    def start_gather(idx_ref, dst_slot):
        for t in range(tm):
            for k in range(TOP_K):
                s = idx_ref[0, 0, t * TOP_K + k]
                pltpu.make_async_copy(y_hbm.at[pl.ds(s, 1), :], buf.at[dst_slot, k, pl.ds(t, 1), :],
                                      sem.at[dst_slot]).start(priority=k % 2)

    @pl.when(i == 0)
    def _():
        start_gather(dest_ref, 0)

    for nxt in range(2):
        @pl.when(jnp.logical_and(i + 1 < n_tiles, slot == 1 - nxt))
        def _():
            start_gather(dest_next_ref, nxt)

    for k in range(TOP_K):
        pltpu.make_async_copy(y_hbm.at[pl.ds(0, tm), :], buf.at[slot, k], sem.at[slot]).wait()

    gate = gate_ref[...]
    ysum = gate[:, 0:1] * buf[slot, 0]
    for k in range(1, TOP_K):
        ysum = ysum + gate[:, k:k + 1] * buf[slot, k]
    ga2 = mod_ref[0, 5:6, :]
    o_ref[...] = x1_ref[...] + ga2 * _rms(ysum, nw_ref[3:4, :])


def _combine(y_sorted, dest, gates, x1, mod3, norm_w, seq):
    n, d = x1.shape
    tm = 256
    n_tiles = n // tm
    tiles_per_seq = seq // tm
    dest3 = dest.reshape(n_tiles, 1, tm * TOP_K)
    return pl.pallas_call(
        _combine_kernel,
        grid=(n_tiles,),
        in_specs=[
            pl.BlockSpec((1, 1, tm * TOP_K), lambda i: (i, 0, 0), memory_space=pltpu.SMEM),
            pl.BlockSpec((1, 1, tm * TOP_K), lambda i: (jnp.minimum(i + 1, n_tiles - 1), 0, 0),
                         memory_space=pltpu.SMEM),
            pl.BlockSpec(memory_space=pl.ANY),
            pl.BlockSpec((tm, LANES), lambda i: (i, 0)),
            pl.BlockSpec((tm, d), lambda i: (i, 0)),
            pl.BlockSpec((1, 6, d), lambda i: (i // tiles_per_seq, 0, 0)),
            pl.BlockSpec((4, d), lambda i: (0, 0)),
        ],
        out_specs=pl.BlockSpec((tm, d), lambda i: (i, 0)),
        out_shape=jax.ShapeDtypeStruct((n, d), F32),
        scratch_shapes=[
            pltpu.VMEM((2, TOP_K, tm, d), F32),
            pltpu.SemaphoreType.DMA((2,)),
        ],
        compiler_params=pltpu.CompilerParams(
            dimension_semantics=("arbitrary",), vmem_limit_bytes=32 * MIB),
        name="combine",
    )(dest3, dest3, y_sorted, gates, x1, mod3, norm_w)


def _layer(x, c, w_ada, b_ada, norm_w, w_in, conv_w, a_log, dt_bias, a_norm_w, rel_bias, w_out,
           w_router, b_router, w_up, b_up, w_down, b_down):
    bsz, seq, d = x.shape
    n = bsz * seq
    a_width = conv_w.shape[1] // 3
    a_heads = a_log.shape[0]
    b_width = w_out.shape[0] - a_width
    n_experts = w_router.shape[1]
    off_gate = 4 * a_width
    off_b = off_gate + 2 * a_heads

    mod3 = _adaln(c, w_ada, b_ada).reshape(bsz, 6, d)
    x2 = x.reshape(n, d)

    w_main = jnp.concatenate([w_in[:, :off_gate], w_in[:, off_b:]], axis=1).astype(BF16)
    w_gate = jnp.pad(w_in[:, off_gate:off_b], ((0, 0), (0, LANES - 2 * a_heads))).astype(BF16)
    qkv_a, z_a, ab, qkv_b = _inproj(x2, mod3, norm_w, w_main, w_gate, conv_w, seq, 3 * a_width, a_width,
                                    3 * b_width)

    o_a = _gdn(qkv_a, z_a, ab, a_log, dt_bias, a_norm_w, bsz, seq)
    o_b = _band_attn(qkv_b, rel_bias, bsz, seq)

    x1, h2, route, gates, cnt = _outproj(o_a, o_b, x2, mod3, norm_w, w_out[:a_width].astype(BF16),
                                         w_out[a_width:].astype(BF16), w_router, b_router, seq)

    top_idx = route[:, :TOP_K]
    rank = route[:, TOP_K:2 * TOP_K]
    counts = cnt[0, :n_experts].astype(jnp.int32)
    padded = (counts + MOE_BLOCK - 1) // MOE_BLOCK * MOE_BLOCK
    pend = jnp.cumsum(padded)
    pstart = pend - padded
    expert_ids = jnp.arange(n_experts, dtype=jnp.int32)
    dest = jnp.sum(jnp.where(top_idx[..., None] == expert_ids, pstart, 0), axis=-1) + rank
    n_assign = n * TOP_K
    n_blocks = -(-n_assign // MOE_BLOCK) + n_experts
    cap = n_blocks * MOE_BLOCK
    block_start = jnp.arange(n_blocks, dtype=jnp.int32) * MOE_BLOCK
    block_e = jnp.minimum(jnp.sum(pend[None, :] <= block_start[:, None], axis=1), n_experts - 1).astype(jnp.int32)
    n_used = (pend[-1:] // MOE_BLOCK).astype(jnp.int32)
    dest = dest.astype(jnp.int32)

    vend = (pstart + counts).astype(jnp.int32)
    block_vend = jnp.sum(jnp.where(block_e[:, None] == expert_ids, vend, 0), axis=1)
    n_valid = jnp.clip(block_vend - block_start, 0, MOE_BLOCK).astype(jnp.int32)

    xs = _dispatch(h2, dest, pend.astype(jnp.int32), vend, n_used, cap)
    blocks = jnp.arange(n_blocks, dtype=jnp.int32)
    later = (blocks[None, :] > blocks[:, None]) & (block_e[None, :] != block_e[:, None]) & (n_valid[None, :] > 0)
    first_later = jnp.min(jnp.where(later, blocks[None, :], n_blocks), axis=1)
    next_e = jnp.sum(jnp.where(blocks[None, :] == first_later[:, None], block_e[None, :] + 1, 0), axis=1) - 1
    y_sorted = _experts(xs, block_e, n_used, n_valid, next_e.astype(jnp.int32), w_up, b_up, w_down, b_down)
    out = _combine(y_sorted, dest, gates, x1, mod3, norm_w, seq)
    return out.reshape(bsz, seq, d)


def kernel(x, c, w_ada, b_ada, norm_w, w_in, conv_w, a_log, dt_bias, a_norm_w, rel_bias, w_out,
           w_router, b_router, w_up, b_up, w_down, b_down):
    for l in range(w_ada.shape[0]):
        x = _layer(x, c, w_ada[l], b_ada[l], norm_w[l], w_in[l], conv_w[l], a_log[l], dt_bias[l],
                   a_norm_w[l], rel_bias[l], w_out[l], w_router[l], b_router[l], w_up[l], b_up[l],
                   w_down[l], b_down[l])
    return x
```

```python
import functools
import math

import jax
import jax.numpy as jnp
from jax import lax
from jax.experimental import pallas as pl
from jax.experimental.pallas import tpu as pltpu

F32 = jnp.float32
BF16 = jnp.bfloat16
HIGHEST = lax.Precision.HIGHEST

EPS = 1e-6
CHUNK = 64
CONV_K = 4
A_HEAD_DIM = 128
B_HEAD_DIM = 64
B_PREV_CHUNKS = 8
REL_CLIP = 128
TOP_K = 4
SWIGLU_ALPHA = 1.702
SWIGLU_LIMIT = 7.0
MOE_BLOCK = 512
CLEAR_ROWS = 64
LANES = 128
NEG_BIG = -1e30
LOG2E = 1.4426950408889634

MIB = 1024 * 1024


def _dot(a, b):
    return jnp.dot(a, b, preferred_element_type=F32)


def _dot_nt(a, b):
    return lax.dot_general(a, b, (((1,), (1,)), ((), ())), preferred_element_type=F32)


def _dot_exact(a, b):
    return jnp.dot(a, b, precision=HIGHEST, preferred_element_type=F32)


def _rms(x, w):
    return x * lax.rsqrt(jnp.mean(x * x, axis=-1, keepdims=True) + EPS) * w


def _adaln_kernel(ct_ref, w_ref, b_ref, o_ref, *, bsz):
    ct = ct_ref[...]
    cs = ct * jax.nn.sigmoid(ct)
    w = w_ref[...]
    rows = [jnp.sum(cs[:, b:b + 1] * w, axis=0, keepdims=True) for b in range(bsz)]
    rows.append(jnp.zeros((o_ref.shape[0] - bsz, w.shape[1]), F32))
    o_ref[...] = jnp.concatenate(rows, axis=0) + b_ref[...]


def _adaln(c, w, b):
    bsz, d = c.shape
    n_out = w.shape[1]
    rows = 8
    ct = jnp.pad(c.T, ((0, 0), (0, LANES - bsz)))
    tn = 512
    out = pl.pallas_call(
        functools.partial(_adaln_kernel, bsz=bsz),
        grid=(n_out // tn,),
        in_specs=[
            pl.BlockSpec((d, LANES), lambda j: (0, 0)),
            pl.BlockSpec((d, tn), lambda j: (0, j)),
            pl.BlockSpec((1, tn), lambda j: (0, j)),
        ],
        out_specs=pl.BlockSpec((rows, tn), lambda j: (0, j)),
        out_shape=jax.ShapeDtypeStruct((rows, n_out), F32),
        name="adaln",
    )(ct, w, b.reshape(1, n_out))
    return out[:bsz]


def _inproj_kernel(x_ref, mod_ref, nw_ref, wm_ref, wg_ref, cw_ref, qkva_ref, z_ref, ab_ref, qkvb_ref, raw_ref,
                   *, a_qkv, a_z, tiles_per_seq):
    i = pl.program_id(0)
    tm = x_ref.shape[0]
    dk = A_HEAD_DIM
    a_width = a_qkv // 3
    halo = 8
    tn = 512

    @pl.when(lax.rem(i, tiles_per_seq) == 0)
    def _():
        raw_ref[0:halo, :] = jnp.zeros((halo, a_qkv), F32)

    x = x_ref[...]
    sh = mod_ref[0, 0:1, :]
    sc = mod_ref[0, 1:2, :]
    h = _rms(x, nw_ref[0:1, :]) * (1.0 + sc) + sh
    hb = h.astype(BF16)

    def conv_silu(c0):
        acc = cw_ref[CONV_K - 1:CONV_K, c0:c0 + dk] * raw_ref[halo:halo + tm, c0:c0 + dk]
        for j in range(CONV_K - 1):
            start = halo - (CONV_K - 1) + j
            acc = acc + cw_ref[j:j + 1, c0:c0 + dk] * raw_ref[start:start + tm, c0:c0 + dk]
        y = acc * jax.nn.sigmoid(acc)
        if c0 < 2 * a_width:
            y = y * lax.rsqrt(jnp.sum(y * y, axis=-1, keepdims=True) + EPS)
        if c0 < a_width:
            y = y * (dk ** -0.5)
        qkva_ref[:, c0:c0 + dk] = y

    for c0 in range(0, a_qkv, tn):
        raw_ref[halo:halo + tm, c0:c0 + tn] = _dot(hb, wm_ref[:, c0:c0 + tn])
    slabs = list(range(0, a_qkv, dk))
    other = [("z", c0) for c0 in range(0, a_z, tn)] + [("b", c0) for c0 in range(0, qkvb_ref.shape[1], tn)]
    per = -(-len(slabs) // len(other))
    off = a_qkv + a_z
    for n_o, (kind, c0) in enumerate(other):
        for c in slabs[n_o * per:(n_o + 1) * per]:
            conv_silu(c)
        if kind == "z":
            z_ref[:, c0:c0 + tn] = _dot(hb, wm_ref[:, a_qkv + c0:a_qkv + c0 + tn])
        else:
            qkvb_ref[:, c0:c0 + tn] = _dot(hb, wm_ref[:, off + c0:off + c0 + tn]).astype(BF16)
    ab_ref[...] = _dot(hb, wg_ref[...])
    raw_ref[0:halo, :] = raw_ref[tm:tm + halo, :]


def _inproj(x2, mod3, norm_w, w_main, w_gate, conv_w, seq, a_qkv, a_z, b_qkv):
    n, d = x2.shape
    tm = 512
    tiles_per_seq = seq // tm
    kern = functools.partial(_inproj_kernel, a_qkv=a_qkv, a_z=a_z, tiles_per_seq=tiles_per_seq)
    return pl.pallas_call(
        kern,
        grid=(n // tm,),
        in_specs=[
            pl.BlockSpec((tm, d), lambda i: (i, 0)),
            pl.BlockSpec((1, 6, d), lambda i: (i // tiles_per_seq, 0, 0)),
            pl.BlockSpec((4, d), lambda i: (0, 0)),
            pl.BlockSpec(w_main.shape, lambda i: (0, 0)),
            pl.BlockSpec(w_gate.shape, lambda i: (0, 0)),
            pl.BlockSpec(conv_w.shape, lambda i: (0, 0)),
        ],
        out_specs=[
            pl.BlockSpec((tm, a_qkv), lambda i: (i, 0)),
            pl.BlockSpec((tm, a_z), lambda i: (i, 0)),
            pl.BlockSpec((tm, LANES), lambda i: (i, 0)),
            pl.BlockSpec((tm, b_qkv), lambda i: (i, 0)),
        ],
        out_shape=[
            jax.ShapeDtypeStruct((n, a_qkv), F32),
            jax.ShapeDtypeStruct((n, a_z), F32),
            jax.ShapeDtypeStruct((n, LANES), F32),
            jax.ShapeDtypeStruct((n, b_qkv), BF16),
        ],
        scratch_shapes=[pltpu.VMEM((tm + 8, a_qkv), F32)],
        compiler_params=pltpu.CompilerParams(
            dimension_semantics=("arbitrary",), vmem_limit_bytes=48 * MIB),
        name="inproj",
    )(x2, mod3, norm_w, w_main, w_gate, conv_w)


def _gdn_kernel(qkv_ref, z_ref, ab_ref, alog_ref, dtb_ref, anw_ref, o_ref, state_ref, *, n_heads):
    i = pl.program_id(1)
    tt = qkv_ref.shape[0]
    width = qkv_ref.shape[1]
    a_width = width // 3
    dk = A_HEAD_DIM
    sub = 2 * CHUNK
    per_sub = sub // CHUNK

    @pl.when(i == 0)
    def _():
        state_ref[...] = jnp.zeros_like(state_ref)

    row = lax.broadcasted_iota(jnp.int32, (sub, sub), 0)
    col = lax.broadcasted_iota(jnp.int32, (sub, sub), 1)
    shift = int(math.log2(CHUNK))
    same = (row >> shift) == (col >> shift)
    tri_incl = same & (col <= row)
    tri_strict = same & (col < row)
    eye = (row == col).astype(F32)
    pair = (row >> 1) == (col >> 1)
    couples = [((row >> (lv + 1)) == (col >> (lv + 1))) & ((row >> lv) != (col >> lv))
               for lv in range(1, shift)]
    col_chunk = lax.broadcasted_iota(jnp.int32, (dk, sub), 1) >> shift
    tri_incl_f = tri_incl.astype(F32)
    same_f = same.astype(F32)

    n_sub = tt // sub
    gates = []
    for p in range(n_sub):
        r0 = p * sub
        ab = ab_ref[r0:r0 + sub, :]
        gfull = -jnp.exp(alog_ref[...]) * jax.nn.softplus(ab + dtb_ref[...])
        bfull = jax.nn.sigmoid(ab)
        gcum = _dot_exact(tri_incl_f, gfull)
        gtot = _dot_exact(same_f, gfull)
        gates.append((gcum, gcum.T, gtot, bfull))

    items = [(p, h) for p in range(n_sub) for h in range(n_heads)]
    pre = {}
    for p, h in items:
        r0 = p * sub
        gcum, gcum_t, gtot, bfull = gates[p]
        gc = gcum[:, h:h + 1]
        beta = bfull[:, n_heads + h:n_heads + h + 1]
        q = qkv_ref[r0:r0 + sub, h * dk:(h + 1) * dk]
        k = qkv_ref[r0:r0 + sub, a_width + h * dk:a_width + (h + 1) * dk]
        v = qkv_ref[r0:r0 + sub, 2 * a_width + h * dk:2 * a_width + (h + 1) * dk]
        decay = jnp.exp(jnp.where(tri_incl, gc - gcum_t[h:h + 1, :], -jnp.inf))
        kb = k * beta
        k16 = k.astype(BF16)
        lower = jnp.where(tri_strict, _dot_nt(kb.astype(BF16), k16) * decay, 0.0)
        attn16 = (_dot_nt(q.astype(BF16), k16) * decay).astype(BF16)
        eg = jnp.exp(gc)
        rhs = jnp.concatenate([kb * eg, v * beta], axis=1).astype(BF16)
        kd_t = (k * jnp.exp(gtot[:, h:h + 1] - gc)).T
        kd_st = jnp.concatenate([jnp.where(col_chunk == c, kd_t, 0.0) for c in range(per_sub)],
                                axis=0).astype(BF16)
        pre[p, h] = dict(lower=lower, attn16=attn16, rhs=rhs, kd_st=kd_st, qeg=q * eg)

    tmat = {it: eye - jnp.where(pair, pre[it]["lower"], 0.0) for it in items}
    for couple in couples:
        t16 = {it: tmat[it].astype(BF16) for it in items}
        half = {it: _dot(t16[it], jnp.where(couple, pre[it]["lower"], 0.0).astype(BF16)) for it in items}
        tmat = {it: tmat[it] - _dot(half[it].astype(BF16), t16[it]) for it in items}

    wu16 = {it: _dot(tmat[it].astype(BF16), pre[it]["rhs"]).astype(BF16) for it in items}
    a_wu = {it: _dot(pre[it]["attn16"], wu16[it]) for it in items}
    k_wu = {it: _dot(pre[it]["kd_st"], wu16[it]) for it in items}
    qp16 = {it: (pre[it]["qeg"] - a_wu[it][:, :dk]).astype(BF16) for it in items}

    state = [state_ref[h] for h in range(n_heads)]
    for p in range(n_sub):
        r0 = p * sub
        gtot = gates[p][2]
        outs = [[] for _ in range(n_heads)]
        for c in range(per_sub):
            c0 = c * CHUNK
            for h in range(n_heads):
                kw = k_wu[p, h]
                m16 = kw[c * dk:(c + 1) * dk, :dk].astype(BF16)
                res = _dot(jnp.concatenate([m16, qp16[p, h][c0:c0 + CHUNK]], axis=0), state[h].astype(BF16))
                outs[h].append(res[dk:] + a_wu[p, h][c0:c0 + CHUNK, dk:])
                state[h] = (state[h] * jnp.exp(gtot[c0:c0 + 1, h:h + 1]) - res[:dk]
                            + kw[c * dk:(c + 1) * dk, dk:])
        for h in range(n_heads):
            o = jnp.concatenate(outs[h], axis=0)
            zh = z_ref[r0:r0 + sub, h * dk:(h + 1) * dk]
            o_ref[r0:r0 + sub, h * dk:(h + 1) * dk] = (
                _rms(o, anw_ref[...]) * (zh * jax.nn.sigmoid(zh))).astype(o_ref.dtype)
    for h in range(n_heads):
        state_ref[h] = state[h]


def _gdn(qkv_a, z, ab, a_log, dt_bias, a_norm_w, bsz, seq):
    n, width = qkv_a.shape
    a_width = width // 3
    n_heads = a_width // A_HEAD_DIM
    tt = 512
    nt = seq // tt
    alog = jnp.zeros((1, LANES), F32).at[0, :n_heads].set(a_log)
    dtb = jnp.zeros((1, LANES), F32).at[0, :n_heads].set(dt_bias)
    kern = functools.partial(_gdn_kernel, n_heads=n_heads)
    return pl.pallas_call(
        kern,
        grid=(bsz, nt),
        in_specs=[
            pl.BlockSpec((tt, width), lambda b, i: (b * nt + i, 0)),
            pl.BlockSpec((tt, a_width), lambda b, i: (b * nt + i, 0)),
            pl.BlockSpec((tt, LANES), lambda b, i: (b * nt + i, 0)),
            pl.BlockSpec((1, LANES), lambda b, i: (0, 0)),
            pl.BlockSpec((1, LANES), lambda b, i: (0, 0)),
            pl.BlockSpec((1, A_HEAD_DIM), lambda b, i: (0, 0)),
        ],
        out_specs=pl.BlockSpec((tt, a_width), lambda b, i: (b * nt + i, 0)),
        out_shape=jax.ShapeDtypeStruct((n, a_width), BF16),
        scratch_shapes=[pltpu.VMEM((n_heads, A_HEAD_DIM, A_HEAD_DIM), F32)],
        compiler_params=pltpu.CompilerParams(
            dimension_semantics=("arbitrary", "arbitrary"), vmem_limit_bytes=48 * MIB),
        name="gdn",
    )(qkv_a, z, ab, alog, dtb, a_norm_w.reshape(1, A_HEAD_DIM))


def _attn_kernel(q_ref, *refs, n_heads, n_kb, n_q):
    n_win = n_q + n_kb - 1
    k_refs, v_refs = refs[:n_win], refs[n_win:2 * n_win]
    rev_ref, o_ref, bias_ref = refs[2 * n_win], refs[2 * n_win + 1], refs[2 * n_win + 2]
    i = pl.program_id(1)
    tq = q_ref.shape[0] // n_q
    hd = B_HEAD_DIM

    @pl.when(jnp.logical_and(pl.program_id(0) == 0, i == 0))
    def _():
        nk = n_kb * tq
        span = rev_ref.shape[2]
        row = lax.broadcasted_iota(jnp.int32, (tq, nk), 0)
        col = lax.broadcasted_iota(jnp.int32, (tq, nk), 1)
        shift = int(math.log2(CHUNK))
        qc = row >> shift
        kc = (col >> shift) - (n_kb - 1) * tq // CHUNK
        allowed = (kc <= qc) & (kc >= qc - B_PREV_CHUNKS)
        for h in range(n_heads):
            wide = jnp.broadcast_to(rev_ref[h], (tq, span))
            skew = pltpu.roll(wide, span - (tq - 1), 1, stride=1, stride_axis=0)
            bias_ref[h] = jnp.where(allowed, skew[:, :nk] * LOG2E, NEG_BIG)
    per = LANES // hd
    lane = lax.broadcasted_iota(jnp.int32, (1, LANES), 1)
    items = [(t, h) for t in range(n_q) for h in range(n_heads)]

    def lanes_of(h):
        return slice((h // per) * LANES, (h // per + 1) * LANES)

    def in_head(h):
        hh = h % per
        return (lane >= hh * hd) & (lane < (hh + 1) * hd)

    scores = {}
    for t, h in items:
        qg = q_ref[t * tq:(t + 1) * tq, lanes_of(h)]
        qh = (jnp.where(in_head(h), qg, jnp.zeros_like(qg)).astype(F32) * (hd ** -0.5 * LOG2E)).astype(BF16)
        s = []
        for j in range(n_kb):
            sj = _dot_nt(qh, k_refs[t + j][:, lanes_of(h)]) + bias_ref[h, :, j * tq:(j + 1) * tq]
            if j < n_kb - 1:
                sj = jnp.where(i * n_q + t >= n_kb - 1 - j, sj, NEG_BIG)
            s.append(sj)
        scores[t, h] = s
    probs, dens = {}, {}
    for it in items:
        s = scores[it]
        top = s[0]
        for sj in s[1:]:
            top = jnp.maximum(top, sj)
        m = top.max(axis=-1, keepdims=True)
        p = [jnp.exp2(sj - m) for sj in s]
        tot = p[0]
        for pj in p[1:]:
            tot = tot + pj
        probs[it] = [pj.astype(BF16) for pj in p]
        dens[it] = tot.sum(axis=-1, keepdims=True)
    outs = {}
    for t, h in items:
        acc = _dot(probs[t, h][0], v_refs[t][:, lanes_of(h)])
        for j in range(1, n_kb):
            acc = acc + _dot(probs[t, h][j], v_refs[t + j][:, lanes_of(h)])
        outs[t, h] = acc / dens[t, h]
    for t in range(n_q):
        for g in range(n_heads // per):
            out = outs[t, g * per]
            for hh in range(1, per):
                out = jnp.where(in_head(g * per + hh), outs[t, g * per + hh], out)
            o_ref[t * tq:(t + 1) * tq, g * LANES:(g + 1) * LANES] = out.astype(o_ref.dtype)


def _bias_by_offset(rel_bias, tq, n_kb):
    n_h = rel_bias.shape[0]
    back = (n_kb - 1) * tq
    nk = n_kb * tq
    span = tq + nk - 1
    lo = (nk - 1 - back) - REL_CLIP
    hi = span - lo - (2 * REL_CLIP + 1)
    by_offset = jnp.concatenate([jnp.broadcast_to(rel_bias[:, :1], (n_h, lo)), rel_bias,
                                 jnp.broadcast_to(rel_bias[:, -1:], (n_h, hi))], axis=1).astype(F32)
    padded = -(-span // LANES) * LANES
    return jnp.pad(by_offset[:, ::-1], ((0, 0), (0, padded - span)))[:, None, :]


def _band_attn(qkv_b, rel_bias, bsz, seq):
    n, width = qkv_b.shape
    b_width = width // 3
    n_heads = b_width // B_HEAD_DIM
    tq = 256
    n_kb = 3
    n_q = 2
    assert (n_kb - 1) * tq == B_PREV_CHUNKS * CHUNK
    nt = seq // tq
    steps = nt // n_q
    n_win = n_q + n_kb - 1
    rev = _bias_by_offset(rel_bias, tq, n_kb)
    kern = functools.partial(_attn_kernel, n_heads=n_heads, n_kb=n_kb, n_q=n_q)

    def kv_spec(colblk, w):
        return pl.BlockSpec(
            (tq, b_width), lambda b, i: (b * nt + jnp.maximum(i * n_q - (n_kb - 1) + w, 0), colblk))

    return pl.pallas_call(
        kern,
        grid=(bsz, steps),
        in_specs=[pl.BlockSpec((n_q * tq, b_width), lambda b, i: (b * steps + i, 0))]
        + [kv_spec(1, w) for w in range(n_win)] + [kv_spec(2, w) for w in range(n_win)]
        + [pl.BlockSpec(rev.shape, lambda b, i: (0, 0, 0))],
        out_specs=pl.BlockSpec((n_q * tq, b_width), lambda b, i: (b * steps + i, 0)),
        out_shape=jax.ShapeDtypeStruct((n, b_width), BF16),
        scratch_shapes=[pltpu.VMEM((n_heads, tq, n_kb * tq), F32)],
        compiler_params=pltpu.CompilerParams(
            dimension_semantics=("arbitrary", "arbitrary"), vmem_limit_bytes=48 * MIB),
        name="band_attn",
    )(*([qkv_b] * (1 + 2 * n_win)), rev)


def _outproj_kernel(oa_ref, ob_ref, x_ref, mod_ref, nw_ref, woa_ref, wob_ref, wr_ref, br_ref,
                    x1_ref, h2_ref, route_ref, gate_ref, cnt_ref, carry_ref, *, n_experts):
    i = pl.program_id(0)
    tm = x_ref.shape[0]
    sub = 256
    parts = [slice(r0, r0 + sub) for r0 in range(0, tm, sub)]

    @pl.when(i == 0)
    def _():
        carry_ref[...] = jnp.zeros_like(carry_ref)

    ga1 = mod_ref[0, 2:3, :]
    sh2 = mod_ref[0, 3:4, :]
    sc2 = mod_ref[0, 4:5, :]
    ys = [_dot(oa_ref[r, :], woa_ref[...]) + _dot(ob_ref[r, :], wob_ref[...]) for r in parts]
    h2s = []
    for r, y in zip(parts, ys):
        x1 = x_ref[r, :] + ga1 * _rms(y, nw_ref[1:2, :])
        x1_ref[r, :] = x1
        h2 = _rms(x1, nw_ref[2:3, :]) * (1.0 + sc2) + sh2
        h2_ref[r, :] = h2
        h2s.append(h2)

    logits = []
    for h2 in h2s:
        h_hi = h2.astype(BF16)
        h_lo = (h2 - h_hi.astype(F32)).astype(BF16)
        logits.append(_dot(h_hi, wr_ref[0]) + (_dot(h_hi, wr_ref[1]) + _dot(h_lo, wr_ref[0])) + br_ref[...])

    lane_i = lax.broadcasted_iota(jnp.int32, (sub, LANES), 1)
    lane = lane_i.astype(F32)
    lgs = [jnp.where(lane_i < n_experts, lg, -jnp.inf) for lg in logits]
    vals = [[] for _ in parts]
    idxs = [[] for _ in parts]
    for _ in range(TOP_K):
        for p in range(len(parts)):
            m = lgs[p].max(axis=-1, keepdims=True)
            idx = jnp.where(lgs[p] == m, lane, float(LANES)).min(axis=-1, keepdims=True)
            vals[p].append(m)
            idxs[p].append(idx)
            lgs[p] = jnp.where(lane == idx, -jnp.inf, lgs[p])

    row = lax.broadcasted_iota(jnp.int32, (sub, sub), 0)
    col = lax.broadcasted_iota(jnp.int32, (sub, sub), 1)
    before = (col < row).astype(BF16)
    carry = carry_ref[0:1, :]
    for p, r in enumerate(parts):
        ex = [jnp.exp(v - vals[p][0]) for v in vals[p]]
        den = ex[0]
        for e in ex[1:]:
            den = den + e
        onehot = jnp.zeros((sub, LANES), F32)
        for idx in idxs[p]:
            onehot = onehot + (lane == idx).astype(F32)
        cum = _dot(before, onehot.astype(BF16)) + carry
        carry = carry + onehot.sum(axis=0, keepdims=True)
        route = jnp.zeros((sub, LANES), F32)
        gate = jnp.zeros((sub, LANES), F32)
        for k in range(TOP_K):
            rank = jnp.where(lane == idxs[p][k], cum, 0.0).sum(axis=-1, keepdims=True)
            route = jnp.where(lane_i == k, idxs[p][k], route)
            route = jnp.where(lane_i == TOP_K + k, rank, route)
            gate = jnp.where(lane_i == k, ex[k] / den, gate)
        route_ref[r, :] = route.astype(jnp.int32)
        gate_ref[r, :] = gate
    carry_ref[...] = jnp.broadcast_to(carry, carry_ref.shape)
    cnt_ref[...] = jnp.broadcast_to(carry, cnt_ref.shape)


def _outproj(o_a, o_b, x2, mod3, norm_w, w_out_a, w_out_b, w_router, b_router, seq):
    n, d = x2.shape
    n_experts = w_router.shape[1]
    tm = 1024
    tiles_per_seq = seq // tm
    wr32 = jnp.zeros((d, LANES), F32).at[:, :n_experts].set(w_router)
    wr_hi = wr32.astype(BF16)
    wr = jnp.stack([wr_hi, (wr32 - wr_hi.astype(F32)).astype(BF16)])
    br = jnp.zeros((1, LANES), F32).at[0, :n_experts].set(b_router)
    kern = functools.partial(_outproj_kernel, n_experts=n_experts)
    aw = o_a.shape[1]
    bw = o_b.shape[1]
    return pl.pallas_call(
        kern,
        grid=(n // tm,),
        in_specs=[
            pl.BlockSpec((tm, aw), lambda i: (i, 0)),
            pl.BlockSpec((tm, bw), lambda i: (i, 0)),
            pl.BlockSpec((tm, d), lambda i: (i, 0)),
            pl.BlockSpec((1, 6, d), lambda i: (i // tiles_per_seq, 0, 0)),
            pl.BlockSpec((4, d), lambda i: (0, 0)),
            pl.BlockSpec((aw, d), lambda i: (0, 0)),
            pl.BlockSpec((bw, d), lambda i: (0, 0)),
            pl.BlockSpec((2, d, LANES), lambda i: (0, 0, 0)),
            pl.BlockSpec((1, LANES), lambda i: (0, 0)),
        ],
        out_specs=[
            pl.BlockSpec((tm, d), lambda i: (i, 0)),
            pl.BlockSpec((tm, d), lambda i: (i, 0)),
            pl.BlockSpec((tm, LANES), lambda i: (i, 0)),
            pl.BlockSpec((tm, LANES), lambda i: (i, 0)),
            pl.BlockSpec((8, LANES), lambda i: (0, 0)),
        ],
        out_shape=[
            jax.ShapeDtypeStruct((n, d), F32),
            jax.ShapeDtypeStruct((n, d), F32),
            jax.ShapeDtypeStruct((n, LANES), jnp.int32),
            jax.ShapeDtypeStruct((n, LANES), F32),
            jax.ShapeDtypeStruct((8, LANES), F32),
        ],
        scratch_shapes=[pltpu.VMEM((8, LANES), F32)],
        compiler_params=pltpu.CompilerParams(
            dimension_semantics=("arbitrary",), vmem_limit_bytes=48 * MIB),
        name="outproj_router",
    )(o_a, o_b, x2, mod3, norm_w, w_out_a, w_out_b, wr, br)


def _dispatch_kernel(pend_ref, vend_ref, nu_ref, dest_ref, h_ref, xs_hbm, stage, zeros, sem, zsem, tsem,
                     *, n_experts, blk):
    i = pl.program_id(0)
    n_tiles = pl.num_programs(0)
    tm, d = h_ref.shape
    slot = lax.rem(i, 2)
    piece = zeros.shape[0]
    shift = int(math.log2(piece))
    n_pieces = xs_hbm.shape[0] // piece

    def wait_rows(s):
        for _ in range(TOP_K):
            pltpu.make_async_copy(stage.at[s], xs_hbm.at[pl.ds(0, tm), :], sem.at[s]).wait()

    def clear(sem_ref):
        def body(p, carry):
            start = pl.multiple_of(p * piece, piece)
            pltpu.make_async_copy(zeros, xs_hbm.at[pl.ds(start, piece), :], sem_ref).start()
            return carry
        return body

    def wait_clear(sem_ref):
        def body(p, carry):
            pltpu.make_async_copy(zeros, xs_hbm.at[pl.ds(0, piece), :], sem_ref).wait()
            return carry
        return body

    @pl.when(i == 0)
    def _():
        zeros[...] = jnp.zeros_like(zeros)
        for e in range(n_experts):
            lax.fori_loop(vend_ref[e] >> shift, pend_ref[e] >> shift, clear(zsem), 0)
        lax.fori_loop((nu_ref[0] * blk) >> shift, n_pieces, clear(tsem), 0)
        for e in range(n_experts):
            lax.fori_loop(vend_ref[e] >> shift, pend_ref[e] >> shift, wait_clear(zsem), 0)

    @pl.when(i >= 2)
    def _():
        wait_rows(slot)

    stage[slot] = h_ref[...]

    for t in range(tm):
        for k in range(TOP_K):
            s = dest_ref[0, 0, t * TOP_K + k]
            pltpu.make_async_copy(stage.at[slot, pl.ds(t, 1), :], xs_hbm.at[pl.ds(s, 1), :],
                                  sem.at[slot]).start(priority=k % 2)

    @pl.when(i == n_tiles - 1)
    def _():
        wait_rows(slot)

        @pl.when(n_tiles > 1)
        def _():
            wait_rows(1 - slot)

        lax.fori_loop((nu_ref[0] * blk) >> shift, n_pieces, wait_clear(tsem), 0)


def _dispatch(h2, dest, pend, vend, n_used, cap):
    n, d = h2.shape
    n_experts = pend.shape[0]
    tm = 256
    n_tiles = n // tm
    kern = functools.partial(_dispatch_kernel, n_experts=n_experts, blk=MOE_BLOCK)
    grid_spec = pltpu.PrefetchScalarGridSpec(
        num_scalar_prefetch=3,
        grid=(n_tiles,),
        in_specs=[
            pl.BlockSpec((1, 1, tm * TOP_K), lambda i, pe, pa, nu: (i, 0, 0), memory_space=pltpu.SMEM),
            pl.BlockSpec((tm, d), lambda i, pe, pa, nu: (i, 0)),
        ],
        out_specs=pl.BlockSpec(memory_space=pl.ANY),
        scratch_shapes=[
            pltpu.VMEM((2, tm, d), F32),
            pltpu.VMEM((CLEAR_ROWS, d), F32),
            pltpu.SemaphoreType.DMA((2,)),
            pltpu.SemaphoreType.DMA(()),
            pltpu.SemaphoreType.DMA(()),
        ],
    )
    return pl.pallas_call(
        kern,
        grid_spec=grid_spec,
        out_shape=jax.ShapeDtypeStruct((cap, d), F32),
        compiler_params=pltpu.CompilerParams(
            dimension_semantics=("arbitrary",), vmem_limit_bytes=32 * MIB),
        name="dispatch",
    )(pend, vend, n_used, dest.reshape(n_tiles, 1, tm * TOP_K), h2)


def _expert_kernel(be_ref, nu_ref, nv_ref, nx_ref, xs_ref, wup_hbm, bup_ref, wdn_hbm, bdn_ref, y_ref,
                   up_stage, dn_stage, wsem, wup16, wdn16):
    j = pl.program_id(0)
    blk, d = xs_ref.shape
    f = wdn16.shape[0]
    n_valid = nv_ref[j]
    half = blk // 2

    def weight_copies(e):
        return (pltpu.make_async_copy(wup_hbm.at[e], up_stage, wsem.at[0]),
                pltpu.make_async_copy(wdn_hbm.at[e], dn_stage, wsem.at[1]))

    prev = be_ref[jnp.maximum(j - 1, 0)]
    first_of_expert = jnp.logical_and(n_valid > 0, jnp.logical_or(j == 0, be_ref[j] != prev))

    @pl.when(jnp.logical_and(j == 0, n_valid > 0))
    def _():
        for c in weight_copies(be_ref[0]):
            c.start()

    @pl.when(first_of_expert)
    def _():
        for c in weight_copies(be_ref[j]):
            c.wait()
        step = 128
        for r0 in range(0, d, step):
            wup16[r0:r0 + step, :] = up_stage[r0:r0 + step, :].astype(BF16)
        for r0 in range(0, f, step):
            wdn16[r0:r0 + step, :] = dn_stage[r0:r0 + step, :].astype(BF16)

        @pl.when(nx_ref[j] >= 0)
        def _():
            for c in weight_copies(nx_ref[j]):
                c.start()

    def mlp(rows):
        xb = xs_ref[0:rows, :].astype(BF16)
        fc = min(f, 512)
        acc = None
        for c0 in range(0, f, fc):
            glu = _dot(xb, wup16[:, c0:c0 + fc]) + bup_ref[0, :, c0:c0 + fc]
            lin = _dot(xb, wup16[:, f + c0:f + c0 + fc]) + bup_ref[0, :, f + c0:f + c0 + fc]
            glu = jnp.minimum(glu, SWIGLU_LIMIT)
            lin = jnp.clip(lin, -SWIGLU_LIMIT, SWIGLU_LIMIT)
            act = glu * jax.nn.sigmoid(SWIGLU_ALPHA * glu) * (lin + 1.0)
            part = _dot(act.astype(BF16), wdn16[c0:c0 + fc, :])
            acc = part if acc is None else acc + part
        y_ref[0:rows, :] = acc + bdn_ref[0]

    @pl.when(n_valid > half)
    def _():
        mlp(blk)

    @pl.when(jnp.logical_and(n_valid > 0, n_valid <= half))
    def _():
        mlp(half)
        y_ref[half:, :] = jnp.zeros((blk - half, d), y_ref.dtype)

    @pl.when(n_valid == 0)
    def _():
        y_ref[...] = jnp.zeros_like(y_ref)


def _experts(xs, block_e, n_used, n_valid, next_e, w_up, b_up, w_down, b_down):
    cap, d = xs.shape
    n_exp, _, f2 = w_up.shape
    f = w_down.shape[1]
    blk = MOE_BLOCK
    n_blocks = cap // blk
    grid_spec = pltpu.PrefetchScalarGridSpec(
        num_scalar_prefetch=4,
        grid=(n_blocks,),
        in_specs=[
            pl.BlockSpec((blk, d), lambda j, be, nu, nv, nx: (jnp.minimum(j, jnp.maximum(nu[0] - 1, 0)), 0)),
            pl.BlockSpec(memory_space=pl.ANY),
            pl.BlockSpec((1, 1, f2), lambda j, be, nu, nv, nx: (be[j], 0, 0)),
            pl.BlockSpec(memory_space=pl.ANY),
            pl.BlockSpec((1, 1, d), lambda j, be, nu, nv, nx: (be[j], 0, 0)),
        ],
        out_specs=pl.BlockSpec((blk, d), lambda j, be, nu, nv, nx: (j, 0)),
        scratch_shapes=[
            pltpu.VMEM((d, f2), F32),
            pltpu.VMEM((f, d), F32),
            pltpu.SemaphoreType.DMA((2,)),
            pltpu.VMEM((d, f2), BF16),
            pltpu.VMEM((f, d), BF16),
        ],
    )
    return pl.pallas_call(
        _expert_kernel,
        grid_spec=grid_spec,
        out_shape=jax.ShapeDtypeStruct((cap, d), F32),
        compiler_params=pltpu.CompilerParams(
            dimension_semantics=("arbitrary",), vmem_limit_bytes=56 * MIB),
        name="experts",
    )(block_e, n_used, n_valid, next_e, xs, w_up, b_up.reshape(n_exp, 1, f2), w_down, b_down.reshape(n_exp, 1, d))


def _combine_kernel(dest_ref, dest_next_ref, y_hbm, gate_ref, x1_ref, mod_ref, nw_ref, o_ref, buf, sem):
    i = pl.program_id(0)
    n_tiles = pl.num_programs(0)
    tm, d = x1_ref.shape
    slot = lax.rem(i, 2)

    def start_gather(idx_ref, dst_slot):
        for t in range(tm):
            for k in range(TOP_K):
                s = idx_ref[0, 0, t * TOP_K + k]
                pltpu.make_async_copy(y_hbm.at[pl.ds(s, 1), :], buf.at[dst_slot, k, pl.ds(t, 1), :],
                                      sem.at[dst_slot]).start(priority=k % 2)

    @pl.when(i == 0)
    def _():
        start_gather(dest_ref, 0)

    for nxt in range(2):
        @pl.when(jnp.logical_and(i + 1 < n_tiles, slot == 1 - nxt))
        def _():
            start_gather(dest_next_ref, nxt)

    for k in range(TOP_K):
        pltpu.make_async_copy(y_hbm.at[pl.ds(0, tm), :], buf.at[slot, k], sem.at[slot]).wait()

    gate = gate_ref[...]
    ysum = gate[:, 0:1] * buf[slot, 0]
    for k in range(1, TOP_K):
        ysum = ysum + gate[:, k:k + 1] * buf[slot, k]
    ga2 = mod_ref[0, 5:6, :]
    o_ref[...] = x1_ref[...] + ga2 * _rms(ysum, nw_ref[3:4, :])


def _combine(y_sorted, dest, gates, x1, mod3, norm_w, seq):
    n, d = x1.shape
    tm = 256
    n_tiles = n // tm
    tiles_per_seq = seq // tm
    dest3 = dest.reshape(n_tiles, 1, tm * TOP_K)
    return pl.pallas_call(
        _combine_kernel,
        grid=(n_tiles,),
        in_specs=[
            pl.BlockSpec((1, 1, tm * TOP_K), lambda i: (i, 0, 0), memory_space=pltpu.SMEM),
            pl.BlockSpec((1, 1, tm * TOP_K), lambda i: (jnp.minimum(i + 1, n_tiles - 1), 0, 0),
                         memory_space=pltpu.SMEM),
            pl.BlockSpec(memory_space=pl.ANY),
            pl.BlockSpec((tm, LANES), lambda i: (i, 0)),
            pl.BlockSpec((tm, d), lambda i: (i, 0)),
            pl.BlockSpec((1, 6, d), lambda i: (i // tiles_per_seq, 0, 0)),
            pl.BlockSpec((4, d), lambda i: (0, 0)),
        ],
        out_specs=pl.BlockSpec((tm, d), lambda i: (i, 0)),
        out_shape=jax.ShapeDtypeStruct((n, d), F32),
        scratch_shapes=[
            pltpu.VMEM((2, TOP_K, tm, d), F32),
            pltpu.SemaphoreType.DMA((2,)),
        ],
        compiler_params=pltpu.CompilerParams(
            dimension_semantics=("arbitrary",), vmem_limit_bytes=32 * MIB),
        name="combine",
    )(dest3, dest3, y_sorted, gates, x1, mod3, norm_w)


def _layer(x, c, w_ada, b_ada, norm_w, w_in, conv_w, a_log, dt_bias, a_norm_w, rel_bias, w_out,
           w_router, b_router, w_up, b_up, w_down, b_down):
    bsz, seq, d = x.shape
    n = bsz * seq
    a_width = conv_w.shape[1] // 3
    a_heads = a_log.shape[0]
    b_width = w_out.shape[0] - a_width
    n_experts = w_router.shape[1]
    off_gate = 4 * a_width
    off_b = off_gate + 2 * a_heads

    mod3 = _adaln(c, w_ada, b_ada).reshape(bsz, 6, d)
    x2 = x.reshape(n, d)

    w_main = jnp.concatenate([w_in[:, :off_gate], w_in[:, off_b:]], axis=1).astype(BF16)
    w_gate = jnp.pad(w_in[:, off_gate:off_b], ((0, 0), (0, LANES - 2 * a_heads))).astype(BF16)
    qkv_a, z_a, ab, qkv_b = _inproj(x2, mod3, norm_w, w_main, w_gate, conv_w, seq, 3 * a_width, a_width,
                                    3 * b_width)

    o_a = _gdn(qkv_a, z_a, ab, a_log, dt_bias, a_norm_w, bsz, seq)
    o_b = _band_attn(qkv_b, rel_bias, bsz, seq)

    x1, h2, route, gates, cnt = _outproj(o_a, o_b, x2, mod3, norm_w, w_out[:a_width].astype(BF16),
                                         w_out[a_width:].astype(BF16), w_router, b_router, seq)

    top_idx = route[:, :TOP_K]
    rank = route[:, TOP_K:2 * TOP_K]
    counts = cnt[0, :n_experts].astype(jnp.int32)
    padded = (counts + MOE_BLOCK - 1) // MOE_BLOCK * MOE_BLOCK
    pend = jnp.cumsum(padded)
    pstart = pend - padded
    expert_ids = jnp.arange(n_experts, dtype=jnp.int32)
    dest = jnp.sum(jnp.where(top_idx[..., None] == expert_ids, pstart, 0), axis=-1) + rank
    n_assign = n * TOP_K
    n_blocks = -(-n_assign // MOE_BLOCK) + n_experts
    cap = n_blocks * MOE_BLOCK
    block_start = jnp.arange(n_blocks, dtype=jnp.int32) * MOE_BLOCK
    block_e = jnp.minimum(jnp.sum(pend[None, :] <= block_start[:, None], axis=1), n_experts - 1).astype(jnp.int32)
    n_used = (pend[-1:] // MOE_BLOCK).astype(jnp.int32)
    dest = dest.astype(jnp.int32)

    vend = (pstart + counts).astype(jnp.int32)
    block_vend = jnp.sum(jnp.where(block_e[:, None] == expert_ids, vend, 0), axis=1)
    n_valid = jnp.clip(block_vend - block_start, 0, MOE_BLOCK).astype(jnp.int32)

    xs = _dispatch(h2, dest, pend.astype(jnp.int32), vend, n_used, cap)
    blocks = jnp.arange(n_blocks, dtype=jnp.int32)
    later = (blocks[None, :] > blocks[:, None]) & (block_e[None, :] != block_e[:, None]) & (n_valid[None, :] > 0)
    first_later = jnp.min(jnp.where(later, blocks[None, :], n_blocks), axis=1)
    next_e = jnp.sum(jnp.where(blocks[None, :] == first_later[:, None], block_e[None, :] + 1, 0), axis=1) - 1
    y_sorted = _experts(xs, block_e, n_used, n_valid, next_e.astype(jnp.int32), w_up, b_up, w_down, b_down)
    out = _combine(y_sorted, dest, gates, x1, mod3, norm_w, seq)
    return out.reshape(bsz, seq, d)


def kernel(x, c, w_ada, b_ada, norm_w, w_in, conv_w, a_log, dt_bias, a_norm_w, rel_bias, w_out,
           w_router, b_router, w_up, b_up, w_down, b_down):
    for l in range(w_ada.shape[0]):
        x = _layer(x, c, w_ada[l], b_ada[l], norm_w[l], w_in[l], conv_w[l], a_log[l], dt_bias[l],
                   a_norm_w[l], rel_bias[l], w_out[l], w_router[l], b_router[l], w_up[l], b_up[l],
                   w_down[l], b_down[l])
    return x
```

```python
import functools
import math

import jax
import jax.numpy as jnp
from jax import lax
from jax.experimental import pallas as pl
from jax.experimental.pallas import tpu as pltpu

F32 = jnp.float32
BF16 = jnp.bfloat16
HIGHEST = lax.Precision.HIGHEST

EPS = 1e-6
CHUNK = 64
CONV_K = 4
A_HEAD_DIM = 128
B_HEAD_DIM = 64
B_PREV_CHUNKS = 8
REL_CLIP = 128
TOP_K = 4
SWIGLU_ALPHA = 1.702
SWIGLU_LIMIT = 7.0
MOE_BLOCK = 512
CLEAR_ROWS = 64
LANES = 128
NEG_BIG = -1e30
LOG2E = 1.4426950408889634

MIB = 1024 * 1024


def _dot(a, b):
    return jnp.dot(a, b, preferred_element_type=F32)


def _dot_nt(a, b):
    return lax.dot_general(a, b, (((1,), (1,)), ((), ())), preferred_element_type=F32)


def _dot_exact(a, b):
    return jnp.dot(a, b, precision=HIGHEST, preferred_element_type=F32)


def _rms(x, w):
    return x * lax.rsqrt(jnp.mean(x * x, axis=-1, keepdims=True) + EPS) * w


def _adaln_kernel(ct_ref, w_ref, b_ref, o_ref, *, bsz):
    ct = ct_ref[...]
    cs = ct * jax.nn.sigmoid(ct)
    w = w_ref[...]
    rows = [jnp.sum(cs[:, b:b + 1] * w, axis=0, keepdims=True) for b in range(bsz)]
    rows.append(jnp.zeros((o_ref.shape[0] - bsz, w.shape[1]), F32))
    o_ref[...] = jnp.concatenate(rows, axis=0) + b_ref[...]


def _adaln(c, w, b):
    bsz, d = c.shape
    n_out = w.shape[1]
    rows = 8
    ct = jnp.pad(c.T, ((0, 0), (0, LANES - bsz)))
    tn = 512
    out = pl.pallas_call(
        functools.partial(_adaln_kernel, bsz=bsz),
        grid=(n_out // tn,),
        in_specs=[
            pl.BlockSpec((d, LANES), lambda j: (0, 0)),
            pl.BlockSpec((d, tn), lambda j: (0, j)),
            pl.BlockSpec((1, tn), lambda j: (0, j)),
        ],
        out_specs=pl.BlockSpec((rows, tn), lambda j: (0, j)),
        out_shape=jax.ShapeDtypeStruct((rows, n_out), F32),
        name="adaln",
    )(ct, w, b.reshape(1, n_out))
    return out[:bsz]


def _inproj_kernel(x_ref, mod_ref, nw_ref, wm_ref, wg_ref, cw_ref, qkva_ref, z_ref, ab_ref, qkvb_ref, raw_ref,
                   *, a_qkv, a_z, tiles_per_seq):
    i = pl.program_id(0)
    tm = x_ref.shape[0]
    dk = A_HEAD_DIM
    a_width = a_qkv // 3
    halo = 8
    tn = 512

    @pl.when(lax.rem(i, tiles_per_seq) == 0)
    def _():
        raw_ref[0:halo, :] = jnp.zeros((halo, a_qkv), F32)

    x = x_ref[...]
    sh = mod_ref[0, 0:1, :]
    sc = mod_ref[0, 1:2, :]
    h = _rms(x, nw_ref[0:1, :]) * (1.0 + sc) + sh
    hb = h.astype(BF16)

    def conv_silu(c0):
        acc = cw_ref[CONV_K - 1:CONV_K, c0:c0 + dk] * raw_ref[halo:halo + tm, c0:c0 + dk]
        for j in range(CONV_K - 1):
            start = halo - (CONV_K - 1) + j
            acc = acc + cw_ref[j:j + 1, c0:c0 + dk] * raw_ref[start:start + tm, c0:c0 + dk]
        y = acc * jax.nn.sigmoid(acc)
        if c0 < 2 * a_width:
            y = y * lax.rsqrt(jnp.sum(y * y, axis=-1, keepdims=True) + EPS)
        if c0 < a_width:
            y = y * (dk ** -0.5)
        qkva_ref[:, c0:c0 + dk] = y

    for c0 in range(0, a_qkv, tn):
        raw_ref[halo:halo + tm, c0:c0 + tn] = _dot(hb, wm_ref[:, c0:c0 + tn])
    slabs = list(range(0, a_qkv, dk))
    other = [("z", c0) for c0 in range(0, a_z, tn)] + [("b", c0) for c0 in range(0, qkvb_ref.shape[1], tn)]
    per = -(-len(slabs) // len(other))
    off = a_qkv + a_z
    for n_o, (kind, c0) in enumerate(other):
        for c in slabs[n_o * per:(n_o + 1) * per]:
            conv_silu(c)
        if kind == "z":
            z_ref[:, c0:c0 + tn] = _dot(hb, wm_ref[:, a_qkv + c0:a_qkv + c0 + tn])
        else:
            qkvb_ref[:, c0:c0 + tn] = _dot(hb, wm_ref[:, off + c0:off + c0 + tn]).astype(BF16)
    ab_ref[...] = _dot(hb, wg_ref[...])
    raw_ref[0:halo, :] = raw_ref[tm:tm + halo, :]


def _inproj(x2, mod3, norm_w, w_main, w_gate, conv_w, seq, a_qkv, a_z, b_qkv):
    n, d = x2.shape
    tm = 512
    tiles_per_seq = seq // tm
    kern = functools.partial(_inproj_kernel, a_qkv=a_qkv, a_z=a_z, tiles_per_seq=tiles_per_seq)
    return pl.pallas_call(
        kern,
        grid=(n // tm,),
        in_specs=[
            pl.BlockSpec((tm, d), lambda i: (i, 0)),
            pl.BlockSpec((1, 6, d), lambda i: (i // tiles_per_seq, 0, 0)),
            pl.BlockSpec((4, d), lambda i: (0, 0)),
            pl.BlockSpec(w_main.shape, lambda i: (0, 0)),
            pl.BlockSpec(w_gate.shape, lambda i: (0, 0)),
            pl.BlockSpec(conv_w.shape, lambda i: (0, 0)),
        ],
        out_specs=[
            pl.BlockSpec((tm, a_qkv), lambda i: (i, 0)),
            pl.BlockSpec((tm, a_z), lambda i: (i, 0)),
            pl.BlockSpec((tm, LANES), lambda i: (i, 0)),
            pl.BlockSpec((tm, b_qkv), lambda i: (i, 0)),
        ],
        out_shape=[
            jax.ShapeDtypeStruct((n, a_qkv), F32),
            jax.ShapeDtypeStruct((n, a_z), F32),
            jax.ShapeDtypeStruct((n, LANES), F32),
            jax.ShapeDtypeStruct((n, b_qkv), BF16),
        ],
        scratch_shapes=[pltpu.VMEM((tm + 8, a_qkv), F32)],
        compiler_params=pltpu.CompilerParams(
            dimension_semantics=("arbitrary",), vmem_limit_bytes=48 * MIB),
        name="inproj",
    )(x2, mod3, norm_w, w_main, w_gate, conv_w)


def _gdn_kernel(qkv_ref, z_ref, ab_ref, alog_ref, dtb_ref, anw_ref, o_ref, state_ref, *, n_heads):
    i = pl.program_id(1)
    tt = qkv_ref.shape[0]
    width = qkv_ref.shape[1]
    a_width = width // 3
    dk = A_HEAD_DIM
    sub = 2 * CHUNK
    per_sub = sub // CHUNK

    @pl.when(i == 0)
    def _():
        state_ref[...] = jnp.zeros_like(state_ref)

    row = lax.broadcasted_iota(jnp.int32, (sub, sub), 0)
    col = lax.broadcasted_iota(jnp.int32, (sub, sub), 1)
    shift = int(math.log2(CHUNK))
    same = (row >> shift) == (col >> shift)
    tri_incl = same & (col <= row)
    tri_strict = same & (col < row)
    eye = (row == col).astype(F32)
    pair = (row >> 1) == (col >> 1)
    couples = [((row >> (lv + 1)) == (col >> (lv + 1))) & ((row >> lv) != (col >> lv))
               for lv in range(1, shift)]
    col_chunk = lax.broadcasted_iota(jnp.int32, (dk, sub), 1) >> shift
    tri_incl_f = tri_incl.astype(F32)
    same_f = same.astype(F32)

    n_sub = tt // sub
    gates = []
    for p in range(n_sub):
        r0 = p * sub
        ab = ab_ref[r0:r0 + sub, :]
        gfull = -jnp.exp(alog_ref[...]) * jax.nn.softplus(ab + dtb_ref[...])
        bfull = jax.nn.sigmoid(ab)
        gcum = _dot_exact(tri_incl_f, gfull)
        gtot = _dot_exact(same_f, gfull)
        gates.append((gcum, gcum.T, gtot, bfull))

    items = [(p, h) for p in range(n_sub) for h in range(n_heads)]
    pre = {}
    for p, h in items:
        r0 = p * sub
        gcum, gcum_t, gtot, bfull = gates[p]
        gc = gcum[:, h:h + 1]
        beta = bfull[:, n_heads + h:n_heads + h + 1]
        q = qkv_ref[r0:r0 + sub, h * dk:(h + 1) * dk]
        k = qkv_ref[r0:r0 + sub, a_width + h * dk:a_width + (h + 1) * dk]
        v = qkv_ref[r0:r0 + sub, 2 * a_width + h * dk:2 * a_width + (h + 1) * dk]
        decay = jnp.exp(jnp.where(tri_incl, gc - gcum_t[h:h + 1, :], -jnp.inf))
        kb = k * beta
        k16 = k.astype(BF16)
        lower = jnp.where(tri_strict, _dot_nt(kb.astype(BF16), k16) * decay, 0.0)
        attn16 = (_dot_nt(q.astype(BF16), k16) * decay).astype(BF16)
        eg = jnp.exp(gc)
        rhs = jnp.concatenate([kb * eg, v * beta], axis=1).astype(BF16)
        kd_t = (k * jnp.exp(gtot[:, h:h + 1] - gc)).T
        kd_st = jnp.concatenate([jnp.where(col_chunk == c, kd_t, 0.0) for c in range(per_sub)],
                                axis=0).astype(BF16)
        pre[p, h] = dict(lower=lower, attn16=attn16, rhs=rhs, kd_st=kd_st, qeg=q * eg)

    tmat = {it: eye - jnp.where(pair, pre[it]["lower"], 0.0) for it in items}
    for couple in couples:
        t16 = {it: tmat[it].astype(BF16) for it in items}
        half = {it: _dot(t16[it], jnp.where(couple, pre[it]["lower"], 0.0).astype(BF16)) for it in items}
        tmat = {it: tmat[it] - _dot(half[it].astype(BF16), t16[it]) for it in items}

    wu16 = {it: _dot(tmat[it].astype(BF16), pre[it]["rhs"]).astype(BF16) for it in items}
    a_wu = {it: _dot(pre[it]["attn16"], wu16[it]) for it in items}
    k_wu = {it: _dot(pre[it]["kd_st"], wu16[it]) for it in items}
    qp16 = {it: (pre[it]["qeg"] - a_wu[it][:, :dk]).astype(BF16) for it in items}

    state = [state_ref[h] for h in range(n_heads)]
    for p in range(n_sub):
        r0 = p * sub
        gtot = gates[p][2]
        outs = [[] for _ in range(n_heads)]
        for c in range(per_sub):
            c0 = c * CHUNK
            for h in range(n_heads):
                kw = k_wu[p, h]
                m16 = kw[c * dk:(c + 1) * dk, :dk].astype(BF16)
                res = _dot(jnp.concatenate([m16, qp16[p, h][c0:c0 + CHUNK]], axis=0), state[h].astype(BF16))
                outs[h].append(res[dk:] + a_wu[p, h][c0:c0 + CHUNK, dk:])
                state[h] = (state[h] * jnp.exp(gtot[c0:c0 + 1, h:h + 1]) - res[:dk]
                            + kw[c * dk:(c + 1) * dk, dk:])
        for h in range(n_heads):
            o = jnp.concatenate(outs[h], axis=0)
            zh = z_ref[r0:r0 + sub, h * dk:(h + 1) * dk]
            o_ref[r0:r0 + sub, h * dk:(h + 1) * dk] = (
                _rms(o, anw_ref[...]) * (zh * jax.nn.sigmoid(zh))).astype(o_ref.dtype)
    for h in range(n_heads):
        state_ref[h] = state[h]


def _gdn(qkv_a, z, ab, a_log, dt_bias, a_norm_w, bsz, seq):
    n, width = qkv_a.shape
    a_width = width // 3
    n_heads = a_width // A_HEAD_DIM
    tt = 512
    nt = seq // tt
    alog = jnp.zeros((1, LANES), F32).at[0, :n_heads].set(a_log)
    dtb = jnp.zeros((1, LANES), F32).at[0, :n_heads].set(dt_bias)
    kern = functools.partial(_gdn_kernel, n_heads=n_heads)
    return pl.pallas_call(
        kern,
        grid=(bsz, nt),
        in_specs=[
            pl.BlockSpec((tt, width), lambda b, i: (b * nt + i, 0)),
            pl.BlockSpec((tt, a_width), lambda b, i: (b * nt + i, 0)),
            pl.BlockSpec((tt, LANES), lambda b, i: (b * nt + i, 0)),
            pl.BlockSpec((1, LANES), lambda b, i: (0, 0)),
            pl.BlockSpec((1, LANES), lambda b, i: (0, 0)),
            pl.BlockSpec((1, A_HEAD_DIM), lambda b, i: (0, 0)),
        ],
        out_specs=pl.BlockSpec((tt, a_width), lambda b, i: (b * nt + i, 0)),
        out_shape=jax.ShapeDtypeStruct((n, a_width), BF16),
        scratch_shapes=[pltpu.VMEM((n_heads, A_HEAD_DIM, A_HEAD_DIM), F32)],
        compiler_params=pltpu.CompilerParams(
            dimension_semantics=("arbitrary", "arbitrary"), vmem_limit_bytes=48 * MIB),
        name="gdn",
    )(qkv_a, z, ab, alog, dtb, a_norm_w.reshape(1, A_HEAD_DIM))


def _attn_kernel(q_ref, *refs, n_heads, n_kb, n_q):
    n_win = n_q + n_kb - 1
    k_refs, v_refs = refs[:n_win], refs[n_win:2 * n_win]
    rev_ref, o_ref, bias_ref = refs[2 * n_win], refs[2 * n_win + 1], refs[2 * n_win + 2]
    i = pl.program_id(1)
    tq = q_ref.shape[0] // n_q
    hd = B_HEAD_DIM

    @pl.when(jnp.logical_and(pl.program_id(0) == 0, i == 0))
    def _():
        nk = n_kb * tq
        span = rev_ref.shape[2]
        row = lax.broadcasted_iota(jnp.int32, (tq, nk), 0)
        col = lax.broadcasted_iota(jnp.int32, (tq, nk), 1)
        shift = int(math.log2(CHUNK))
        qc = row >> shift
        kc = (col >> shift) - (n_kb - 1) * tq // CHUNK
        allowed = (kc <= qc) & (kc >= qc - B_PREV_CHUNKS)
        for h in range(n_heads):
            wide = jnp.broadcast_to(rev_ref[h], (tq, span))
            skew = pltpu.roll(wide, span - (tq - 1), 1, stride=1, stride_axis=0)
            bias_ref[h] = jnp.where(allowed, skew[:, :nk] * LOG2E, NEG_BIG)
    per = LANES // hd
    lane = lax.broadcasted_iota(jnp.int32, (1, LANES), 1)
    items = [(t, h) for t in range(n_q) for h in range(n_heads)]

    def lanes_of(h):
        return slice((h // per) * LANES, (h // per + 1) * LANES)

    def in_head(h):
        hh = h % per
        return (lane >= hh * hd) & (lane < (hh + 1) * hd)

    scores = {}
    for t, h in items:
        qg = q_ref[t * tq:(t + 1) * tq, lanes_of(h)]
        qh = (jnp.where(in_head(h), qg, jnp.zeros_like(qg)).astype(F32) * (hd ** -0.5 * LOG2E)).astype(BF16)
        s = []
        for j in range(n_kb):
            sj = _dot_nt(qh, k_refs[t + j][:, lanes_of(h)]) + bias_ref[h, :, j * tq:(j + 1) * tq]
            if j < n_kb - 1:
                sj = jnp.where(i * n_q + t >= n_kb - 1 - j, sj, NEG_BIG)
            s.append(sj)
        scores[t, h] = s
    probs, dens = {}, {}
    for it in items:
        s = scores[it]
        top = s[0]
        for sj in s[1:]:
            top = jnp.maximum(top, sj)
        m = top.max(axis=-1, keepdims=True)
        p = [jnp.exp2(sj - m) for sj in s]
        tot = p[0]
        for pj in p[1:]:
            tot = tot + pj
        probs[it] = [pj.astype(BF16) for pj in p]
        dens[it] = tot.sum(axis=-1, keepdims=True)
    outs = {}
    for t, h in items:
        acc = _dot(probs[t, h][0], v_refs[t][:, lanes_of(h)])
        for j in range(1, n_kb):
            acc = acc + _dot(probs[t, h][j], v_refs[t + j][:, lanes_of(h)])
        outs[t, h] = acc / dens[t, h]
    for t in range(n_q):
        for g in range(n_heads // per):
            out = outs[t, g * per]
            for hh in range(1, per):
                out = jnp.where(in_head(g * per + hh), outs[t, g * per + hh], out)
            o_ref[t * tq:(t + 1) * tq, g * LANES:(g + 1) * LANES] = out.astype(o_ref.dtype)


def _bias_by_offset(rel_bias, tq, n_kb):
    n_h = rel_bias.shape[0]
    back = (n_kb - 1) * tq
    nk = n_kb * tq
    span = tq + nk - 1
    lo = (nk - 1 - back) - REL_CLIP
    hi = span - lo - (2 * REL_CLIP + 1)
    by_offset = jnp.concatenate([jnp.broadcast_to(rel_bias[:, :1], (n_h, lo)), rel_bias,
                                 jnp.broadcast_to(rel_bias[:, -1:], (n_h, hi))], axis=1).astype(F32)
    padded = -(-span // LANES) * LANES
    return jnp.pad(by_offset[:, ::-1], ((0, 0), (0, padded - span)))[:, None, :]


def _band_attn(qkv_b, rel_bias, bsz, seq):
    n, width = qkv_b.shape
    b_width = width // 3
    n_heads = b_width // B_HEAD_DIM
    tq = 256
    n_kb = 3
    n_q = 2
    assert (n_kb - 1) * tq == B_PREV_CHUNKS * CHUNK
    nt = seq // tq
    steps = nt // n_q
    n_win = n_q + n_kb - 1
    rev = _bias_by_offset(rel_bias, tq, n_kb)
    kern = functools.partial(_attn_kernel, n_heads=n_heads, n_kb=n_kb, n_q=n_q)

    def kv_spec(colblk, w):
        return pl.BlockSpec(
            (tq, b_width), lambda b, i: (b * nt + jnp.maximum(i * n_q - (n_kb - 1) + w, 0), colblk))

    return pl.pallas_call(
        kern,
        grid=(bsz, steps),
        in_specs=[pl.BlockSpec((n_q * tq, b_width), lambda b, i: (b * steps + i, 0))]
        + [kv_spec(1, w) for w in range(n_win)] + [kv_spec(2, w) for w in range(n_win)]
        + [pl.BlockSpec(rev.shape, lambda b, i: (0, 0, 0))],
        out_specs=pl.BlockSpec((n_q * tq, b_width), lambda b, i: (b * steps + i, 0)),
        out_shape=jax.ShapeDtypeStruct((n, b_width), BF16),
        scratch_shapes=[pltpu.VMEM((n_heads, tq, n_kb * tq), F32)],
        compiler_params=pltpu.CompilerParams(
            dimension_semantics=("arbitrary", "arbitrary"), vmem_limit_bytes=48 * MIB),
        name="band_attn",
    )(*([qkv_b] * (1 + 2 * n_win)), rev)


def _outproj_kernel(oa_ref, ob_ref, x_ref, mod_ref, nw_ref, woa_ref, wob_ref, wr_ref, br_ref,
                    x1_ref, h2_ref, route_ref, gate_ref, cnt_ref, carry_ref, *, n_experts):
    i = pl.program_id(0)
    tm = x_ref.shape[0]
    sub = 256
    parts = [slice(r0, r0 + sub) for r0 in range(0, tm, sub)]

    @pl.when(i == 0)
    def _():
        carry_ref[...] = jnp.zeros_like(carry_ref)

    ga1 = mod_ref[0, 2:3, :]
    sh2 = mod_ref[0, 3:4, :]
    sc2 = mod_ref[0, 4:5, :]
    ys = [_dot(oa_ref[r, :], woa_ref[...]) + _dot(ob_ref[r, :], wob_ref[...]) for r in parts]
    h2s = []
    for r, y in zip(parts, ys):
        x1 = x_ref[r, :] + ga1 * _rms(y, nw_ref[1:2, :])
        x1_ref[r, :] = x1
        h2 = _rms(x1, nw_ref[2:3, :]) * (1.0 + sc2) + sh2
        h2_ref[r, :] = h2
        h2s.append(h2)

    logits = []
    for h2 in h2s:
        h_hi = h2.astype(BF16)
        h_lo = (h2 - h_hi.astype(F32)).astype(BF16)
        logits.append(_dot(h_hi, wr_ref[0]) + (_dot(h_hi, wr_ref[1]) + _dot(h_lo, wr_ref[0])) + br_ref[...])

    lane_i = lax.broadcasted_iota(jnp.int32, (sub, LANES), 1)
    lane = lane_i.astype(F32)
    lgs = [jnp.where(lane_i < n_experts, lg, -jnp.inf) for lg in logits]
    vals = [[] for _ in parts]
    idxs = [[] for _ in parts]
    for _ in range(TOP_K):
        for p in range(len(parts)):
            m = lgs[p].max(axis=-1, keepdims=True)
            idx = jnp.where(lgs[p] == m, lane, float(LANES)).min(axis=-1, keepdims=True)
            vals[p].append(m)
            idxs[p].append(idx)
            lgs[p] = jnp.where(lane == idx, -jnp.inf, lgs[p])

    row = lax.broadcasted_iota(jnp.int32, (sub, sub), 0)
    col = lax.broadcasted_iota(jnp.int32, (sub, sub), 1)
    before = (col < row).astype(BF16)
    carry = carry_ref[0:1, :]
    for p, r in enumerate(parts):
        ex = [jnp.exp(v - vals[p][0]) for v in vals[p]]
        den = ex[0]
        for e in ex[1:]:
            den = den + e
        onehot = jnp.zeros((sub, LANES), F32)
        for idx in idxs[p]:
            onehot = onehot + (lane == idx).astype(F32)
        cum = _dot(before, onehot.astype(BF16)) + carry
        carry = carry + onehot.sum(axis=0, keepdims=True)
        route = jnp.zeros((sub, LANES), F32)
        gate = jnp.zeros((sub, LANES), F32)
        for k in range(TOP_K):
            rank = jnp.where(lane == idxs[p][k], cum, 0.0).sum(axis=-1, keepdims=True)
            route = jnp.where(lane_i == k, idxs[p][k], route)
            route = jnp.where(lane_i == TOP_K + k, rank, route)
            gate = jnp.where(lane_i == k, ex[k] / den, gate)
        route_ref[r, :] = route.astype(jnp.int32)
        gate_ref[r, :] = gate
    carry_ref[...] = jnp.broadcast_to(carry, carry_ref.shape)
    cnt_ref[...] = jnp.broadcast_to(carry, cnt_ref.shape)


def _outproj(o_a, o_b, x2, mod3, norm_w, w_out_a, w_out_b, w_router, b_router, seq):
    n, d = x2.shape
    n_experts = w_router.shape[1]
    tm = 1024
    tiles_per_seq = seq // tm
    wr32 = jnp.zeros((d, LANES), F32).at[:, :n_experts].set(w_router)
    wr_hi = wr32.astype(BF16)
    wr = jnp.stack([wr_hi, (wr32 - wr_hi.astype(F32)).astype(BF16)])
    br = jnp.zeros((1, LANES), F32).at[0, :n_experts].set(b_router)
    kern = functools.partial(_outproj_kernel, n_experts=n_experts)
    aw = o_a.shape[1]
    bw = o_b.shape[1]
    return pl.pallas_call(
        kern,
        grid=(n // tm,),
        in_specs=[
            pl.BlockSpec((tm, aw), lambda i: (i, 0)),
            pl.BlockSpec((tm, bw), lambda i: (i, 0)),
            pl.BlockSpec((tm, d), lambda i: (i, 0)),
            pl.BlockSpec((1, 6, d), lambda i: (i // tiles_per_seq, 0, 0)),
            pl.BlockSpec((4, d), lambda i: (0, 0)),
            pl.BlockSpec((aw, d), lambda i: (0, 0)),
            pl.BlockSpec((bw, d), lambda i: (0, 0)),
            pl.BlockSpec((2, d, LANES), lambda i: (0, 0, 0)),
            pl.BlockSpec((1, LANES), lambda i: (0, 0)),
        ],
        out_specs=[
            pl.BlockSpec((tm, d), lambda i: (i, 0)),
            pl.BlockSpec((tm, d), lambda i: (i, 0)),
            pl.BlockSpec((tm, LANES), lambda i: (i, 0)),
            pl.BlockSpec((tm, LANES), lambda i: (i, 0)),
            pl.BlockSpec((8, LANES), lambda i: (0, 0)),
        ],
        out_shape=[
            jax.ShapeDtypeStruct((n, d), F32),
            jax.ShapeDtypeStruct((n, d), F32),
            jax.ShapeDtypeStruct((n, LANES), jnp.int32),
            jax.ShapeDtypeStruct((n, LANES), F32),
            jax.ShapeDtypeStruct((8, LANES), F32),
        ],
        scratch_shapes=[pltpu.VMEM((8, LANES), F32)],
        compiler_params=pltpu.CompilerParams(
            dimension_semantics=("arbitrary",), vmem_limit_bytes=48 * MIB),
        name="outproj_router",
    )(o_a, o_b, x2, mod3, norm_w, w_out_a, w_out_b, wr, br)


def _dispatch_kernel(pend_ref, vend_ref, nu_ref, dest_ref, h_ref, xs_hbm, stage, zeros, sem, zsem, tsem,
                     *, n_experts, blk):
    i = pl.program_id(0)
    n_tiles = pl.num_programs(0)
    tm, d = h_ref.shape
    slot = lax.rem(i, 2)
    piece = zeros.shape[0]
    shift = int(math.log2(piece))
    n_pieces = xs_hbm.shape[0] // piece

    def wait_rows(s):
        for _ in range(TOP_K):
            pltpu.make_async_copy(stage.at[s], xs_hbm.at[pl.ds(0, tm), :], sem.at[s]).wait()

    def clear(sem_ref):
        def body(p, carry):
            start = pl.multiple_of(p * piece, piece)
            pltpu.make_async_copy(zeros, xs_hbm.at[pl.ds(start, piece), :], sem_ref).start()
            return carry
        return body

    def wait_clear(sem_ref):
        def body(p, carry):
            pltpu.make_async_copy(zeros, xs_hbm.at[pl.ds(0, piece), :], sem_ref).wait()
            return carry
        return body

    def first_piece(e):
        return vend_ref[e] >> shift

    def after_first(e):
        return jnp.minimum(first_piece(e) + 1, pend_ref[e] >> shift)

    @pl.when(i == 0)
    def _():
        zeros[...] = jnp.zeros_like(zeros)
        for e in range(n_experts):
            lax.fori_loop(first_piece(e), after_first(e), clear(zsem), 0)
        for e in range(n_experts):
            lax.fori_loop(after_first(e), pend_ref[e] >> shift, clear(tsem), 0)
        lax.fori_loop((nu_ref[0] * blk) >> shift, n_pieces, clear(tsem), 0)
        for e in range(n_experts):
            lax.fori_loop(first_piece(e), after_first(e), wait_clear(zsem), 0)

    @pl.when(i >= 2)
    def _():
        wait_rows(slot)

    stage[slot] = h_ref[...]

    for t in range(tm):
        for k in range(TOP_K):
            s = dest_ref[0, 0, t * TOP_K + k]
            pltpu.make_async_copy(stage.at[slot, pl.ds(t, 1), :], xs_hbm.at[pl.ds(s, 1), :],
                                  sem.at[slot]).start(priority=k % 2)

    @pl.when(i == n_tiles - 1)
    def _():
        wait_rows(slot)

        @pl.when(n_tiles > 1)
        def _():
            wait_rows(1 - slot)

        for e in range(n_experts):
            lax.fori_loop(after_first(e), pend_ref[e] >> shift, wait_clear(tsem), 0)
        lax.fori_loop((nu_ref[0] * blk) >> shift, n_pieces, wait_clear(tsem), 0)


def _dispatch(h2, dest, pend, vend, n_used, cap):
    n, d = h2.shape
    n_experts = pend.shape[0]
    tm = 256
    n_tiles = n // tm
    kern = functools.partial(_dispatch_kernel, n_experts=n_experts, blk=MOE_BLOCK)
    grid_spec = pltpu.PrefetchScalarGridSpec(
        num_scalar_prefetch=3,
        grid=(n_tiles,),
        in_specs=[
            pl.BlockSpec((1, 1, tm * TOP_K), lambda i, pe, pa, nu: (i, 0, 0), memory_space=pltpu.SMEM),
            pl.BlockSpec((tm, d), lambda i, pe, pa, nu: (i, 0)),
        ],
        out_specs=pl.BlockSpec(memory_space=pl.ANY),
        scratch_shapes=[
            pltpu.VMEM((2, tm, d), F32),
            pltpu.VMEM((CLEAR_ROWS, d), F32),
            pltpu.SemaphoreType.DMA((2,)),
            pltpu.SemaphoreType.DMA(()),
            pltpu.SemaphoreType.DMA(()),
        ],
    )
    return pl.pallas_call(
        kern,
        grid_spec=grid_spec,
        out_shape=jax.ShapeDtypeStruct((cap, d), F32),
        compiler_params=pltpu.CompilerParams(
            dimension_semantics=("arbitrary",), vmem_limit_bytes=32 * MIB),
        name="dispatch",
    )(pend, vend, n_used, dest.reshape(n_tiles, 1, tm * TOP_K), h2)


def _expert_kernel(be_ref, nu_ref, nv_ref, nx_ref, xs_ref, wup_hbm, bup_ref, wdn_hbm, bdn_ref, y_ref,
                   up_stage, dn_stage, wsem, wup16, wdn16):
    j = pl.program_id(0)
    blk, d = xs_ref.shape
    f = wdn16.shape[0]
    n_valid = nv_ref[j]
    half = blk // 2

    def weight_copies(e):
        return (pltpu.make_async_copy(wup_hbm.at[e], up_stage, wsem.at[0]),
                pltpu.make_async_copy(wdn_hbm.at[e], dn_stage, wsem.at[1]))

    prev = be_ref[jnp.maximum(j - 1, 0)]
    first_of_expert = jnp.logical_and(n_valid > 0, jnp.logical_or(j == 0, be_ref[j] != prev))

    @pl.when(jnp.logical_and(j == 0, n_valid > 0))
    def _():
        for c in weight_copies(be_ref[0]):
            c.start()

    @pl.when(first_of_expert)
    def _():
        for c in weight_copies(be_ref[j]):
            c.wait()
        step = 128
        for r0 in range(0, d, step):
            wup16[r0:r0 + step, :] = up_stage[r0:r0 + step, :].astype(BF16)
        for r0 in range(0, f, step):
            wdn16[r0:r0 + step, :] = dn_stage[r0:r0 + step, :].astype(BF16)

        @pl.when(nx_ref[j] >= 0)
        def _():
            for c in weight_copies(nx_ref[j]):
                c.start()

    def mlp(rows):
        xb = xs_ref[0:rows, :].astype(BF16)
        fc = min(f, 512)
        acc = None
        for c0 in range(0, f, fc):
            glu = _dot(xb, wup16[:, c0:c0 + fc]) + bup_ref[0, :, c0:c0 + fc]
            lin = _dot(xb, wup16[:, f + c0:f + c0 + fc]) + bup_ref[0, :, f + c0:f + c0 + fc]
            glu = jnp.minimum(glu, SWIGLU_LIMIT)
            lin = jnp.clip(lin, -SWIGLU_LIMIT, SWIGLU_LIMIT)
            act = glu * jax.nn.sigmoid(SWIGLU_ALPHA * glu) * (lin + 1.0)
            part = _dot(act.astype(BF16), wdn16[c0:c0 + fc, :])
            acc = part if acc is None else acc + part
        y_ref[0:rows, :] = acc + bdn_ref[0]

    @pl.when(n_valid > half)
    def _():
        mlp(blk)

    @pl.when(jnp.logical_and(n_valid > 0, n_valid <= half))
    def _():
        mlp(half)
        y_ref[half:, :] = jnp.zeros((blk - half, d), y_ref.dtype)

    @pl.when(n_valid == 0)
    def _():
        y_ref[...] = jnp.zeros_like(y_ref)


def _experts(xs, block_e, n_used, n_valid, next_e, w_up, b_up, w_down, b_down):
    cap, d = xs.shape
    n_exp, _, f2 = w_up.shape
    f = w_down.shape[1]
    blk = MOE_BLOCK
    n_blocks = cap // blk
    grid_spec = pltpu.PrefetchScalarGridSpec(
        num_scalar_prefetch=4,
        grid=(n_blocks,),
        in_specs=[
            pl.BlockSpec((blk, d), lambda j, be, nu, nv, nx: (jnp.minimum(j, jnp.maximum(nu[0] - 1, 0)), 0)),
            pl.BlockSpec(memory_space=pl.ANY),
            pl.BlockSpec((1, 1, f2), lambda j, be, nu, nv, nx: (be[j], 0, 0)),
            pl.BlockSpec(memory_space=pl.ANY),
            pl.BlockSpec((1, 1, d), lambda j, be, nu, nv, nx: (be[j], 0, 0)),
        ],
        out_specs=pl.BlockSpec((blk, d), lambda j, be, nu, nv, nx: (j, 0)),
        scratch_shapes=[
            pltpu.VMEM((d, f2), F32),
            pltpu.VMEM((f, d), F32),
            pltpu.SemaphoreType.DMA((2,)),
            pltpu.VMEM((d, f2), BF16),
            pltpu.VMEM((f, d), BF16),
        ],
    )
    return pl.pallas_call(
        _expert_kernel,
        grid_spec=grid_spec,
        out_shape=jax.ShapeDtypeStruct((cap, d), F32),
        compiler_params=pltpu.CompilerParams(
            dimension_semantics=("arbitrary",), vmem_limit_bytes=56 * MIB),
        name="experts",
    )(block_e, n_used, n_valid, next_e, xs, w_up, b_up.reshape(n_exp, 1, f2), w_down, b_down.reshape(n_exp, 1, d))


def _combine_kernel(dest_ref, dest_next_ref, y_hbm, gate_ref, x1_ref, mod_ref, nw_ref, o_ref, buf, sem):
    i = pl.program_id(0)
    n_tiles = pl.num_programs(0)
    tm, d = x1_ref.shape
    slot = lax.rem(i, 2)

    def start_gather(idx_ref, dst_slot):
        for t in range(tm):
            for k in range(TOP_K):
                s = idx_ref[0, 0, t * TOP_K + k]
                pltpu.make_async_copy(y_hbm.at[pl.ds(s, 1), :], buf.at[dst_slot, k, pl.ds(t, 1), :],
                                      sem.at[dst_slot]).start(priority=k % 2)

    @pl.when(i == 0)
    def _():
        start_gather(dest_ref, 0)

    for nxt in range(2):
        @pl.when(jnp.logical_and(i + 1 < n_tiles, slot == 1 - nxt))
        def _():
            start_gather(dest_next_ref, nxt)

    for k in range(TOP_K):
        pltpu.make_async_copy(y_hbm.at[pl.ds(0, tm), :], buf.at[slot, k], sem.at[slot]).wait()

    gate = gate_ref[...]
    ysum = gate[:, 0:1] * buf[slot, 0]
    for k in range(1, TOP_K):
        ysum = ysum + gate[:, k:k + 1] * buf[slot, k]
    ga2 = mod_ref[0, 5:6, :]
    o_ref[...] = x1_ref[...] + ga2 * _rms(ysum, nw_ref[3:4, :])


def _combine(y_sorted, dest, gates, x1, mod3, norm_w, seq):
    n, d = x1.shape
    tm = 256
    n_tiles = n // tm
    tiles_per_seq = seq // tm
    dest3 = dest.reshape(n_tiles, 1, tm * TOP_K)
    return pl.pallas_call(
        _combine_kernel,
        grid=(n_tiles,),
        in_specs=[
            pl.BlockSpec((1, 1, tm * TOP_K), lambda i: (i, 0, 0), memory_space=pltpu.SMEM),
            pl.BlockSpec((1, 1, tm * TOP_K), lambda i: (jnp.minimum(i + 1, n_tiles - 1), 0, 0),
                         memory_space=pltpu.SMEM),
            pl.BlockSpec(memory_space=pl.ANY),
            pl.BlockSpec((tm, LANES), lambda i: (i, 0)),
            pl.BlockSpec((tm, d), lambda i: (i, 0)),
            pl.BlockSpec((1, 6, d), lambda i: (i // tiles_per_seq, 0, 0)),
            pl.BlockSpec((4, d), lambda i: (0, 0)),
        ],
        out_specs=pl.BlockSpec((tm, d), lambda i: (i, 0)),
        out_shape=jax.ShapeDtypeStruct((n, d), F32),
        scratch_shapes=[
            pltpu.VMEM((2, TOP_K, tm, d), F32),
            pltpu.SemaphoreType.DMA((2,)),
        ],
        compiler_params=pltpu.CompilerParams(
            dimension_semantics=("arbitrary",), vmem_limit_bytes=32 * MIB),
        name="combine",
    )(dest3, dest3, y_sorted, gates, x1, mod3, norm_w)


def _layer(x, c, w_ada, b_ada, norm_w, w_in, conv_w, a_log, dt_bias, a_norm_w, rel_bias, w_out,
           w_router, b_router, w_up, b_up, w_down, b_down):
    bsz, seq, d = x.shape
    n = bsz * seq
    a_width = conv_w.shape[1] // 3
    a_heads = a_log.shape[0]
    b_width = w_out.shape[0] - a_width
    n_experts = w_router.shape[1]
    off_gate = 4 * a_width
    off_b = off_gate + 2 * a_heads

    mod3 = _adaln(c, w_ada, b_ada).reshape(bsz, 6, d)
    x2 = x.reshape(n, d)

    w_main = jnp.concatenate([w_in[:, :off_gate], w_in[:, off_b:]], axis=1).astype(BF16)
    w_gate = jnp.pad(w_in[:, off_gate:off_b], ((0, 0), (0, LANES - 2 * a_heads))).astype(BF16)
    qkv_a, z_a, ab, qkv_b = _inproj(x2, mod3, norm_w, w_main, w_gate, conv_w, seq, 3 * a_width, a_width,
                                    3 * b_width)

    o_a = _gdn(qkv_a, z_a, ab, a_log, dt_bias, a_norm_w, bsz, seq)
    o_b = _band_attn(qkv_b, rel_bias, bsz, seq)

    x1, h2, route, gates, cnt = _outproj(o_a, o_b, x2, mod3, norm_w, w_out[:a_width].astype(BF16),
                                         w_out[a_width:].astype(BF16), w_router, b_router, seq)

    top_idx = route[:, :TOP_K]
    rank = route[:, TOP_K:2 * TOP_K]
    counts = cnt[0, :n_experts].astype(jnp.int32)
    padded = (counts + MOE_BLOCK - 1) // MOE_BLOCK * MOE_BLOCK
    pend = jnp.cumsum(padded)
    pstart = pend - padded
    expert_ids = jnp.arange(n_experts, dtype=jnp.int32)
    dest = jnp.sum(jnp.where(top_idx[..., None] == expert_ids, pstart, 0), axis=-1) + rank
    n_assign = n * TOP_K
    n_blocks = -(-n_assign // MOE_BLOCK) + n_experts
    cap = n_blocks * MOE_BLOCK
    block_start = jnp.arange(n_blocks, dtype=jnp.int32) * MOE_BLOCK
    block_e = jnp.minimum(jnp.sum(pend[None, :] <= block_start[:, None], axis=1), n_experts - 1).astype(jnp.int32)
    n_used = (pend[-1:] // MOE_BLOCK).astype(jnp.int32)
    dest = dest.astype(jnp.int32)

    vend = (pstart + counts).astype(jnp.int32)
    block_vend = jnp.sum(jnp.where(block_e[:, None] == expert_ids, vend, 0), axis=1)
    n_valid = jnp.clip(block_vend - block_start, 0, MOE_BLOCK).astype(jnp.int32)

    xs = _dispatch(h2, dest, pend.astype(jnp.int32), vend, n_used, cap)
    blocks = jnp.arange(n_blocks, dtype=jnp.int32)
    later = (blocks[None, :] > blocks[:, None]) & (block_e[None, :] != block_e[:, None]) & (n_valid[None, :] > 0)
    first_later = jnp.min(jnp.where(later, blocks[None, :], n_blocks), axis=1)
    next_e = jnp.sum(jnp.where(blocks[None, :] == first_later[:, None], block_e[None, :] + 1, 0), axis=1) - 1
    y_sorted = _experts(xs, block_e, n_used, n_valid, next_e.astype(jnp.int32), w_up, b_up, w_down, b_down)
    out = _combine(y_sorted, dest, gates, x1, mod3, norm_w, seq)
    return out.reshape(bsz, seq, d)


def kernel(x, c, w_ada, b_ada, norm_w, w_in, conv_w, a_log, dt_bias, a_norm_w, rel_bias, w_out,
           w_router, b_router, w_up, b_up, w_down, b_down):
    for l in range(w_ada.shape[0]):
        x = _layer(x, c, w_ada[l], b_ada[l], norm_w[l], w_in[l], conv_w[l], a_log[l], dt_bias[l],
                   a_norm_w[l], rel_bias[l], w_out[l], w_router[l], b_router[l], w_up[l], b_up[l],
                   w_down[l], b_down[l])
    return x
```
